```python
import jax
import jax.numpy as jnp
from jax import lax
import numpy as np

D_MODEL = 2048
BATCH = 1
SEQ = 8192
DEPTH = 1
DEC_BATCH = 32
DEC_SEQ = 4
PAST_LEN = 8192
PAGE_SIZE = 128

D_CONV = D_MODEL // 2
D_ATTN = D_MODEL - D_CONV
HEAD_DIM = 64
N_HEADS = D_ATTN // HEAD_DIM
N_KV = 4
GROUP = N_HEADS // N_KV
KV_W = N_KV * HEAD_DIM
CONV_W = 3
CMP_LEN = 32
CMP_STRIDE = 16
CMP_HIDDEN = 2 * HEAD_DIM
SLC_BLOCK = 64
TOP_N = 16
WINDOW = 512
Q_BLOCK = 128
FORCE_SCORE = 1.0e4
ROPE_THETA = 10000.0
N_KV_SLOTS = 4
N_KEYS = 128
N_EXPERTS = N_KEYS * N_KEYS
PEER_HEADS = 8
PEER_TOPK = 16
D_KEY = 256
HALF_KEY = D_KEY // 2
TOK_BLOCK = 128
EPS = 1e-6
SPLITS = [D_CONV, 2 * D_CONV, 3 * D_CONV, 3 * D_CONV + D_ATTN, 3 * D_CONV + D_ATTN + 6 * KV_W]
IN_COLS = 3 * D_CONV + D_ATTN + 6 * KV_W + 3 * N_HEADS

kernel_name = 'hymba_shortconv_nsa_peer_adaln_step'


def rmsnorm(x):
    xf = x.astype(jnp.float32)
    y = xf * lax.rsqrt(jnp.mean(xf * xf, axis=-1, keepdims=True) + EPS)
    return y.astype(x.dtype)


def modulate(x, gain, shift, scale):
    return rmsnorm(x) * gain * (1.0 + scale) + shift


def ada_terms(c, w_ada, b_ada):
    m = jax.nn.silu(c) @ w_ada + b_ada
    return jnp.split(m[:, None, :], 6, axis=-1)


def rope(x, pos):
    half = HEAD_DIM // 2
    inv = ROPE_THETA ** (-jnp.arange(half, dtype=jnp.float32) / half)
    ang = pos.astype(jnp.float32)[:, None] * inv[None, :]
    cos = jnp.cos(ang)[:, None, :]
    sin = jnp.sin(ang)[:, None, :]
    xf = x.astype(jnp.float32)
    x1, x2 = xf[..., :half], xf[..., half:]
    return jnp.concatenate([x1 * cos - x2 * sin, x2 * cos + x1 * sin], axis=-1).astype(x.dtype)


def masked_softmax(s, mask):
    s = jnp.where(mask, s.astype(jnp.float32), -jnp.inf)
    m = jnp.max(s, axis=-1, keepdims=True)
    m = jnp.where(jnp.isfinite(m), m, 0.0)
    p = jnp.exp(s - m)
    return p / jnp.maximum(jnp.sum(p, axis=-1, keepdims=True), 1e-30)


def project(h, pos, w_in):
    B, T, _ = h.shape
    b_g, c_g, xv, q, kv, g = jnp.split(h @ w_in, SPLITS, axis=-1)
    q = rope(q.reshape(B, T, N_HEADS, HEAD_DIM), pos)
    kv = kv.reshape(B, T, 6, N_KV, HEAD_DIM)
    rows = jnp.stack([kv[:, :, 0], kv[:, :, 1], rope(kv[:, :, 2], pos), kv[:, :, 3]], axis=2)
    win = jnp.stack([rope(kv[:, :, 4], pos), kv[:, :, 5]], axis=2)
    gates = jax.nn.sigmoid(g).reshape(B, T, N_HEADS, 3)
    return b_g, c_g * xv, q, gates, rows, win


def short_conv(z_ctx, w, b):
    T = z_ctx.shape[1] - (CONV_W - 1)
    out = b
    for k in range(CONV_W):
        out = out + w[k] * z_ctx[:, k:k + T]
    return out


def compress(raw, pe, w1, b1, w2, b2, n_cmp):
    B = raw.shape[0]
    n_chunk = n_cmp + 1
    ch = raw[:, :n_chunk * CMP_STRIDE].reshape(B, n_chunk, CMP_STRIDE, N_KV, HEAD_DIM)
    ha = jnp.einsum('bcsgd,sdh->bcgh', ch + pe[:CMP_STRIDE, None, :], w1[:CMP_STRIDE])
    hb = jnp.einsum('bcsgd,sdh->bcgh', ch + pe[CMP_STRIDE:, None, :], w1[CMP_STRIDE:])
    hid = jax.nn.gelu(ha[:, :-1] + hb[:, 1:] + b1)
    return jnp.einsum('bjgh,hd->bjgd', hid, w2) + b2


def compressed_kv(k_raw, v_raw, pe, w1, b1, w2, b2):
    T = k_raw.shape[1]
    n_cmp = (T - CMP_LEN) // CMP_STRIDE + 1
    kc = compress(k_raw, pe[0], w1[0], b1[0], w2[0], b2[0], n_cmp)
    vc = compress(v_raw, pe[1], w1[1], b1[1], w2[1], b2[1], n_cmp)
    cpos = CMP_STRIDE * jnp.arange(n_cmp) + (CMP_LEN - 1)
    return rope(kc, cpos), vc


def cmp_to_slc(n_cmp, n_sel):
    j = np.arange(n_cmp)[:, None]
    b = np.arange(n_sel)[None, :]
    ov = (CMP_STRIDE * j < SLC_BLOCK * (b + 1)) & (CMP_STRIDE * j + CMP_LEN > SLC_BLOCK * b)
    return jnp.asarray(ov.astype(np.float32))


def slc_blocks(k, n_sel):
    B, T = k.shape[:2]
    k = jnp.pad(k, ((0, 0), (0, n_sel * SLC_BLOCK - T), (0, 0), (0, 0)))
    return k.reshape(B, n_sel, SLC_BLOCK, N_KV, HEAD_DIM).transpose(0, 3, 1, 2, 4)


def nsa_attend(q, qpos, kc, vc, ks_blk, vs_blk, kw, vw, wpos, gates):
    B, Tq = q.shape[:2]
    scale = HEAD_DIM ** -0.5
    qg = q.reshape(B, Tq, N_KV, GROUP, HEAD_DIM)
    n_cmp = kc.shape[1]
    n_sel = ks_blk.shape[2]
    cpos = CMP_STRIDE * jnp.arange(n_cmp) + (CMP_LEN - 1)
    s_c = jnp.einsum('bqgrd,bjgd->bgrqj', qg, kc) * scale
    p_c = masked_softmax(s_c, cpos[None, :] <= qpos[:, None])
    o_c = jnp.einsum('bgrqj,bjgd->bqgrd', p_c.astype(vc.dtype), vc)
    imp = jnp.einsum('bgrqj,js->bgqs', p_c, cmp_to_slc(n_cmp, n_sel))
    cur = qpos // SLC_BLOCK
    blk = jnp.arange(n_sel)
    forced = (blk[None, :] == 0) | (blk[None, :] == cur[:, None]) | (blk[None, :] == cur[:, None] - 1)
    valid = blk[None, :] <= cur[:, None]
    score = jnp.where(valid, jnp.where(forced, FORCE_SCORE, imp), -jnp.inf)
    n_top = min(TOP_N, n_sel)
    _, idx = lax.top_k(score, n_top)
    sel_ok = idx <= cur[None, None, :, None]
    bi = jnp.arange(B)[:, None, None, None]
    gi = jnp.arange(N_KV)[None, :, None, None]
    ksel = ks_blk[bi, gi, idx]
    vsel = vs_blk[bi, gi, idx]
    kpos = idx[..., None] * SLC_BLOCK + jnp.arange(SLC_BLOCK)
    m_s = (sel_ok[..., None] & (kpos <= qpos[None, None, :, None, None])).reshape(B, N_KV, Tq, -1)[:, :, None]
    s_s = jnp.einsum('bqgrd,bgqkpd->bgrqkp', qg, ksel).reshape(B, N_KV, GROUP, Tq, -1) * scale
    p_s = masked_softmax(s_s, m_s)
    o_s = jnp.einsum('bgrqn,bgqnd->bqgrd', p_s.astype(vsel.dtype), vsel.reshape(B, N_KV, Tq, -1, HEAD_DIM))
    s_w = jnp.einsum('bqgrd,bwgd->bgrqw', qg, kw) * scale
    dist = qpos[:, None] - wpos[None, :]
    p_w = masked_softmax(s_w, (dist >= 0) & (dist < WINDOW) & (wpos[None, :] >= 0))
    o_w = jnp.einsum('bgrqw,bwgd->bqgrd', p_w.astype(vw.dtype), vw)
    g = gates.reshape(B, Tq, N_KV, GROUP, 3)
    o = g[..., 0:1] * o_c + g[..., 1:2] * o_s + g[..., 2:3] * o_w
    return o.reshape(B, Tq, D_ATTN)


def peer(h, wq, keys, u_tab, v_tab):
    N = h.shape[0]
    n_blk = -(-N // TOK_BLOCK)
    hp = jnp.pad(h, ((0, n_blk * TOK_BLOCK - N), (0, 0))).reshape(n_blk, TOK_BLOCK, D_MODEL)

    def one_block(hb):
        q = (hb @ wq).reshape(TOK_BLOCK, PEER_HEADS, 2, HALF_KEY)
        s = jnp.einsum('thcd,hckd->thck', q, keys)
        s1, i1 = lax.top_k(s[:, :, 0], PEER_TOPK)
        s2, i2 = lax.top_k(s[:, :, 1], PEER_TOPK)
        cand = (s1[..., :, None] + s2[..., None, :]).reshape(TOK_BLOCK, PEER_HEADS, PEER_TOPK * PEER_TOPK)
        cidx = (i1[..., :, None] * N_KEYS + i2[..., None, :]).reshape(TOK_BLOCK, PEER_HEADS, PEER_TOPK * PEER_TOPK)
        top_s, sel = lax.top_k(cand, PEER_TOPK)
        e = jnp.take_along_axis(cidx, sel, axis=-1)
        g = jax.nn.softmax(top_s.astype(jnp.float32), axis=-1).astype(hb.dtype)
        a = jax.nn.gelu(jnp.einsum('td,thkd->thk', hb, u_tab[e]))
        return jnp.einsum('thk,thkd->td', g * a, v_tab[e])

    return lax.map(one_block, hp).reshape(n_blk * TOK_BLOCK, D_MODEL)[:N]


def finish_layer(x, conv_y, attn, ga1, sh2, sc2, ga2, p):
    x = x + ga1 * (jnp.concatenate([conv_y, attn], axis=-1) @ p['w_out'])
    B, T, _ = x.shape
    h2 = modulate(x, p['norm2_g'], sh2, sc2)
    f = peer(h2.reshape(B * T, D_MODEL), p['peer_wq'], p['peer_keys'], p['peer_u'], p['peer_v'])
    return x + ga2 * f.reshape(B, T, D_MODEL)


def prompt_layer(x, c, p):
    B, T, _ = x.shape
    sh1, sc1, ga1, sh2, sc2, ga2 = ada_terms(c, p['w_ada'], p['b_ada'])
    h = modulate(x, p['norm1_g'], sh1, sc1)
    pos = jnp.arange(T)
    b_g, zc, q, gates, rows, win = project(h, pos, p['w_in'])
    conv_y = b_g * short_conv(jnp.pad(zc, ((0, 0), (CONV_W - 1, 0), (0, 0))), p['conv_w'], p['conv_b'])
    kc, vc = compressed_kv(rows[:, :, 0], rows[:, :, 1], p['cmp_pe'], p['cmp_w1'], p['cmp_b1'], p['cmp_w2'], p['cmp_b2'])
    n_sel = -(-T // SLC_BLOCK)
    ksb = slc_blocks(rows[:, :, 2], n_sel)
    vsb = slc_blocks(rows[:, :, 3], n_sel)
    win_pad = jnp.pad(win, ((0, 0), (WINDOW, 0), (0, 0), (0, 0), (0, 0)))

    def attend_block(i):
        start = i * Q_BLOCK
        qb = lax.dynamic_slice_in_dim(q, start, Q_BLOCK, axis=1)
        gb = lax.dynamic_slice_in_dim(gates, start, Q_BLOCK, axis=1)
        wb = lax.dynamic_slice_in_dim(win_pad, start, WINDOW + Q_BLOCK, axis=1)
        qpos = start + jnp.arange(Q_BLOCK)
        wpos = start - WINDOW + jnp.arange(WINDOW + Q_BLOCK)
        return nsa_attend(qb, qpos, kc, vc, ksb, vsb, wb[:, :, 0], wb[:, :, 1], wpos, gb)

    attn = lax.map(attend_block, jnp.arange(T // Q_BLOCK))
    attn = attn.transpose(1, 0, 2, 3).reshape(B, T, D_ATTN)
    x = finish_layer(x, conv_y, attn, ga1, sh2, sc2, ga2, p)
    return x, rows, win[:, T - min(WINDOW, T):], zc[:, T - (CONV_W - 1):]


def sample_layer(x, c, cache_kv, page_table, win_buf, conv_buf, p):
    B, T, _ = x.shape
    n_pages = page_table.shape[1]
    past_len = n_pages * cache_kv.shape[1]
    sh1, sc1, ga1, sh2, sc2, ga2 = ada_terms(c, p['w_ada'], p['b_ada'])
    h = modulate(x, p['norm1_g'], sh1, sc1)
    pos = past_len + jnp.arange(T)
    b_g, zc, q, gates, rows, win = project(h, pos, p['w_in'])
    z_ctx = jnp.concatenate([conv_buf, zc], axis=1)
    conv_y = b_g * short_conv(z_ctx, p['conv_w'], p['conv_b'])
    past = cache_kv[page_table].reshape(B, past_len, N_KV_SLOTS, N_KV, HEAD_DIM)
    full = jnp.concatenate([past, rows], axis=1)
    kc, vc = compressed_kv(full[:, :, 0], full[:, :, 1], p['cmp_pe'], p['cmp_w1'], p['cmp_b1'], p['cmp_w2'], p['cmp_b2'])
    n_sel = -(-(past_len + T) // SLC_BLOCK)
    ksb = slc_blocks(full[:, :, 2], n_sel)
    vsb = slc_blocks(full[:, :, 3], n_sel)
    wkeep = win_buf.shape[1]
    win_ctx = jnp.concatenate([win_buf, win], axis=1)
    wpos = past_len - wkeep + jnp.arange(wkeep + T)
    attn = nsa_attend(q, pos, kc, vc, ksb, vsb, win_ctx[:, :, 0], win_ctx[:, :, 1], wpos, gates)
    x = finish_layer(x, conv_y, attn, ga1, sh2, sc2, ga2, p)
    return x, rows, win_ctx[:, T:], z_ctx[:, T:]


def setup_inputs(seed: int = 0) -> dict:
    key = jax.random.key(seed)
    ks = jax.random.split(key, 32)
    n_pages = PAST_LEN // PAGE_SIZE
    n_used = DEC_BATCH * n_pages
    n_pool = n_used + max(1, n_used // 4)
    wkeep = min(WINDOW, PAST_LEN)

    def nrm(k, shape, s):
        return s * jax.random.normal(k, shape, jnp.float32)

    page_table = jax.random.permutation(ks[7], n_pool)[:n_used].reshape(DEC_BATCH, n_pages).astype(jnp.int32)
    return {
        'x_prompt': nrm(ks[0], (BATCH, SEQ, D_MODEL), 1.0),
        'x_sample': nrm(ks[1], (DEC_BATCH, DEC_SEQ, D_MODEL), 1.0),
        'c_prompt': nrm(ks[2], (BATCH, D_MODEL), 1.0),
        'c_sample': nrm(ks[3], (DEC_BATCH, D_MODEL), 1.0),
        'cache_kv': nrm(ks[4], (DEPTH, n_pool, PAGE_SIZE, N_KV_SLOTS, N_KV, HEAD_DIM), 1.0),
        'state_win': nrm(ks[5], (DEPTH, DEC_BATCH, wkeep, 2, N_KV, HEAD_DIM), 1.0),
        'state_conv': nrm(ks[6], (DEPTH, DEC_BATCH, CONV_W - 1, D_CONV), 1.0),
        'page_table': page_table,
        'w_ada': nrm(ks[8], (DEPTH, D_MODEL, 6 * D_MODEL), 0.5 * D_MODEL ** -0.5),
        'b_ada': nrm(ks[9], (DEPTH, 6 * D_MODEL), 0.01),
        'norm1_g': 1.0 + nrm(ks[10], (DEPTH, D_MODEL), 0.02),
        'norm2_g': 1.0 + nrm(ks[11], (DEPTH, D_MODEL), 0.02),
        'w_in': nrm(ks[12], (DEPTH, D_MODEL, IN_COLS), D_MODEL ** -0.5),
        'conv_w': nrm(ks[13], (DEPTH, CONV_W, D_CONV), 0.5),
        'conv_b': nrm(ks[14], (DEPTH, D_CONV), 0.01),
        'cmp_pe': nrm(ks[15], (DEPTH, 2, CMP_LEN, HEAD_DIM), 0.02),
        'cmp_w1': nrm(ks[16], (DEPTH, 2, CMP_LEN, HEAD_DIM, CMP_HIDDEN), (CMP_LEN * HEAD_DIM) ** -0.5),
        'cmp_b1': nrm(ks[17], (DEPTH, 2, CMP_HIDDEN), 0.01),
        'cmp_w2': nrm(ks[18], (DEPTH, 2, CMP_HIDDEN, HEAD_DIM), CMP_HIDDEN ** -0.5),
        'cmp_b2': nrm(ks[19], (DEPTH, 2, HEAD_DIM), 0.01),
        'w_out': nrm(ks[20], (DEPTH, D_MODEL, D_MODEL), D_MODEL ** -0.5),
        'peer_wq': nrm(ks[21], (DEPTH, D_MODEL, PEER_HEADS * D_KEY), D_MODEL ** -0.5),
        'peer_keys': nrm(ks[22], (DEPTH, PEER_HEADS, 2, N_KEYS, HALF_KEY), HALF_KEY ** -0.5),
        'peer_u': nrm(ks[23], (DEPTH, N_EXPERTS, D_MODEL), D_MODEL ** -0.5),
        'peer_v': nrm(ks[24], (DEPTH, N_EXPERTS, D_MODEL), PEER_HEADS ** -0.5),
        'final_g': 1.0 + nrm(ks[25], (D_MODEL,), 0.02),
    }


def reference(x_prompt, x_sample, c_prompt, c_sample, cache_kv, state_win, state_conv, page_table,
              w_ada, b_ada, norm1_g, norm2_g, w_in, conv_w, conv_b, cmp_pe, cmp_w1, cmp_b1, cmp_w2, cmp_b2,
              w_out, peer_wq, peer_keys, peer_u, peer_v, final_g):
    xp, xs = x_prompt, x_sample
    rows_p, rows_s, win_p, win_s, conv_p, conv_s = [], [], [], [], [], []
    for l in range(DEPTH):
        p = {'w_ada': w_ada[l], 'b_ada': b_ada[l], 'norm1_g': norm1_g[l], 'norm2_g': norm2_g[l],
             'w_in': w_in[l], 'conv_w': conv_w[l], 'conv_b': conv_b[l], 'cmp_pe': cmp_pe[l],
             'cmp_w1': cmp_w1[l], 'cmp_b1': cmp_b1[l], 'cmp_w2': cmp_w2[l], 'cmp_b2': cmp_b2[l],
             'w_out': w_out[l], 'peer_wq': peer_wq[l], 'peer_keys': peer_keys[l],
             'peer_u': peer_u[l], 'peer_v': peer_v[l]}
        xp, rp, wp, cp = prompt_layer(xp, c_prompt, p)
        xs, rs, ws, cs = sample_layer(xs, c_sample, cache_kv[l], page_table, state_win[l], state_conv[l], p)
        rows_p.append(rp)
        rows_s.append(rs)
        win_p.append(wp)
        win_s.append(ws)
        conv_p.append(cp)
        conv_s.append(cs)
    y_prompt = rmsnorm(xp) * final_g
    y_sample = rmsnorm(xs) * final_g
    kv_rows_prompt = jnp.stack(rows_p)
    kv_rows_sample = jnp.stack(rows_s)
    win_prompt = jnp.stack(win_p)
    win_sample = jnp.stack(win_s)
    conv_prompt = jnp.stack(conv_p)
    conv_sample = jnp.stack(conv_s)
    return (y_prompt, y_sample, kv_rows_prompt, kv_rows_sample, win_prompt, win_sample, conv_prompt, conv_sample)
```

```python
import functools

import numpy as np
import jax
import jax.numpy as jnp
from jax import lax
from jax.experimental import pallas as pl
from jax.experimental.pallas import tpu as pltpu

F32 = jnp.float32
BF16 = jnp.bfloat16

HEAD_DIM = 64
N_KV = 4
GROUP = 4
N_HEADS = N_KV * GROUP
KV_W = N_KV * HEAD_DIM
D_ATTN = N_HEADS * HEAD_DIM
CMP_STRIDE = 16
SLC_BLOCK = 64
TOP_N = 16
WINDOW = 512
Q_BLOCK = 128
PAGE = 128
FORCE_SCORE = 1.0e4
ROPE_THETA = 10000.0
N_KEYS = 128
PEER_HEADS = 8
PEER_TOPK = 16
EPS = 1e-6
LANES = 128
PAGES_PER_STEP = 8
KEY_TILE = 512
V7X_VMEM_LIMIT = 58 * 1024 * 1024
NEG_BIG = -1e30


def _cparams(*sem):
    return pltpu.CompilerParams(dimension_semantics=sem, vmem_limit_bytes=V7X_VMEM_LIMIT)


def _vmem():
    return pl.BlockSpec(memory_space=pltpu.VMEM)


def _dot(a, b):
    return jnp.dot(a, b, preferred_element_type=F32)


def _dot_nt(a, b):
    return lax.dot_general(a, b, (((1,), (1,)), ((), ())), preferred_element_type=F32)


def _split_dot(x, m):
    hi = x.astype(BF16)
    lo = (x - hi.astype(F32)).astype(BF16)
    return _dot(hi, m) + _dot(lo, m)


def _rope(x, cos, sinsg):
    w = x.shape[1]
    n = w // LANES
    c = jnp.concatenate([cos] * n, axis=1) if n > 1 else cos
    s = jnp.concatenate([sinsg] * n, axis=1) if n > 1 else sinsg
    lane = lax.broadcasted_iota(jnp.int32, x.shape, 1)
    first = (lane % HEAD_DIM) < (HEAD_DIM // 2)
    partner = jnp.where(first, pltpu.roll(x, w - HEAD_DIM // 2, 1), pltpu.roll(x, HEAD_DIM // 2, 1))
    return x * c + partner * s


def _rope_tables(pos):
    half = HEAD_DIM // 2
    inv = ROPE_THETA ** (-jnp.arange(half, dtype=F32) / half)
    ang = pos.astype(F32)[:, None] * inv[None, :]
    cos = jnp.cos(ang)
    sin = jnp.sin(ang)
    cos = jnp.concatenate([cos, cos, cos, cos], axis=1)
    sinsg = jnp.concatenate([-sin, sin, -sin, sin], axis=1)
    return cos, sinsg


def _rms_mod(xf, g, sc, sh):
    y = xf * lax.rsqrt(jnp.mean(xf * xf, axis=-1, keepdims=True) + EPS)
    return y * g * (1.0 + sc) + sh


def _ada_kernel(c_ref, w_ref, b_ref, o_ref):
    s = jax.nn.silu(c_ref[...]).astype(BF16)
    o_ref[...] = _dot(s, w_ref[...].astype(BF16)) + b_ref[...]


def _ada(c_all, w_ada, b_ada):
    r, d = c_all.shape
    n = w_ada.shape[1]
    tn = 1024
    return pl.pallas_call(
        _ada_kernel,
        grid=(n // tn,),
        in_specs=[pl.BlockSpec((r, d), lambda j: (0, 0)),
                  pl.BlockSpec((d, tn), lambda j: (0, j)),
                  pl.BlockSpec((1, tn), lambda j: (0, j))],
        out_specs=pl.BlockSpec((r, tn), lambda j: (0, j)),
        out_shape=jax.ShapeDtypeStruct((r, n), F32),
        compiler_params=_cparams("arbitrary"),
        name="ada",
    )(c_all, w_ada, b_ada.reshape(1, n))


def _proj_kernel(*refs, sample, tm, dc):
    if sample:
        (x_ref, g1_ref, sc_ref, sh_ref, w_ref, cos_ref, sin_ref, cw_ref, cb_ref, a1_ref, a2_ref,
         convy_ref, q_ref, rows_ref, win_ref, kvb_ref, gates_ref, zc_ref) = refs
    else:
        (x_ref, g1_ref, sc_ref, sh_ref, w_ref, cos_ref, sin_ref, cw_ref, cb_ref,
         convy_ref, q_ref, rows_ref, win_ref, kvb_ref, gates_ref, zc_ref, carry_ref) = refs
    h = _rms_mod(x_ref[...], g1_ref[...], sc_ref[...], sh_ref[...]).astype(BF16)

    def mm(a, b):
        return _dot(h, w_ref[:, a:b])

    cos = cos_ref[...]
    sin = sin_ref[...]
    q0 = 3 * dc
    kv0 = q0 + D_ATTN
    g0 = kv0 + 6 * KV_W

    b_g = mm(0, dc)
    zc = mm(dc, 2 * dc) * mm(2 * dc, 3 * dc)
    row = lax.broadcasted_iota(jnp.int32, zc.shape, 0)
    if sample:
        t = row % 4
        zm1 = jnp.where(t >= 1, pltpu.roll(zc, 1, 0), a1_ref[...])
        zm2 = jnp.where(t >= 2, pltpu.roll(zc, 2, 0), a2_ref[...])
        zc_ref[...] = zc
    else:
        @pl.when(pl.program_id(0) == 0)
        def _():
            carry_ref[...] = jnp.zeros_like(carry_ref)
        p1 = carry_ref[7:8, :]
        p2 = carry_ref[6:7, :]
        zm1 = jnp.where(row == 0, p1, pltpu.roll(zc, 1, 0))
        zm2 = jnp.where(row == 0, p2, jnp.where(row == 1, p1, pltpu.roll(zc, 2, 0)))
        carry_ref[...] = zc[tm - 8:tm, :]
        zc_ref[...] = zc[tm - 8:tm, :]
    conv = cb_ref[...] + cw_ref[0:1, :] * zm2
    conv = conv + cw_ref[1:2, :] * zm1
    conv = conv + cw_ref[2:3, :] * zc
    convy_ref[...] = (b_g * conv).astype(BF16)

    q = _rope(mm(q0, kv0), cos, sin) * (HEAD_DIM ** -0.5)
    q_ref[...] = q.astype(BF16)

    kv = mm(kv0, g0)
    k_cmp = kv[:, 0:KV_W]
    v_cmp = kv[:, KV_W:2 * KV_W]
    k_slc = _rope(kv[:, 2 * KV_W:3 * KV_W], cos, sin)
    v_slc = kv[:, 3 * KV_W:4 * KV_W]
    k_win = _rope(kv[:, 4 * KV_W:5 * KV_W], cos, sin)
    v_win = kv[:, 5 * KV_W:6 * KV_W]
    rows_ref[:, 0:KV_W] = k_cmp
    rows_ref[:, KV_W:2 * KV_W] = v_cmp
    rows_ref[:, 2 * KV_W:3 * KV_W] = k_slc
    rows_ref[:, 3 * KV_W:4 * KV_W] = v_slc
    win_ref[:, 0:KV_W] = k_win
    win_ref[:, KV_W:2 * KV_W] = v_win
    kvb_ref[:, 0:KV_W] = k_slc.astype(BF16)
    kvb_ref[:, KV_W:2 * KV_W] = v_slc.astype(BF16)
    kvb_ref[:, 2 * KV_W:3 * KV_W] = k_win.astype(BF16)
    kvb_ref[:, 3 * KV_W:4 * KV_W] = v_win.astype(BF16)
    gates_ref[...] = jax.nn.sigmoid(mm(g0, g0 + LANES))


def _proj(x, g1, sc, sh, w_perm, cos, sin, conv_w, conv_b, a1=None, a2=None):
    sample = a1 is not None
    r, d = x.shape
    dc = d // 2
    tm = r if sample else 256
    nw = w_perm.shape[1]
    row_spec = lambda wdt: pl.BlockSpec((tm, wdt), lambda i: (i, 0))
    full = lambda a: pl.BlockSpec(a.shape, lambda i: (0,) * a.ndim)
    mod_spec = row_spec(d) if sample else pl.BlockSpec((1, d), lambda i: (0, 0))
    in_specs = [row_spec(d), full(g1), mod_spec, mod_spec, _vmem(), row_spec(LANES), row_spec(LANES),
                full(conv_w), full(conv_b)]
    args = [x, g1, sc, sh, w_perm, cos, sin, conv_w, conv_b]
    scratch = []
    if sample:
        in_specs += [row_spec(dc), row_spec(dc)]
        args += [a1, a2]
        zc_shape = jax.ShapeDtypeStruct((r, dc), F32)
        zc_spec = row_spec(dc)
    else:
        zc_shape = jax.ShapeDtypeStruct((8, dc), F32)
        zc_spec = pl.BlockSpec((8, dc), lambda i: (0, 0))
        scratch = [pltpu.VMEM((8, dc), F32)]
    out_shape = (jax.ShapeDtypeStruct((r, dc), BF16), jax.ShapeDtypeStruct((r, D_ATTN), BF16),
                 jax.ShapeDtypeStruct((r, 4 * KV_W), F32), jax.ShapeDtypeStruct((r, 2 * KV_W), F32),
                 jax.ShapeDtypeStruct((r, 4 * KV_W), BF16), jax.ShapeDtypeStruct((r, LANES), F32), zc_shape)
    out_specs = (row_spec(dc), row_spec(D_ATTN), row_spec(4 * KV_W), row_spec(2 * KV_W),
                 row_spec(4 * KV_W), row_spec(LANES), zc_spec)
    return pl.pallas_call(
        functools.partial(_proj_kernel, sample=sample, tm=tm, dc=dc),
        grid=(r // tm,),
        in_specs=in_specs, out_specs=out_specs, out_shape=out_shape, scratch_shapes=scratch,
        compiler_params=_cparams("arbitrary"),
        name="proj_sample" if sample else "proj_prompt",
    )(*args)


def _cmp_kernel(pt_ref, *refs):
    npg = PAGES_PER_STEP
    pages = refs[:npg]
    (perm_ref, w1a_ref, w1b_ref, pea_ref, peb_ref, b1_ref, w2_ref, b2_ref, cos_ref, sin_ref,
     kc_ref, vc_ref, xa_s, xb_s, hprev_s) = refs[npg:]
    del pt_ref
    nch = PAGE // CMP_STRIDE
    rows = npg * nch

    @pl.when(pl.program_id(1) == 0)
    def _():
        hprev_s[...] = jnp.zeros_like(hprev_s)

    lane = lax.broadcasted_iota(jnp.int32, (nch, 2 * KV_W), 1)
    low = (lane % LANES) < HEAD_DIM
    perm = perm_ref[...]
    for k in range(npg):
        page = pages[k][...]
        for pe_ref, x_s in ((pea_ref, xa_s), (peb_ref, xb_s)):
            y = _dot(perm, (page + pe_ref[...]).astype(BF16))
            for p in range(CMP_STRIDE // 2):
                ev = y[2 * p * nch:(2 * p + 1) * nch]
                od = y[(2 * p + 1) * nch:(2 * p + 2) * nch]
                a = jnp.where(low, ev, pltpu.roll(od, HEAD_DIM, 1))
                b = jnp.where(low, pltpu.roll(ev, 2 * KV_W - HEAD_DIM, 1), od)
                for j in range(4):
                    kv, ge = j // 2, 2 * (j % 2)
                    rsl = slice(nch * k, nch * (k + 1))
                    csl = slice(LANES * p, LANES * (p + 1))
                    x_s[kv, ge, rsl, csl] = a[:, LANES * j:LANES * (j + 1)]
                    x_s[kv, ge + 1, rsl, csl] = b[:, LANES * j:LANES * (j + 1)]

    row = lax.broadcasted_iota(jnp.int32, (N_KV * rows, 1), 0)
    for kv in range(2):
        ha = _dot(xa_s[kv].reshape(N_KV * rows, CMP_STRIDE * HEAD_DIM).astype(BF16), w1a_ref[kv])
        hb = _dot(xb_s[kv].reshape(N_KV * rows, CMP_STRIDE * HEAD_DIM).astype(BF16), w1b_ref[kv])
        hp = hprev_s[kv]
        prev = jnp.where(row % rows == 0, pltpu.roll(hp, N_KV * rows - (rows - 1), 0), pltpu.roll(ha, 1, 0))
        hprev_s[kv] = ha
        hid = jax.nn.gelu(prev + hb + b1_ref[kv])
        hcat = jnp.concatenate([hid[rows * g:rows * (g + 1)] for g in range(N_KV)], axis=1).astype(BF16)
        o = _dot(hcat, w2_ref[kv]) + b2_ref[kv]
        if kv == 0:
            kc_ref[...] = _rope(o, cos_ref[...], sin_ref[...]).astype(BF16)
        else:
            vc_ref[...] = o.astype(BF16)


def _compress(pool, page_table, cw):
    b, n_pages = page_table.shape
    npg = PAGES_PER_STEP
    nsteps = n_pages // npg
    rows = npg * (PAGE // CMP_STRIDE)
    nc = nsteps * rows
    cpos = CMP_STRIDE * jnp.arange(nc) + (CMP_STRIDE - 1)
    cos, sin = _rope_tables(cpos)

    def page_spec(k):
        return pl.BlockSpec((None, PAGE, 2 * KV_W), lambda bi, i, pt: (pt[bi, i * npg + k], 0, 0))

    const = lambda a: pl.BlockSpec(a.shape, lambda bi, i, pt: (0,) * a.ndim)
    nch = PAGE // CMP_STRIDE
    perm = np.zeros((PAGE, PAGE), np.float32)
    for s in range(CMP_STRIDE):
        for c in range(nch):
            perm[s * nch + c, CMP_STRIDE * c + s] = 1.0
    weights = [jnp.asarray(perm, BF16), cw["w1a"], cw["w1b"], cw["pea"], cw["peb"], cw["b1"], cw["w2bd"], cw["b2"]]
    tab_spec = pl.BlockSpec((rows, LANES), lambda bi, i, pt: (i, 0))
    out_spec = pl.BlockSpec((None, rows, KV_W), lambda bi, i, pt: (bi, i, 0))
    x_scratch = pltpu.VMEM((2, N_KV, rows, CMP_STRIDE * HEAD_DIM), F32)
    grid_spec = pltpu.PrefetchScalarGridSpec(
        num_scalar_prefetch=1,
        grid=(b, nsteps),
        in_specs=[page_spec(k) for k in range(npg)] + [const(a) for a in weights] + [tab_spec, tab_spec],
        out_specs=(out_spec, out_spec),
        scratch_shapes=[x_scratch, x_scratch, pltpu.VMEM((2, N_KV * rows, 2 * HEAD_DIM), F32)],
    )
    return pl.pallas_call(
        _cmp_kernel,
        grid_spec=grid_spec,
        out_shape=(jax.ShapeDtypeStruct((b, nc, KV_W), BF16), jax.ShapeDtypeStruct((b, nc, KV_W), BF16)),
        compiler_params=_cparams("arbitrary", "arbitrary"),
        name="compress",
    )(page_table, *([pool] * npg), *weights, cos, sin)


def _prep_cmp_weights(cmp_pe, cmp_w1, cmp_b1, cmp_w2, cmp_b2):
    s = CMP_STRIDE
    w2bd = jnp.zeros((2, N_KV * 2 * HEAD_DIM, KV_W), F32)
    for g in range(N_KV):
        w2bd = w2bd.at[:, g * 2 * HEAD_DIM:(g + 1) * 2 * HEAD_DIM, g * HEAD_DIM:(g + 1) * HEAD_DIM].set(cmp_w2)
    def pe_page(pe_half):
        t = jnp.broadcast_to(pe_half.transpose(1, 0, 2)[:, :, None, :], (s, 2, N_KV, HEAD_DIM))
        return jnp.tile(t.reshape(s, 2 * KV_W), (PAGE // s, 1))

    return {
        "w1a": cmp_w1[:, :s].reshape(2, s * HEAD_DIM, 2 * HEAD_DIM).astype(BF16),
        "w1b": cmp_w1[:, s:].reshape(2, s * HEAD_DIM, 2 * HEAD_DIM).astype(BF16),
        "pea": pe_page(cmp_pe[:, :s]),
        "peb": pe_page(cmp_pe[:, s:]),
        "b1": cmp_b1.reshape(2, 1, 2 * HEAD_DIM),
        "w2bd": w2bd.astype(BF16),
        "b2": jnp.tile(cmp_b2, (1, N_KV)).reshape(2, 1, KV_W),
    }


def _cmp_to_slc(nc, ns):
    j = np.arange(nc)[:, None] - 1
    b = np.arange(ns)[None, :]
    ov = (CMP_STRIDE * j < SLC_BLOCK * (b + 1)) & (CMP_STRIDE * j + 2 * CMP_STRIDE > SLC_BLOCK * b) & (j >= 0)
    return ov.astype(np.float32)


def _gate_expand():
    ex = np.zeros((LANES, 3 * GROUP * KV_W), np.float32)
    for g in range(N_KV):
        for r in range(GROUP):
            for br in range(3):
                c0 = ((br * GROUP + r) * N_KV + g) * HEAD_DIM
                ex[(GROUP * g + r) * 3 + br, c0:c0 + HEAD_DIM] = 1.0
    return ex


def _masked_softmax(s, mask):
    s = jnp.where(mask, s, -jnp.inf)
    m = jnp.max(s, axis=-1, keepdims=True)
    m = jnp.where(m > -jnp.inf, m, 0.0)
    p = jnp.exp(s - m)
    return p / jnp.maximum(jnp.sum(p, axis=-1, keepdims=True), 1e-30)


def _select_blocks(score, n_top):
    lane = lax.broadcasted_iota(jnp.int32, score.shape, 1)
    nb = score.shape[1]
    sel = jnp.zeros(score.shape, F32)
    cur = score
    for _ in range(n_top):
        mx = jnp.max(cur, axis=-1, keepdims=True)
        first = jnp.min(jnp.where(cur == mx, lane, nb), axis=-1, keepdims=True)
        hit = (lane == first) & (mx > -jnp.inf)
        sel = jnp.where(hit, 1.0, sel)
        cur = jnp.where(lane == first, -jnp.inf, cur)
    return sel


def _nsa_prompt_kernel(q_ref, gates_ref, kc_ref, vc_ref, kvb_ref, wpad_ref, m_ref, e_ref, ex_ref,
                       out_ref, qm_s, sel_s, *, nc, ns):
    i = pl.program_id(0)
    tq = Q_BLOCK
    rq = GROUP * tq
    qpos = i * tq + lax.broadcasted_iota(jnp.int32, (tq, 1), 0)
    qpos4 = jnp.concatenate([qpos] * GROUP, axis=0)
    lane_g = lax.broadcasted_iota(jnp.int32, (tq, KV_W), 1) // HEAD_DIM
    q = q_ref[...]
    for g in range(N_KV):
        for r in range(GROUP):
            qm_s[g, r * tq:(r + 1) * tq, :] = jnp.where(lane_g == g, q[:, r * KV_W:(r + 1) * KV_W], 0).astype(BF16)

    lane_g4 = lax.broadcasted_iota(jnp.int32, (rq, KV_W), 1) // HEAD_DIM

    kc = kc_ref[...]
    vc = vc_ref[...]
    cidx = lax.broadcasted_iota(jnp.int32, (1, nc), 1)
    cvalid = (cidx >= 1) & (CMP_STRIDE * cidx + (CMP_STRIDE - 1) <= qpos4)
    o_c = jnp.zeros((rq, KV_W), F32)
    imps = []
    for g in range(N_KV):
        p = _masked_softmax(_dot_nt(qm_s[g], kc), cvalid)
        o_c = o_c + jnp.where(lane_g4 == g, _dot(p.astype(BF16), vc), 0.0)
        psum = p[0:tq] + p[tq:2 * tq] + p[2 * tq:3 * tq] + p[3 * tq:4 * tq]
        imps.append(_split_dot(psum, m_ref[...]))
    imp = jnp.concatenate(imps, axis=0)
    blk = lax.broadcasted_iota(jnp.int32, (1, ns), 1)
    cur = qpos4 // SLC_BLOCK
    forced = (blk == 0) | (blk == cur) | (blk == cur - 1)
    score = jnp.where(blk <= cur, jnp.where(forced, FORCE_SCORE, imp), -jnp.inf)
    sel = _select_blocks(score, min(TOP_N, ns))
    for g in range(N_KV):
        sel_s[g] = sel[g * tq:(g + 1) * tq].astype(BF16)

    n_kt = (i * tq + tq - 1) // KEY_TILE + 1
    kcol = lax.broadcasted_iota(jnp.int32, (1, KEY_TILE), 1)

    def body(kt, carry):
        ms, ls, accs = carry
        k0 = pl.multiple_of(kt * KEY_TILE, KEY_TILE)
        kk = kvb_ref[pl.ds(k0, KEY_TILE), 0:KV_W]
        vv = kvb_ref[pl.ds(k0, KEY_TILE), KV_W:2 * KV_W]
        et = e_ref[kt]
        causal = (k0 + kcol) <= qpos
        nm, nl, na = [], [], []
        for g in range(N_KV):
            mk = (_dot(sel_s[g], et) > 0.5) & causal
            mk4 = jnp.concatenate([mk] * GROUP, axis=0)
            s = jnp.where(mk4, _dot_nt(qm_s[g], kk), NEG_BIG)
            m_new = jnp.maximum(ms[g], jnp.max(s, axis=-1, keepdims=True))
            alpha = jnp.exp(ms[g] - m_new)
            p = jnp.exp(s - m_new)
            nl.append(alpha * ls[g] + jnp.sum(p, axis=-1, keepdims=True))
            na.append(alpha * accs[g] + _dot(p.astype(BF16), vv))
            nm.append(m_new)
        return tuple(nm), tuple(nl), tuple(na)

    init = (tuple(jnp.full((rq, 1), NEG_BIG, F32) for _ in range(N_KV)),
            tuple(jnp.zeros((rq, 1), F32) for _ in range(N_KV)),
            tuple(jnp.zeros((rq, KV_W), F32) for _ in range(N_KV)))
    _, ls, accs = lax.fori_loop(0, n_kt, body, init)
    o_s = jnp.zeros((rq, KV_W), F32)
    for g in range(N_KV):
        o_s = o_s + jnp.where(lane_g4 == g, accs[g] / ls[g], 0.0)

    w0 = pl.multiple_of(i * tq, tq)
    kw = wpad_ref[pl.ds(w0, WINDOW + tq), 0:KV_W]
    vw = wpad_ref[pl.ds(w0, WINDOW + tq), KV_W:2 * KV_W]
    wpos = i * tq - WINDOW + lax.broadcasted_iota(jnp.int32, (1, WINDOW + tq), 1)
    dist = qpos4 - wpos
    wmask = (dist >= 0) & (dist < WINDOW) & (wpos >= 0)
    o_w = jnp.zeros((rq, KV_W), F32)
    for g in range(N_KV):
        p = _masked_softmax(_dot_nt(qm_s[g], kw), wmask)
        o_w = o_w + jnp.where(lane_g4 == g, _dot(p.astype(BF16), vw), 0.0)

    gx = _split_dot(gates_ref[...], ex_ref[...])
    for r in range(GROUP):
        rs = slice(r * tq, (r + 1) * tq)
        gcol = lambda br: gx[:, (br * GROUP + r) * KV_W:(br * GROUP + r + 1) * KV_W]
        o = gcol(0) * o_c[rs] + gcol(1) * o_s[rs] + gcol(2) * o_w[rs]
        out_ref[:, r * KV_W:(r + 1) * KV_W] = o.astype(BF16)


def _nsa_prompt(q, gates, kc, vc, kvb, wpad):
    t = q.shape[0]
    nc = kc.shape[0]
    ns = t // SLC_BLOCK
    nt = t // KEY_TILE
    m = jnp.asarray(_cmp_to_slc(nc, ns), BF16)
    e = np.zeros((nt, ns, KEY_TILE), np.float32)
    for kt in range(nt):
        for j in range(KEY_TILE):
            e[kt, (kt * KEY_TILE + j) // SLC_BLOCK, j] = 1.0
    e = jnp.asarray(e, BF16)
    ex = jnp.asarray(_gate_expand(), BF16)
    full = lambda a: pl.BlockSpec(a.shape, lambda i: (0,) * a.ndim)
    return pl.pallas_call(
        functools.partial(_nsa_prompt_kernel, nc=nc, ns=ns),
        grid=(t // Q_BLOCK,),
        in_specs=[pl.BlockSpec((Q_BLOCK, D_ATTN), lambda i: (i, 0)),
                  pl.BlockSpec((Q_BLOCK, LANES), lambda i: (i, 0)),
                  full(kc), full(vc), _vmem(), _vmem(), full(m), _vmem(), full(ex)],
        out_specs=pl.BlockSpec((Q_BLOCK, D_ATTN), lambda i: (i, 0)),
        out_shape=jax.ShapeDtypeStruct((t, D_ATTN), BF16),
        scratch_shapes=[pltpu.VMEM((N_KV, GROUP * Q_BLOCK, KV_W), BF16),
                        pltpu.VMEM((N_KV, Q_BLOCK, ns), BF16)],
        compiler_params=_cparams("arbitrary"),
        name="nsa_prompt",
    )(q, gates, kc, vc, kvb, wpad, m, e, ex)


def _nsa_sample_kernel(pt_ref, *refs, nc, ns, nsteps):
    npg = PAGES_PER_STEP
    pages = refs[:npg]
    (qrep_ref, gx_ref, kc_ref, vc_ref, sw_ref, new_ref, m_ref, e_ref, en_ref,
     out_ref, qall_s, sel_s, oc_s, ow_s, m_s, l_s, acc_s) = refs[npg:]
    del pt_ref
    i = pl.program_id(1)
    nr = GROUP * N_KV * 4
    row = lax.broadcasted_iota(jnp.int32, (nr, 1), 0)
    row_t = row % 4
    row_g = (row // 4) % N_KV
    lane_g = lax.broadcasted_iota(jnp.int32, (1, KV_W), 1) // HEAD_DIM
    own = row_g == lane_g
    new = new_ref[...]
    ucol = lax.broadcasted_iota(jnp.int32, (1, 8), 1)
    new_ok = (ucol <= row_t) & (ucol < 4)

    @pl.when(i == 0)
    def _():
        qall = jnp.where(own, qrep_ref[...], 0).astype(BF16)
        qall_s[...] = qall
        cidx = lax.broadcasted_iota(jnp.int32, (1, nc), 1)
        p = _masked_softmax(_dot_nt(qall, kc_ref[...]), cidx >= 1)
        oc_s[...] = _dot(p.astype(BF16), vc_ref[...])
        n16 = N_KV * 4
        psum = p[0:n16] + p[n16:2 * n16] + p[2 * n16:3 * n16] + p[3 * n16:4 * n16]
        imp = _split_dot(psum, m_ref[...])
        nb = imp.shape[1]
        blk = lax.broadcasted_iota(jnp.int32, (1, nb), 1)
        cur = ns - 1
        forced = (blk == 0) | (blk == cur) | (blk == cur - 1)
        score = jnp.where(blk <= cur, jnp.where(forced, FORCE_SCORE, imp), -jnp.inf)
        sel_s[...] = _select_blocks(score, min(TOP_N, ns)).astype(BF16)
        sw = sw_ref[...]
        kw = sw[:, 0:KV_W].astype(BF16)
        vw = sw[:, KV_W:2 * KV_W].astype(BF16)
        wcol = lax.broadcasted_iota(jnp.int32, (1, kw.shape[0]), 1)
        s1 = jnp.where(wcol > row_t, _dot_nt(qall, kw), -jnp.inf)
        s2 = jnp.where(new_ok, _dot_nt(qall, new[:, 2 * KV_W:3 * KV_W]), -jnp.inf)
        mw = jnp.maximum(jnp.max(s1, axis=-1, keepdims=True), jnp.max(s2, axis=-1, keepdims=True))
        p1 = jnp.exp(s1 - mw)
        p2 = jnp.exp(s2 - mw)
        den = jnp.maximum(jnp.sum(p1, axis=-1, keepdims=True) + jnp.sum(p2, axis=-1, keepdims=True), 1e-30)
        ow_s[...] = _dot((p1 / den).astype(BF16), vw) + _dot((p2 / den).astype(BF16), new[:, 3 * KV_W:4 * KV_W])
        m_s[...] = jnp.full(m_s.shape, NEG_BIG, F32)
        l_s[...] = jnp.zeros_like(l_s)
        acc_s[...] = jnp.zeros_like(acc_s)

    qall = qall_s[...]
    kk = jnp.concatenate([pages[k][:, 0:KV_W] for k in range(npg)], axis=0).astype(BF16)
    vv = jnp.concatenate([pages[k][:, KV_W:2 * KV_W] for k in range(npg)], axis=0).astype(BF16)
    mk = _dot(sel_s[...], e_ref[i]) > 0.5
    mk = jnp.concatenate([mk] * GROUP, axis=0)
    s = jnp.where(mk, _dot_nt(qall, kk), NEG_BIG)
    m_old = m_s[...]
    m_new = jnp.maximum(m_old, jnp.max(s, axis=-1, keepdims=True))
    alpha = jnp.exp(m_old - m_new)
    p = jnp.exp(s - m_new)
    l_new = alpha * l_s[...] + jnp.sum(p, axis=-1, keepdims=True)
    acc_new = alpha * acc_s[...] + _dot(p.astype(BF16), vv)
    m_s[...] = m_new
    l_s[...] = l_new
    acc_s[...] = acc_new

    @pl.when(i == nsteps - 1)
    def _():
        mkn = _dot(sel_s[...], en_ref[...]) > 0.5
        mkn = jnp.concatenate([mkn] * GROUP, axis=0) & new_ok
        sn = jnp.where(mkn, _dot_nt(qall, new[:, 0:KV_W]), NEG_BIG)
        m_f = jnp.maximum(m_new, jnp.max(sn, axis=-1, keepdims=True))
        al = jnp.exp(m_new - m_f)
        pn = jnp.exp(sn - m_f)
        l_f = al * l_new + jnp.sum(pn, axis=-1, keepdims=True)
        acc_f = al * acc_new + _dot(pn.astype(BF16), new[:, KV_W:2 * KV_W])
        o_s = acc_f / l_f
        out_ref[...] = gx_ref[0] * oc_s[...] + gx_ref[1] * o_s + gx_ref[2] * ow_s[...]


def _nsa_sample(pool, page_table, qrep, gx, kc, vc, sw, new8):
    b, n_pages = page_table.shape
    npg = PAGES_PER_STEP
    nsteps = n_pages // npg
    nc = kc.shape[1]
    past = n_pages * PAGE
    ns = past // SLC_BLOCK + 1
    nb = -(-ns // LANES) * LANES
    keys_step = npg * PAGE
    m = np.zeros((nc, nb), np.float32)
    m[:, :ns] = _cmp_to_slc(nc, ns)
    e = np.zeros((nsteps, nb, keys_step), np.float32)
    for st in range(nsteps):
        for j in range(keys_step):
            e[st, (st * keys_step + j) // SLC_BLOCK, j] = 1.0
    en = np.zeros((nb, 8), np.float32)
    en[ns - 1, :] = 1.0
    m, e, en = jnp.asarray(m, BF16), jnp.asarray(e, BF16), jnp.asarray(en, BF16)
    nr = GROUP * N_KV * 4

    def page_spec(k):
        return pl.BlockSpec((None, PAGE, 2 * KV_W), lambda bi, i, pt: (pt[bi, i * npg + k], 0, 1))

    per_b = lambda a: pl.BlockSpec((None,) + a.shape[1:], lambda bi, i, pt: (bi,) + (0,) * (a.ndim - 1))
    const = lambda a: pl.BlockSpec(a.shape, lambda bi, i, pt: (0,) * a.ndim)
    grid_spec = pltpu.PrefetchScalarGridSpec(
        num_scalar_prefetch=1,
        grid=(b, nsteps),
        in_specs=[page_spec(k) for k in range(npg)]
        + [per_b(qrep), per_b(gx), per_b(kc), per_b(vc), per_b(sw), per_b(new8), const(m), _vmem(), const(en)],
        out_specs=pl.BlockSpec((None, nr, KV_W), lambda bi, i, pt: (bi, 0, 0)),
        scratch_shapes=[pltpu.VMEM((nr, KV_W), BF16), pltpu.VMEM((N_KV * 4, nb), BF16),
                        pltpu.VMEM((nr, KV_W), F32), pltpu.VMEM((nr, KV_W), F32),
                        pltpu.VMEM((nr, 1), F32), pltpu.VMEM((nr, 1), F32), pltpu.VMEM((nr, KV_W), F32)],
    )
    return pl.pallas_call(
        functools.partial(_nsa_sample_kernel, nc=nc, ns=ns, nsteps=nsteps),
        grid_spec=grid_spec,
        out_shape=jax.ShapeDtypeStruct((b, nr, KV_W), F32),
        compiler_params=_cparams("arbitrary", "arbitrary"),
        name="nsa_sample",
    )(page_table, *([pool] * npg), qrep, gx, kc, vc, sw, new8, m, e, en)


def _outproj_kernel(x_ref, cy_ref, at_ref, ga1_ref, g2_ref, sc2_ref, sh2_ref, wo_ref, wq_ref, keys_ref,
                    x1_ref, h2_ref, s_ref, *, dc):
    u = _dot(cy_ref[...], wo_ref[0:dc, :]) + _dot(at_ref[...], wo_ref[dc:, :])
    x1 = x_ref[...] + ga1_ref[...] * u
    x1_ref[...] = x1
    h2 = _rms_mod(x1, g2_ref[...], sc2_ref[...], sh2_ref[...]).astype(BF16)
    h2_ref[...] = h2
    qp = _dot(h2, wq_ref[...]).astype(BF16)
    for hc in range(2 * PEER_HEADS):
        s_ref[:, hc * N_KEYS:(hc + 1) * N_KEYS] = _dot_nt(qp[:, hc * LANES:(hc + 1) * LANES], keys_ref[hc])


def _outproj(x, convy, attn, ga1, g2, sc2, sh2, wo, wq, keys, per_row):
    r, d = x.shape
    dc = d // 2
    tm = r if per_row else 256
    row_spec = lambda wdt: pl.BlockSpec((tm, wdt), lambda i: (i, 0))
    mod_spec = row_spec(d) if per_row else pl.BlockSpec((1, d), lambda i: (0, 0))
    nsc = keys.shape[0] * N_KEYS
    return pl.pallas_call(
        functools.partial(_outproj_kernel, dc=dc),
        grid=(r // tm,),
        in_specs=[row_spec(d), row_spec(dc), row_spec(d - dc), mod_spec,
                  pl.BlockSpec((1, d), lambda i: (0, 0)), mod_spec, mod_spec, _vmem(), _vmem(), _vmem()],
        out_specs=(row_spec(d), row_spec(d), row_spec(nsc)),
        out_shape=(jax.ShapeDtypeStruct((r, d), F32), jax.ShapeDtypeStruct((r, d), BF16),
                   jax.ShapeDtypeStruct((r, nsc), F32)),
        compiler_params=_cparams("arbitrary"),
        name="outproj_sample" if per_row else "outproj_prompt",
    )(x, convy, attn, ga1, g2, sc2, sh2, wo, wq, keys)


def _top_sorted(s, n):
    rowi = lax.broadcasted_iota(jnp.int32, s.shape, 0)
    nrow = s.shape[0]
    vals = []
    cur = s
    for _ in range(n):
        mx = jnp.max(cur, axis=0, keepdims=True)
        vals.append(mx)
        first = jnp.min(jnp.where(cur == mx, rowi, nrow), axis=0, keepdims=True)
        cur = jnp.where(rowi == first, -jnp.inf, cur)
    return vals


def _peer_select_kernel(s_ref, a1_ref, a2_ref, e1_ref, e2_ref, tau_ref):
    k = PEER_TOPK
    s1 = s_ref[0:N_KEYS, :]
    s2 = s_ref[N_KEYS:2 * N_KEYS, :]
    v1 = _top_sorted(s1, k)
    v2 = _top_sorted(s2, k)
    tn = s1.shape[1]
    d1 = jnp.concatenate([v - v1[0] for v in v1], axis=0)
    d2 = jnp.concatenate([v - v2[0] for v in v2], axis=0)
    cand = (d1[:, None, :] + d2[None, :, :]).reshape(k * k, tn)
    prod = (jnp.exp(d1)[:, None, :] * jnp.exp(d2)[None, :, :]).reshape(k * k, tn)
    ridx = lax.broadcasted_iota(jnp.int32, (k * k, 1), 0)
    possible = (ridx // k + 1) * (ridx % k + 1) <= k
    cand = jnp.where(possible, cand, -jnp.inf)
    tau = _top_sorted(cand, k)[k - 1]
    z = jnp.sum(jnp.where(cand >= tau, prod, 0.0), axis=0, keepdims=True)
    a1 = s1 - v1[0]
    a2 = s2 - v2[0]
    a1_ref[...] = a1
    a2_ref[...] = a2
    e1_ref[...] = jnp.exp(a1) / z
    e2_ref[...] = jnp.exp(a2)
    tau_ref[...] = jnp.broadcast_to(tau, tau_ref.shape)


def _peer_select(s_t, tn):
    nrow, tp = s_t.shape
    hk = PEER_HEADS * N_KEYS
    half_spec = pl.BlockSpec((N_KEYS, tn), lambda t, h: (h, t))
    return pl.pallas_call(
        _peer_select_kernel,
        grid=(tp // tn, PEER_HEADS),
        in_specs=[pl.BlockSpec((2 * N_KEYS, tn), lambda t, h: (h, t))],
        out_specs=(half_spec, half_spec, half_spec, half_spec, pl.BlockSpec((None, 8, tn), lambda t, h: (h, 0, t))),
        out_shape=(jax.ShapeDtypeStruct((hk, tp), F32),) * 4 + (jax.ShapeDtypeStruct((PEER_HEADS, 8, tp), F32),),
        compiler_params=_cparams("arbitrary", "arbitrary"),
        name="peer_select",
    )(s_t)


EXPERT_ROWS_PER_STEP = 4


def _peer_dense_kernel(h_ref, u_ref, vt_ref, a1_ref, a2_ref, e1_ref, e2_ref, tau_ref, out_ref):
    c = pl.program_id(1)

    @pl.when(c == 0)
    def _():
        out_ref[...] = jnp.zeros_like(out_ref)

    act = jax.nn.gelu(_dot(u_ref[...], h_ref[...]))
    ws = []
    for ii in range(EXPERT_ROWS_PER_STEP):
        i = c * EXPERT_ROWS_PER_STEP + ii
        w = jnp.zeros((N_KEYS, h_ref.shape[1]), F32)
        for h in range(PEER_HEADS):
            r1 = a1_ref[pl.ds(h * N_KEYS + i, 1), :]
            er = e1_ref[pl.ds(h * N_KEYS + i, 1), :]
            hs = slice(h * N_KEYS, (h + 1) * N_KEYS)
            keep = (r1 + a2_ref[hs, :]) >= tau_ref[h, 0:1, :]
            w = w + jnp.where(keep, er * e2_ref[hs, :], 0.0)
        ws.append(w)
    wg = (jnp.concatenate(ws, axis=0) * act).astype(BF16)
    out_ref[...] += _dot(vt_ref[...], wg)


def _peer_dense(h_t, u_bf, vt_bf, a1, a2, e1, e2, tau, tm):
    d, tp = h_t.shape
    ne = u_bf.shape[0]
    ec = EXPERT_ROWS_PER_STEP * N_KEYS
    hk = PEER_HEADS * N_KEYS
    tok = lambda rows: pl.BlockSpec((rows, tm), lambda t, c: (0, t))
    return pl.pallas_call(
        _peer_dense_kernel,
        grid=(tp // tm, ne // ec),
        in_specs=[tok(d), pl.BlockSpec((ec, d), lambda t, c: (c, 0)), pl.BlockSpec((d, ec), lambda t, c: (0, c)),
                  tok(hk), tok(hk), tok(hk), tok(hk), pl.BlockSpec((PEER_HEADS, 8, tm), lambda t, c: (0, 0, t))],
        out_specs=tok(d),
        out_shape=jax.ShapeDtypeStruct((d, tp), F32),
        compiler_params=_cparams("arbitrary", "arbitrary"),
        name="peer_dense",
    )(h_t, u_bf, vt_bf, a1, a2, e1, e2, tau)


def _final_kernel(x1_ref, f_ref, ga2_ref, fg_ref, y_ref):
    x2 = x1_ref[...] + ga2_ref[...] * f_ref[...]
    y = x2 * lax.rsqrt(jnp.mean(x2 * x2, axis=-1, keepdims=True) + EPS)
    y_ref[...] = y * fg_ref[...]


def _final(x1, f, ga2, fg, per_row):
    r, d = x1.shape
    tm = r if per_row else 256
    row_spec = pl.BlockSpec((tm, d), lambda i: (i, 0))
    vec_spec = pl.BlockSpec((1, d), lambda i: (0, 0))
    return pl.pallas_call(
        _final_kernel,
        grid=(r // tm,),
        in_specs=[row_spec, row_spec, row_spec if per_row else vec_spec, vec_spec],
        out_specs=row_spec,
        out_shape=jax.ShapeDtypeStruct((r, d), F32),
        compiler_params=_cparams("arbitrary"),
        name="final_sample" if per_row else "final_prompt",
    )(x1, f, ga2, fg)


def _prep_w_in(w_in):
    d = w_in.shape[0]
    q0 = 3 * (d // 2)
    wq = w_in[:, q0:q0 + D_ATTN].reshape(d, N_KV, GROUP, HEAD_DIM).transpose(0, 2, 1, 3).reshape(d, D_ATTN)
    rest = w_in[:, q0 + D_ATTN:]
    pad = jnp.zeros((d, LANES - 3 * N_HEADS), w_in.dtype)
    return jnp.concatenate([w_in[:, :q0], wq, rest, pad], axis=1).astype(BF16)


def _prep_w_out(w_out):
    d = w_out.shape[0]
    dc = d // 2
    wa = w_out[dc:].reshape(N_KV, GROUP, HEAD_DIM, d).transpose(1, 0, 2, 3).reshape(D_ATTN, d)
    return jnp.concatenate([w_out[:dc], wa], axis=0).astype(BF16)


def _token_tile(n128):
    for m in (5, 4, 3, 2, 1):
        if n128 % m == 0:
            return m * LANES
    return LANES


def kernel(x_prompt, x_sample, c_prompt, c_sample, cache_kv, state_win, state_conv, page_table, w_ada, b_ada,
           norm1_g, norm2_g, w_in, conv_w, conv_b, cmp_pe, cmp_w1, cmp_b1, cmp_w2, cmp_b2, w_out, peer_wq,
           peer_keys, peer_u, peer_v, final_g):
    depth = w_ada.shape[0]
    assert depth == 1
    bp, t, d = x_prompt.shape
    assert bp == 1 and d - d // 2 == D_ATTN
    bs, ts, _ = x_sample.shape
    assert ts == 4
    dc = d // 2
    n_pages = page_table.shape[1]
    past = n_pages * PAGE
    wkeep = state_win.shape[2]
    assert wkeep == WINDOW and t % KEY_TILE == 0 and n_pages % PAGES_PER_STEP == 0
    rs = bs * ts

    n_c = 1 + bs
    n_cp = -(-n_c // 8) * 8
    c_all = jnp.concatenate([c_prompt, c_sample, jnp.zeros((n_cp - n_c, d), F32)], axis=0)
    ada = _ada(c_all, w_ada[0], b_ada[0])
    sh1, sc1, ga1, sh2, sc2, ga2 = [ada[:, k * d:(k + 1) * d] for k in range(6)]
    pr = lambda a: a[0:1]
    sm = lambda a: jnp.repeat(a[1:1 + bs], ts, axis=0)

    w_perm = _prep_w_in(w_in[0])
    wo_perm = _prep_w_out(w_out[0])
    g1 = norm1_g[0].reshape(1, d)
    g2 = norm2_g[0].reshape(1, d)
    cw = conv_w[0]
    cb = conv_b[0].reshape(1, dc)
    cmpw = _prep_cmp_weights(cmp_pe[0], cmp_w1[0], cmp_b1[0], cmp_w2[0], cmp_b2[0])

    xp = x_prompt[0]
    cos_p, sin_p = _rope_tables(jnp.arange(t))
    convy_p, q_p, rows_p, win_p, kvb_p, gates_p, zc_p = _proj(xp, g1, pr(sc1), pr(sh1), w_perm, cos_p, sin_p, cw, cb)
    ident = jnp.arange(t // PAGE, dtype=jnp.int32).reshape(1, t // PAGE)
    kc_p, vc_p = _compress(rows_p.reshape(t // PAGE, PAGE, 4 * KV_W), ident, cmpw)
    wpad = jnp.pad(kvb_p[:, 2 * KV_W:], ((WINDOW, 0), (0, 0)))
    attn_p = _nsa_prompt(q_p, gates_p, kc_p[0], vc_p[0], kvb_p, wpad)
    wq_bf = peer_wq[0].astype(BF16)
    keys_bf = peer_keys[0].reshape(2 * PEER_HEADS, N_KEYS, -1).astype(BF16)
    x1_p, h2_p, s_p = _outproj(xp, convy_p, attn_p, pr(ga1), g2, pr(sc2), pr(sh2), wo_perm, wq_bf, keys_bf, False)

    xs = x_sample.reshape(rs, d)
    pos_s = jnp.tile(past + jnp.arange(ts), bs)
    cos_s, sin_s = _rope_tables(pos_s)
    st = state_conv[0]
    a1 = jnp.repeat(st[:, 1], ts, axis=0)
    a2 = jnp.stack([st[:, 0], st[:, 1], st[:, 1], st[:, 1]], axis=1).reshape(rs, dc)
    convy_s, q_s, rows_s, win_s, kvb_s, gates_s, zc_s = _proj(
        xs, g1, sm(sc1), sm(sh1), w_perm, cos_s, sin_s, cw, cb, a1, a2)
    pool = cache_kv[0].reshape(cache_kv.shape[1], PAGE, 4 * KV_W)
    kc_s, vc_s = _compress(pool, page_table, cmpw)
    qrep = jnp.broadcast_to(q_s.reshape(bs, ts, GROUP, 1, KV_W).transpose(0, 2, 3, 1, 4),
                            (bs, GROUP, N_KV, ts, KV_W)).reshape(bs, GROUP * N_KV * ts, KV_W)
    gts = gates_s[:, :3 * N_HEADS].reshape(bs, ts, N_KV, GROUP, 3).transpose(0, 4, 3, 2, 1)
    gx = jnp.broadcast_to(gts.reshape(bs, 3, GROUP * N_KV * ts, 1), (bs, 3, GROUP * N_KV * ts, KV_W))
    new8 = jnp.pad(kvb_s.reshape(bs, ts, 4 * KV_W), ((0, 0), (0, 8 - ts), (0, 0)))
    sw = state_win[0].reshape(bs, wkeep, 2 * KV_W)
    o_s = _nsa_sample(pool, page_table, qrep, gx, kc_s, vc_s, sw, new8)
    o6 = o_s.reshape(bs, GROUP, N_KV, ts, N_KV, HEAD_DIM)
    gi = jnp.arange(N_KV)
    attn_s = o6[:, :, gi, :, gi, :]
    attn_s = attn_s.transpose(1, 3, 2, 0, 4).reshape(rs, D_ATTN).astype(BF16)
    x1_s, h2_s, s_s = _outproj(xs, convy_s, attn_s, sm(ga1), g2, sm(sc2), sm(sh2), wo_perm, wq_bf, keys_bf, True)

    ntok = t + rs
    tp = -(-ntok // LANES) * LANES
    padt = lambda a: jnp.pad(jnp.concatenate(a, axis=0), ((0, tp - ntok), (0, 0))).T
    h_t = padt([h2_p, h2_s])
    s_t = padt([s_p, s_s])
    tm = _token_tile(tp // LANES)
    a1p, a2p, e1p, e2p, tau = _peer_select(s_t, LANES)
    u_bf = peer_u[0].astype(BF16)
    vt_bf = peer_v[0].astype(BF16).T
    f_t = _peer_dense(h_t, u_bf, vt_bf, a1p, a2p, e1p, e2p, tau, tm)
    f = f_t.T
    fg = final_g.reshape(1, d)
    y_p = _final(x1_p, f[:t], pr(ga2), fg, False)
    y_s = _final(x1_s, f[t:ntok], sm(ga2), fg, True)

    wmin = min(WINDOW, t)
    win_ctx = jnp.concatenate([state_win[0], win_s.reshape(bs, ts, 2, N_KV, HEAD_DIM)], axis=1)[:, ts:]
    zc3 = zc_s.reshape(bs, ts, dc)
    return (y_p.reshape(1, t, d), y_s.reshape(bs, ts, d),
            rows_p.reshape(1, 1, t, 4, N_KV, HEAD_DIM), rows_s.reshape(1, bs, ts, 4, N_KV, HEAD_DIM),
            win_p[t - wmin:].reshape(1, 1, wmin, 2, N_KV, HEAD_DIM), win_ctx[None],
            zc_p[8 - (3 - 1):].reshape(1, 1, 2, dc), zc3[:, ts - 2:][None])
```

```python
import functools

import numpy as np
import jax
import jax.numpy as jnp
from jax import lax
from jax.experimental import pallas as pl
from jax.experimental.pallas import tpu as pltpu

F32 = jnp.float32
BF16 = jnp.bfloat16

HEAD_DIM = 64
N_KV = 4
GROUP = 4
N_HEADS = N_KV * GROUP
KV_W = N_KV * HEAD_DIM
D_ATTN = N_HEADS * HEAD_DIM
CMP_STRIDE = 16
SLC_BLOCK = 64
TOP_N = 16
WINDOW = 512
Q_BLOCK = 128
PAGE = 128
FORCE_SCORE = 1.0e4
ROPE_THETA = 10000.0
N_KEYS = 128
PEER_HEADS = 8
PEER_TOPK = 16
EPS = 1e-6
LANES = 128
PAGES_PER_STEP = 8
KEY_TILE = 256
V7X_VMEM_LIMIT = 58 * 1024 * 1024
NEG_BIG = -1e30


def _cparams(*sem):
    return pltpu.CompilerParams(dimension_semantics=sem, vmem_limit_bytes=V7X_VMEM_LIMIT)


def _vmem():
    return pl.BlockSpec(memory_space=pltpu.VMEM)


def _dot(a, b):
    return jnp.dot(a, b, preferred_element_type=F32)


def _dot_nt(a, b):
    return lax.dot_general(a, b, (((1,), (1,)), ((), ())), preferred_element_type=F32)


def _split_dot(x, m):
    hi = x.astype(BF16)
    lo = (x - hi.astype(F32)).astype(BF16)
    return _dot(hi, m) + _dot(lo, m)


def _rope(x, cos, sinsg):
    w = x.shape[1]
    n = w // LANES
    c = jnp.concatenate([cos] * n, axis=1) if n > 1 else cos
    s = jnp.concatenate([sinsg] * n, axis=1) if n > 1 else sinsg
    lane = lax.broadcasted_iota(jnp.int32, x.shape, 1)
    first = (lane % HEAD_DIM) < (HEAD_DIM // 2)
    partner = jnp.where(first, pltpu.roll(x, w - HEAD_DIM // 2, 1), pltpu.roll(x, HEAD_DIM // 2, 1))
    return x * c + partner * s


def _rope_tables(pos):
    half = HEAD_DIM // 2
    inv = ROPE_THETA ** (-jnp.arange(half, dtype=F32) / half)
    ang = pos.astype(F32)[:, None] * inv[None, :]
    cos = jnp.cos(ang)
    sin = jnp.sin(ang)
    cos = jnp.concatenate([cos, cos, cos, cos], axis=1)
    sinsg = jnp.concatenate([-sin, sin, -sin, sin], axis=1)
    return cos, sinsg


def _rms_mod(xf, g, sc, sh):
    y = xf * lax.rsqrt(jnp.mean(xf * xf, axis=-1, keepdims=True) + EPS)
    return y * g * (1.0 + sc) + sh


def _ada_kernel(c_ref, w_ref, b_ref, o_ref):
    s = jax.nn.silu(c_ref[...]).astype(BF16)
    o_ref[...] = _dot(s, w_ref[...].astype(BF16)) + b_ref[...]


def _ada(c_all, w_ada, b_ada):
    r, d = c_all.shape
    n = w_ada.shape[1]
    tn = 1024
    return pl.pallas_call(
        _ada_kernel,
        grid=(n // tn,),
        in_specs=[pl.BlockSpec((r, d), lambda j: (0, 0)),
                  pl.BlockSpec((d, tn), lambda j: (0, j)),
                  pl.BlockSpec((1, tn), lambda j: (0, j))],
        out_specs=pl.BlockSpec((r, tn), lambda j: (0, j)),
        out_shape=jax.ShapeDtypeStruct((r, n), F32),
        compiler_params=_cparams("arbitrary"),
        name="ada",
    )(c_all, w_ada, b_ada.reshape(1, n))


def _proj_kernel(*refs, sample, tm, dc):
    if sample:
        (x_ref, g1_ref, sc_ref, sh_ref, w_ref, cos_ref, sin_ref, cw_ref, cb_ref, a1_ref, a2_ref,
         convy_ref, q_ref, rows_ref, win_ref, kvb_ref, gates_ref, zc_ref) = refs
    else:
        (x_ref, g1_ref, sc_ref, sh_ref, w_ref, cos_ref, sin_ref, cw_ref, cb_ref,
         convy_ref, q_ref, rows_ref, win_ref, kvb_ref, gates_ref, zc_ref, carry_ref) = refs
    h = _rms_mod(x_ref[...], g1_ref[...], sc_ref[...], sh_ref[...]).astype(BF16)

    def mm(a, b):
        return _dot(h, w_ref[:, a:b])

    cos = cos_ref[...]
    sin = sin_ref[...]
    q0 = 3 * dc
    kv0 = q0 + D_ATTN
    g0 = kv0 + 6 * KV_W

    b_g = mm(0, dc)
    zc = mm(dc, 2 * dc) * mm(2 * dc, 3 * dc)
    row = lax.broadcasted_iota(jnp.int32, zc.shape, 0)
    if sample:
        t = row % 4
        zm1 = jnp.where(t >= 1, pltpu.roll(zc, 1, 0), a1_ref[...])
        zm2 = jnp.where(t >= 2, pltpu.roll(zc, 2, 0), a2_ref[...])
        zc_ref[...] = zc
    else:
        @pl.when(pl.program_id(0) == 0)
        def _():
            carry_ref[...] = jnp.zeros_like(carry_ref)
        p1 = carry_ref[7:8, :]
        p2 = carry_ref[6:7, :]
        zm1 = jnp.where(row == 0, p1, pltpu.roll(zc, 1, 0))
        zm2 = jnp.where(row == 0, p2, jnp.where(row == 1, p1, pltpu.roll(zc, 2, 0)))
        carry_ref[...] = zc[tm - 8:tm, :]
        zc_ref[...] = zc[tm - 8:tm, :]
    conv = cb_ref[...] + cw_ref[0:1, :] * zm2
    conv = conv + cw_ref[1:2, :] * zm1
    conv = conv + cw_ref[2:3, :] * zc
    convy_ref[...] = (b_g * conv).astype(BF16)

    q = _rope(mm(q0, kv0), cos, sin) * (HEAD_DIM ** -0.5)
    q_ref[...] = q.astype(BF16)

    kv = mm(kv0, g0)
    k_cmp = kv[:, 0:KV_W]
    v_cmp = kv[:, KV_W:2 * KV_W]
    k_slc = _rope(kv[:, 2 * KV_W:3 * KV_W], cos, sin)
    v_slc = kv[:, 3 * KV_W:4 * KV_W]
    k_win = _rope(kv[:, 4 * KV_W:5 * KV_W], cos, sin)
    v_win = kv[:, 5 * KV_W:6 * KV_W]
    rows_ref[:, 0:KV_W] = k_cmp
    rows_ref[:, KV_W:2 * KV_W] = v_cmp
    rows_ref[:, 2 * KV_W:3 * KV_W] = k_slc
    rows_ref[:, 3 * KV_W:4 * KV_W] = v_slc
    win_ref[:, 0:KV_W] = k_win
    win_ref[:, KV_W:2 * KV_W] = v_win
    kvb_ref[:, 0:KV_W] = k_slc.astype(BF16)
    kvb_ref[:, KV_W:2 * KV_W] = v_slc.astype(BF16)
    kvb_ref[:, 2 * KV_W:3 * KV_W] = k_win.astype(BF16)
    kvb_ref[:, 3 * KV_W:4 * KV_W] = v_win.astype(BF16)
    gates_ref[...] = jax.nn.sigmoid(mm(g0, g0 + LANES))


def _proj(x, g1, sc, sh, w_perm, cos, sin, conv_w, conv_b, a1=None, a2=None):
    sample = a1 is not None
    r, d = x.shape
    dc = d // 2
    tm = r if sample else 256
    nw = w_perm.shape[1]
    row_spec = lambda wdt: pl.BlockSpec((tm, wdt), lambda i: (i, 0))
    full = lambda a: pl.BlockSpec(a.shape, lambda i: (0,) * a.ndim)
    mod_spec = row_spec(d) if sample else pl.BlockSpec((1, d), lambda i: (0, 0))
    in_specs = [row_spec(d), full(g1), mod_spec, mod_spec, _vmem(), row_spec(LANES), row_spec(LANES),
                full(conv_w), full(conv_b)]
    args = [x, g1, sc, sh, w_perm, cos, sin, conv_w, conv_b]
    scratch = []
    if sample:
        in_specs += [row_spec(dc), row_spec(dc)]
        args += [a1, a2]
        zc_shape = jax.ShapeDtypeStruct((r, dc), F32)
        zc_spec = row_spec(dc)
    else:
        zc_shape = jax.ShapeDtypeStruct((8, dc), F32)
        zc_spec = pl.BlockSpec((8, dc), lambda i: (0, 0))
        scratch = [pltpu.VMEM((8, dc), F32)]
    out_shape = (jax.ShapeDtypeStruct((r, dc), BF16), jax.ShapeDtypeStruct((r, D_ATTN), BF16),
                 jax.ShapeDtypeStruct((r, 4 * KV_W), F32), jax.ShapeDtypeStruct((r, 2 * KV_W), F32),
                 jax.ShapeDtypeStruct((r, 4 * KV_W), BF16), jax.ShapeDtypeStruct((r, LANES), F32), zc_shape)
    out_specs = (row_spec(dc), row_spec(D_ATTN), row_spec(4 * KV_W), row_spec(2 * KV_W),
                 row_spec(4 * KV_W), row_spec(LANES), zc_spec)
    return pl.pallas_call(
        functools.partial(_proj_kernel, sample=sample, tm=tm, dc=dc),
        grid=(r // tm,),
        in_specs=in_specs, out_specs=out_specs, out_shape=out_shape, scratch_shapes=scratch,
        compiler_params=_cparams("arbitrary"),
        name="proj_sample" if sample else "proj_prompt",
    )(*args)


def _cmp_kernel(pt_ref, *refs, transposed):
    npg = PAGES_PER_STEP
    pages = refs[:npg]
    (perm_ref, w1a_ref, w1b_ref, pea_ref, peb_ref, b1_ref, w2_ref, b2_ref, cos_ref, sin_ref,
     kc_ref, vc_ref, xa_s, xb_s, hprev_s) = refs[npg:]
    del pt_ref
    nch = PAGE // CMP_STRIDE
    rows = npg * nch

    @pl.when(pl.program_id(1) == 0)
    def _():
        hprev_s[...] = jnp.zeros_like(hprev_s)

    lane = lax.broadcasted_iota(jnp.int32, (nch, 2 * KV_W), 1)
    low = (lane % LANES) < HEAD_DIM
    perm = perm_ref[...]
    for k in range(npg):
        page = pages[k][...]
        for pe_ref, x_s in ((pea_ref, xa_s), (peb_ref, xb_s)):
            src = (page + pe_ref[...]).astype(BF16)
            y = _dot_nt(perm, src) if transposed else _dot(perm, src)
            for p in range(CMP_STRIDE // 2):
                ev = y[2 * p * nch:(2 * p + 1) * nch]
                od = y[(2 * p + 1) * nch:(2 * p + 2) * nch]
                a = jnp.where(low, ev, pltpu.roll(od, HEAD_DIM, 1))
                b = jnp.where(low, pltpu.roll(ev, 2 * KV_W - HEAD_DIM, 1), od)
                for j in range(4):
                    kv, ge = j // 2, 2 * (j % 2)
                    rsl = slice(nch * k, nch * (k + 1))
                    csl = slice(LANES * p, LANES * (p + 1))
                    x_s[kv, ge, rsl, csl] = a[:, LANES * j:LANES * (j + 1)]
                    x_s[kv, ge + 1, rsl, csl] = b[:, LANES * j:LANES * (j + 1)]

    row = lax.broadcasted_iota(jnp.int32, (N_KV * rows, 1), 0)
    for kv in range(2):
        ha = _dot(xa_s[kv].reshape(N_KV * rows, CMP_STRIDE * HEAD_DIM).astype(BF16), w1a_ref[kv])
        hb = _dot(xb_s[kv].reshape(N_KV * rows, CMP_STRIDE * HEAD_DIM).astype(BF16), w1b_ref[kv])
        hp = hprev_s[kv]
        prev = jnp.where(row % rows == 0, pltpu.roll(hp, N_KV * rows - (rows - 1), 0), pltpu.roll(ha, 1, 0))
        hprev_s[kv] = ha
        hid = jax.nn.gelu(prev + hb + b1_ref[kv])
        hcat = jnp.concatenate([hid[rows * g:rows * (g + 1)] for g in range(N_KV)], axis=1).astype(BF16)
        o = _dot(hcat, w2_ref[kv]) + b2_ref[kv]
        if kv == 0:
            kc_ref[...] = _rope(o, cos_ref[...], sin_ref[...]).astype(BF16)
        else:
            vc_ref[...] = o.astype(BF16)


def _compress(pool, page_table, cw, transposed):
    b, n_pages = page_table.shape
    npg = PAGES_PER_STEP
    nsteps = n_pages // npg
    rows = npg * (PAGE // CMP_STRIDE)
    nc = nsteps * rows
    cpos = CMP_STRIDE * jnp.arange(nc) + (CMP_STRIDE - 1)
    cos, sin = _rope_tables(cpos)
    page_block = (None, 2 * KV_W, PAGE) if transposed else (None, PAGE, 2 * KV_W)

    def page_spec(k):
        return pl.BlockSpec(page_block, lambda bi, i, pt: (pt[bi, i * npg + k], 0, 0))

    const = lambda a: pl.BlockSpec(a.shape, lambda bi, i, pt: (0,) * a.ndim)
    nch = PAGE // CMP_STRIDE
    perm = np.zeros((PAGE, PAGE), np.float32)
    for s in range(CMP_STRIDE):
        for c in range(nch):
            perm[s * nch + c, CMP_STRIDE * c + s] = 1.0
    pea, peb = (cw["pea"].T, cw["peb"].T) if transposed else (cw["pea"], cw["peb"])
    weights = [jnp.asarray(perm, BF16), cw["w1a"], cw["w1b"], pea, peb, cw["b1"], cw["w2bd"], cw["b2"]]
    tab_spec = pl.BlockSpec((rows, LANES), lambda bi, i, pt: (i, 0))
    out_spec = pl.BlockSpec((None, rows, KV_W), lambda bi, i, pt: (bi, i, 0))
    x_scratch = pltpu.VMEM((2, N_KV, rows, CMP_STRIDE * HEAD_DIM), F32)
    grid_spec = pltpu.PrefetchScalarGridSpec(
        num_scalar_prefetch=1,
        grid=(b, nsteps),
        in_specs=[page_spec(k) for k in range(npg)] + [const(a) for a in weights] + [tab_spec, tab_spec],
        out_specs=(out_spec, out_spec),
        scratch_shapes=[x_scratch, x_scratch, pltpu.VMEM((2, N_KV * rows, 2 * HEAD_DIM), F32)],
    )
    return pl.pallas_call(
        functools.partial(_cmp_kernel, transposed=transposed),
        grid_spec=grid_spec,
        out_shape=(jax.ShapeDtypeStruct((b, nc, KV_W), BF16), jax.ShapeDtypeStruct((b, nc, KV_W), BF16)),
        compiler_params=_cparams("arbitrary", "arbitrary"),
        name="compress",
    )(page_table, *([pool] * npg), *weights, cos, sin)


def _prep_cmp_weights(cmp_pe, cmp_w1, cmp_b1, cmp_w2, cmp_b2):
    s = CMP_STRIDE
    w2bd = jnp.zeros((2, N_KV * 2 * HEAD_DIM, KV_W), F32)
    for g in range(N_KV):
        w2bd = w2bd.at[:, g * 2 * HEAD_DIM:(g + 1) * 2 * HEAD_DIM, g * HEAD_DIM:(g + 1) * HEAD_DIM].set(cmp_w2)
    def pe_page(pe_half):
        t = jnp.broadcast_to(pe_half.transpose(1, 0, 2)[:, :, None, :], (s, 2, N_KV, HEAD_DIM))
        return jnp.tile(t.reshape(s, 2 * KV_W), (PAGE // s, 1))

    return {
        "w1a": cmp_w1[:, :s].reshape(2, s * HEAD_DIM, 2 * HEAD_DIM).astype(BF16),
        "w1b": cmp_w1[:, s:].reshape(2, s * HEAD_DIM, 2 * HEAD_DIM).astype(BF16),
        "pea": pe_page(cmp_pe[:, :s]),
        "peb": pe_page(cmp_pe[:, s:]),
        "b1": cmp_b1.reshape(2, 1, 2 * HEAD_DIM),
        "w2bd": w2bd.astype(BF16),
        "b2": jnp.tile(cmp_b2, (1, N_KV)).reshape(2, 1, KV_W),
    }


def _cmp_to_slc(nc, ns):
    j = np.arange(nc)[:, None] - 1
    b = np.arange(ns)[None, :]
    ov = (CMP_STRIDE * j < SLC_BLOCK * (b + 1)) & (CMP_STRIDE * j + 2 * CMP_STRIDE > SLC_BLOCK * b) & (j >= 0)
    return ov.astype(np.float32)


def _masked_softmax(s, mask):
    s = jnp.where(mask, s, -jnp.inf)
    m = jnp.max(s, axis=-1, keepdims=True)
    m = jnp.where(m > -jnp.inf, m, 0.0)
    p = jnp.exp(s - m)
    return p * (1.0 / jnp.maximum(jnp.sum(p, axis=-1, keepdims=True), 1e-30))


def _select_blocks(score, n_top):
    lane = lax.broadcasted_iota(jnp.int32, score.shape, 1)
    nb = score.shape[1]
    sel = jnp.zeros(score.shape, F32)
    cur = score
    for _ in range(n_top):
        mx = jnp.max(cur, axis=-1, keepdims=True)
        first = jnp.min(jnp.where(cur == mx, lane, nb), axis=-1, keepdims=True)
        hit = (lane == first) & (mx > -jnp.inf)
        sel = jnp.where(hit, 1.0, sel)
        cur = jnp.where(lane == first, -jnp.inf, cur)
    return sel


def _masked_softmax0(s, mask):
    s = jnp.where(mask, s, -jnp.inf)
    m = jnp.max(s, axis=0, keepdims=True)
    m = jnp.where(m > -jnp.inf, m, 0.0)
    p = jnp.exp(s - m)
    return p * (1.0 / jnp.maximum(jnp.sum(p, axis=0, keepdims=True), 1e-30))


def _select_blocks_t(score, n_top):
    rowi = lax.broadcasted_iota(jnp.int32, score.shape, 0)
    nb = score.shape[0]
    sel = jnp.zeros(score.shape, F32)
    cur = score
    for _ in range(n_top):
        mx = jnp.max(cur, axis=0, keepdims=True)
        first = jnp.min(jnp.where(cur == mx, rowi, nb), axis=0, keepdims=True)
        is_first = rowi == first
        sel = jnp.where(is_first & (mx > -jnp.inf), 1.0, sel)
        cur = jnp.where(is_first, -jnp.inf, cur)
    return sel


def _own_blocks(x, tq):
    rq = GROUP * tq
    return jnp.concatenate([x[g * HEAD_DIM:(g + 1) * HEAD_DIM, g * rq:(g + 1) * rq] for g in range(N_KV)], axis=0)


def _nsa_prompt_kernel(qt_ref, gt_ref, kc_ref, vct_ref, ks_ref, vst_ref, kw_ref, vwt_ref, mt_ref,
                       out_ref, qm_s, bias_s, acc_s, m_s, l_s, sa_s, sb_s, *, nc, ns):
    i = pl.program_id(0)
    tq = Q_BLOCK
    rq = GROUP * tq
    ncol = N_KV * rq
    colq = i * tq + lax.broadcasted_iota(jnp.int32, (1, rq), 1) % tq
    colq_g = i * tq + lax.broadcasted_iota(jnp.int32, (1, N_KV * tq), 1) % tq
    grows = [slice(g * HEAD_DIM, (g + 1) * HEAD_DIM) for g in range(N_KV)]
    gcols = [slice(g * rq, (g + 1) * rq) for g in range(N_KV)]

    row_g = lax.broadcasted_iota(jnp.int32, (KV_W, tq), 0) // HEAD_DIM
    qt = qt_ref[...]
    for g in range(N_KV):
        for r in range(GROUP):
            c0 = (g * GROUP + r) * tq
            qm_s[:, c0:c0 + tq] = jnp.where(row_g == g, qt[r * KV_W:(r + 1) * KV_W, :], 0).astype(BF16)

    crow = lax.broadcasted_iota(jnp.int32, (nc, 1), 0)
    cvalid = (crow >= 1) & (CMP_STRIDE * crow + (CMP_STRIDE - 1) <= colq)
    kc = kc_ref[...]
    o_c, psum = [], []
    scores = [_dot(kc, qm_s[:, gcols[g]]) for g in range(N_KV)]
    for g in range(N_KV):
        p = _masked_softmax0(scores[g], cvalid)
        o_c.append(_dot(vct_ref[grows[g], :], p.astype(BF16)))
        psum.append(sum(p[:, r * tq:(r + 1) * tq] for r in range(GROUP)))
    o_c = jnp.concatenate(o_c, axis=0)
    psum = jnp.concatenate(psum, axis=1)
    hi = psum.astype(BF16)
    lo = (psum - hi.astype(F32)).astype(BF16)
    imp = _dot(mt_ref[...], hi) + _dot(mt_ref[...], lo)
    blk = lax.broadcasted_iota(jnp.int32, (ns, 1), 0)
    cur = colq_g // SLC_BLOCK
    forced = (blk == 0) | (blk == cur) | (blk == cur - 1)
    score = jnp.where(blk <= cur, jnp.where(forced, FORCE_SCORE, imp), -jnp.inf)
    bias = (_select_blocks_t(score, min(TOP_N, ns)) - 1.0) * (-NEG_BIG)
    for g in range(N_KV):
        for r in range(GROUP):
            c0 = (g * GROUP + r) * tq
            bias_s[:, c0:c0 + tq] = bias[:, g * tq:(g + 1) * tq]

    w0 = pl.multiple_of(i * tq, tq)
    kw = kw_ref[pl.ds(w0, WINDOW + tq), :]
    nwt = (WINDOW + tq) // tq
    vwt = vwt_ref[pl.ds(i, nwt)]
    vwt = jnp.concatenate([vwt[j] for j in range(nwt)], axis=1)
    wpos = i * tq - WINDOW + lax.broadcasted_iota(jnp.int32, (WINDOW + tq, 1), 0)
    dist = colq - wpos
    wmask = (dist >= 0) & (dist < WINDOW) & (wpos >= 0)
    o_w = []
    scores = [_dot(kw, qm_s[:, gcols[g]]) for g in range(N_KV)]
    for g in range(N_KV):
        p = _masked_softmax0(scores[g], wmask)
        o_w.append(_dot(vwt[grows[g], :], p.astype(BF16)))
    o_w = jnp.concatenate(o_w, axis=0)

    acc_s[...] = jnp.zeros_like(acc_s)
    m_s[...] = jnp.full(m_s.shape, NEG_BIG, F32)
    l_s[...] = jnp.zeros_like(l_s)
    blocks_per_tile = KEY_TILE // SLC_BLOCK
    krow = lax.broadcasted_iota(jnp.int32, (KEY_TILE, 1), 0)

    def scores(kt, buf):
        k0 = pl.multiple_of(kt * KEY_TILE, KEY_TILE)
        kk = ks_ref[pl.ds(k0, KEY_TILE), :]
        for g in range(N_KV):
            buf[g] = _dot(kk, qm_s[:, gcols[g]])

    def tile(kt, buf, causal):
        k0 = pl.multiple_of(kt * KEY_TILE, KEY_TILE)
        vt = vst_ref[kt]
        probs, alphas = [], []
        for g in range(N_KV):
            b = jnp.concatenate(
                [jnp.broadcast_to(bias_s[pl.ds(kt * blocks_per_tile + j, 1), gcols[g]], (SLC_BLOCK, rq))
                 for j in range(blocks_per_tile)], axis=0)
            s = buf[g] + b
            if causal:
                s = jnp.where(k0 + krow <= colq, s, NEG_BIG)
            m_old = m_s[:, gcols[g]]
            m_new = jnp.maximum(m_old, jnp.max(s, axis=0, keepdims=True))
            alpha = jnp.exp(m_old - m_new)
            p = jnp.exp(s - m_new)
            l_s[:, gcols[g]] = alpha * l_s[:, gcols[g]] + jnp.sum(p, axis=0, keepdims=True)
            m_s[:, gcols[g]] = m_new
            probs.append(p.astype(BF16))
            alphas.append(alpha)
        for g in range(N_KV):
            acc_s[grows[g], :] = acc_s[grows[g], :] * alphas[g] + _dot(vt[grows[g], :], probs[g])

    last = (i * tq) // KEY_TILE

    scores(0, sa_s)

    def body(j, carry):
        scores(2 * j + 1, sb_s)
        tile(2 * j, sa_s, False)
        scores(2 * j + 2, sa_s)
        tile(2 * j + 1, sb_s, False)
        return carry

    lax.fori_loop(0, last // 2, body, 0)

    @pl.when(last % 2 == 0)
    def _():
        tile(last, sa_s, True)

    @pl.when(last % 2 == 1)
    def _():
        scores(last, sb_s)
        tile(last - 1, sa_s, False)
        tile(last, sb_s, True)

    rl = 1.0 / l_s[...]
    o_s = acc_s[...] * jnp.concatenate(
        [jnp.broadcast_to(rl[:, g * rq:(g + 1) * rq], (HEAD_DIM, rq)) for g in range(N_KV)], axis=0)

    gt = gt_ref[...]

    def gate(br):
        return jnp.concatenate(
            [jnp.concatenate(
                [jnp.broadcast_to(gt[(GROUP * g + r) * 3 + br:(GROUP * g + r) * 3 + br + 1, :], (HEAD_DIM, tq))
                 for r in range(GROUP)], axis=1) for g in range(N_KV)], axis=0)

    o = gate(0) * o_c + gate(1) * o_s + gate(2) * o_w
    for r in range(GROUP):
        out_ref[r * KV_W:(r + 1) * KV_W, :] = o[:, r * tq:(r + 1) * tq].astype(BF16)


def _nsa_prompt(q, gates, kc, vc, kvb):
    t = q.shape[0]
    nc = kc.shape[0]
    ns = t // SLC_BLOCK
    tq = Q_BLOCK
    qt = q.T
    gt = gates.T
    vct = vc.T
    ks = kvb[:, 0:KV_W]
    vst = kvb[:, KV_W:2 * KV_W].reshape(t // KEY_TILE, KEY_TILE, KV_W).transpose(0, 2, 1)
    kw = jnp.pad(kvb[:, 2 * KV_W:3 * KV_W], ((WINDOW, 0), (0, 0)))
    vwt = jnp.pad(kvb[:, 3 * KV_W:4 * KV_W], ((WINDOW, 0), (0, 0)))
    vwt = vwt.reshape((t + WINDOW) // tq, tq, KV_W).transpose(0, 2, 1)
    mt = jnp.asarray(_cmp_to_slc(nc, ns).T, BF16)
    full = lambda a: pl.BlockSpec(a.shape, lambda i: (0,) * a.ndim)
    ncol = N_KV * GROUP * tq
    out_t = pl.pallas_call(
        functools.partial(_nsa_prompt_kernel, nc=nc, ns=ns),
        grid=(t // tq,),
        in_specs=[pl.BlockSpec((D_ATTN, tq), lambda i: (0, i)),
                  pl.BlockSpec((LANES, tq), lambda i: (0, i)),
                  full(kc), full(vct), _vmem(), _vmem(), _vmem(), _vmem(), full(mt)],
        out_specs=pl.BlockSpec((D_ATTN, tq), lambda i: (0, i)),
        out_shape=jax.ShapeDtypeStruct((D_ATTN, t), BF16),
        scratch_shapes=[pltpu.VMEM((KV_W, ncol), BF16),
                        pltpu.VMEM((ns, ncol), F32),
                        pltpu.VMEM((KV_W, GROUP * tq), F32),
                        pltpu.VMEM((1, ncol), F32),
                        pltpu.VMEM((1, ncol), F32),
                        pltpu.VMEM((N_KV, KEY_TILE, GROUP * tq), F32),
                        pltpu.VMEM((N_KV, KEY_TILE, GROUP * tq), F32)],
        compiler_params=_cparams("arbitrary"),
        name="nsa_prompt",
    )(qt, gt, kc, vct, ks, vst, kw, vwt, mt)
    return out_t.T


def _nsa_sample_kernel(pt_ref, *refs, nc, ns, nsteps):
    npg = PAGES_PER_STEP
    pages = refs[:npg]
    (qrep_ref, gx_ref, kc_ref, vc_ref, sw_ref, new_ref, m_ref, e_ref, en_ref,
     out_ref, qall_s, sel_s, oc_s, ow_s, m_s, l_s, acc_s) = refs[npg:]
    del pt_ref
    i = pl.program_id(1)
    nr = GROUP * N_KV * 4
    row = lax.broadcasted_iota(jnp.int32, (nr, 1), 0)
    row_t = row % 4
    row_g = (row // 4) % N_KV
    lane_g = lax.broadcasted_iota(jnp.int32, (1, KV_W), 1) // HEAD_DIM
    own = row_g == lane_g
    new = new_ref[...]
    ucol = lax.broadcasted_iota(jnp.int32, (1, 8), 1)
    new_ok = (ucol <= row_t) & (ucol < 4)

    @pl.when(i == 0)
    def _():
        qall = jnp.where(own, qrep_ref[...], 0).astype(BF16)
        qall_s[...] = qall
        cidx = lax.broadcasted_iota(jnp.int32, (1, nc), 1)
        p = _masked_softmax(_dot_nt(qall, kc_ref[...]), cidx >= 1)
        oc_s[...] = _dot(p.astype(BF16), vc_ref[...])
        n16 = N_KV * 4
        psum = p[0:n16] + p[n16:2 * n16] + p[2 * n16:3 * n16] + p[3 * n16:4 * n16]
        imp = _split_dot(psum, m_ref[...])
        nb = imp.shape[1]
        blk = lax.broadcasted_iota(jnp.int32, (1, nb), 1)
        cur = ns - 1
        forced = (blk == 0) | (blk == cur) | (blk == cur - 1)
        score = jnp.where(blk <= cur, jnp.where(forced, FORCE_SCORE, imp), -jnp.inf)
        sel_s[...] = _select_blocks(score, min(TOP_N, ns)).astype(BF16)
        kw_t = sw_ref[0:KV_W, :].astype(BF16)
        vw_t = sw_ref[KV_W:2 * KV_W, :].astype(BF16)
        wcol = lax.broadcasted_iota(jnp.int32, (1, kw_t.shape[1]), 1)
        s1 = jnp.where(wcol > row_t, _dot(qall, kw_t), -jnp.inf)
        s2 = jnp.where(new_ok, _dot_nt(qall, new[:, 2 * KV_W:3 * KV_W]), -jnp.inf)
        mw = jnp.maximum(jnp.max(s1, axis=-1, keepdims=True), jnp.max(s2, axis=-1, keepdims=True))
        p1 = jnp.exp(s1 - mw)
        p2 = jnp.exp(s2 - mw)
        den = jnp.maximum(jnp.sum(p1, axis=-1, keepdims=True) + jnp.sum(p2, axis=-1, keepdims=True), 1e-30)
        ow_s[...] = (_dot_nt((p1 / den).astype(BF16), vw_t)
                     + _dot((p2 / den).astype(BF16), new[:, 3 * KV_W:4 * KV_W]))
        m_s[...] = jnp.full(m_s.shape, NEG_BIG, F32)
        l_s[...] = jnp.zeros_like(l_s)
        acc_s[...] = jnp.zeros_like(acc_s)

    qall = qall_s[...]
    kk_t = jnp.concatenate([pages[k][0:KV_W, :] for k in range(npg)], axis=1).astype(BF16)
    vv_t = jnp.concatenate([pages[k][KV_W:2 * KV_W, :] for k in range(npg)], axis=1).astype(BF16)
    mk = _dot(sel_s[...], e_ref[i]) > 0.5
    mk = jnp.concatenate([mk] * GROUP, axis=0)
    s = jnp.where(mk, _dot(qall, kk_t), NEG_BIG)
    m_old = m_s[...]
    m_new = jnp.maximum(m_old, jnp.max(s, axis=-1, keepdims=True))
    alpha = jnp.exp(m_old - m_new)
    p = jnp.exp(s - m_new)
    l_new = alpha * l_s[...] + jnp.sum(p, axis=-1, keepdims=True)
    acc_new = alpha * acc_s[...] + _dot_nt(p.astype(BF16), vv_t)
    m_s[...] = m_new
    l_s[...] = l_new
    acc_s[...] = acc_new

    @pl.when(i == nsteps - 1)
    def _():
        mkn = _dot(sel_s[...], en_ref[...]) > 0.5
        mkn = jnp.concatenate([mkn] * GROUP, axis=0) & new_ok
        sn = jnp.where(mkn, _dot_nt(qall, new[:, 0:KV_W]), NEG_BIG)
        m_f = jnp.maximum(m_new, jnp.max(sn, axis=-1, keepdims=True))
        al = jnp.exp(m_new - m_f)
        pn = jnp.exp(sn - m_f)
        l_f = al * l_new + jnp.sum(pn, axis=-1, keepdims=True)
        acc_f = al * acc_new + _dot(pn.astype(BF16), new[:, KV_W:2 * KV_W])
        o_s = acc_f / l_f
        out_ref[...] = gx_ref[0] * oc_s[...] + gx_ref[1] * o_s + gx_ref[2] * ow_s[...]


def _nsa_sample(pool, page_table, qrep, gx, kc, vc, sw, new8):
    b, n_pages = page_table.shape
    npg = PAGES_PER_STEP
    nsteps = n_pages // npg
    nc = kc.shape[1]
    past = n_pages * PAGE
    ns = past // SLC_BLOCK + 1
    nb = -(-ns // LANES) * LANES
    keys_step = npg * PAGE
    m = np.zeros((nc, nb), np.float32)
    m[:, :ns] = _cmp_to_slc(nc, ns)
    e = np.zeros((nsteps, nb, keys_step), np.float32)
    for st in range(nsteps):
        for j in range(keys_step):
            e[st, (st * keys_step + j) // SLC_BLOCK, j] = 1.0
    en = np.zeros((nb, 8), np.float32)
    en[ns - 1, :] = 1.0
    m, e, en = jnp.asarray(m, BF16), jnp.asarray(e, BF16), jnp.asarray(en, BF16)
    nr = GROUP * N_KV * 4

    def page_spec(k):
        return pl.BlockSpec((None, 2 * KV_W, PAGE), lambda bi, i, pt: (pt[bi, i * npg + k], 1, 0))

    per_b = lambda a: pl.BlockSpec((None,) + a.shape[1:], lambda bi, i, pt: (bi,) + (0,) * (a.ndim - 1))
    const = lambda a: pl.BlockSpec(a.shape, lambda bi, i, pt: (0,) * a.ndim)
    grid_spec = pltpu.PrefetchScalarGridSpec(
        num_scalar_prefetch=1,
        grid=(b, nsteps),
        in_specs=[page_spec(k) for k in range(npg)]
        + [per_b(qrep), per_b(gx), per_b(kc), per_b(vc), per_b(sw), per_b(new8), const(m), _vmem(), const(en)],
        out_specs=pl.BlockSpec((None, nr, KV_W), lambda bi, i, pt: (bi, 0, 0)),
        scratch_shapes=[pltpu.VMEM((nr, KV_W), BF16), pltpu.VMEM((N_KV * 4, nb), BF16),
                        pltpu.VMEM((nr, KV_W), F32), pltpu.VMEM((nr, KV_W), F32),
                        pltpu.VMEM((nr, 1), F32), pltpu.VMEM((nr, 1), F32), pltpu.VMEM((nr, KV_W), F32)],
    )
    return pl.pallas_call(
        functools.partial(_nsa_sample_kernel, nc=nc, ns=ns, nsteps=nsteps),
        grid_spec=grid_spec,
        out_shape=jax.ShapeDtypeStruct((b, nr, KV_W), F32),
        compiler_params=_cparams("arbitrary", "arbitrary"),
        name="nsa_sample",
    )(page_table, *([pool] * npg), qrep, gx, kc, vc, sw, new8, m, e, en)


def _outproj_kernel(x_ref, cy_ref, at_ref, ga1_ref, g2_ref, sc2_ref, sh2_ref, wo_ref, wq_ref, keys_ref,
                    x1_ref, h2_ref, s_ref, *, dc):
    u = _dot(cy_ref[...], wo_ref[0:dc, :]) + _dot(at_ref[...], wo_ref[dc:, :])
    x1 = x_ref[...] + ga1_ref[...] * u
    x1_ref[...] = x1
    h2 = _rms_mod(x1, g2_ref[...], sc2_ref[...], sh2_ref[...]).astype(BF16)
    h2_ref[...] = h2
    qp = _dot(h2, wq_ref[...]).astype(BF16)
    for hc in range(2 * PEER_HEADS):
        s_ref[:, hc * N_KEYS:(hc + 1) * N_KEYS] = _dot_nt(qp[:, hc * LANES:(hc + 1) * LANES], keys_ref[hc])


def _outproj(x, convy, attn, ga1, g2, sc2, sh2, wo, wq, keys, per_row):
    r, d = x.shape
    dc = d // 2
    tm = r if per_row else 256
    row_spec = lambda wdt: pl.BlockSpec((tm, wdt), lambda i: (i, 0))
    mod_spec = row_spec(d) if per_row else pl.BlockSpec((1, d), lambda i: (0, 0))
    nsc = keys.shape[0] * N_KEYS
    return pl.pallas_call(
        functools.partial(_outproj_kernel, dc=dc),
        grid=(r // tm,),
        in_specs=[row_spec(d), row_spec(dc), row_spec(d - dc), mod_spec,
                  pl.BlockSpec((1, d), lambda i: (0, 0)), mod_spec, mod_spec, _vmem(), _vmem(), _vmem()],
        out_specs=(row_spec(d), row_spec(d), row_spec(nsc)),
        out_shape=(jax.ShapeDtypeStruct((r, d), F32), jax.ShapeDtypeStruct((r, d), BF16),
                   jax.ShapeDtypeStruct((r, nsc), F32)),
        compiler_params=_cparams("arbitrary"),
        name="outproj_sample" if per_row else "outproj_prompt",
    )(x, convy, attn, ga1, g2, sc2, sh2, wo, wq, keys)


def _sort_network(n):
    pairs = []

    def merge(lo, cnt, r):
        step = r * 2
        if step < cnt:
            merge(lo, cnt, step)
            merge(lo + r, cnt, step)
            for i in range(lo + r, lo + cnt - r, step):
                pairs.append((i, i + r))
        else:
            pairs.append((lo, lo + r))

    def sort(lo, cnt):
        if cnt > 1:
            half = cnt // 2
            sort(lo, half)
            sort(lo + half, half)
            merge(lo, cnt, 1)

    sort(0, n)
    return pairs


def _prune_network(pairs, wanted):
    need = set(wanted)
    keep = []
    for i, j in reversed(pairs):
        if i in need or j in need:
            keep.append((i, j))
            need.update((i, j))
    return keep[::-1]


def _apply_network(x, pairs):
    x = list(x)
    for i, j in pairs:
        x[i], x[j] = jnp.maximum(x[i], x[j]), jnp.minimum(x[i], x[j])
    return x


_SORT16 = _sort_network(PEER_TOPK)
_CAND_PAIRS = [(a, b) for a in range(PEER_TOPK + 1) for b in range(PEER_TOPK + 1)
               if (a + 1) * (b + 1) <= PEER_TOPK + 1]
_CAND_WIRES = 64
_SELECT_16_17 = _prune_network(_sort_network(_CAND_WIRES), (PEER_TOPK - 1, PEER_TOPK))


def _top17(s):
    k = PEER_TOPK
    sub = 8
    lst = _apply_network([s[sub * v:sub * (v + 1)] for v in range(k)], _SORT16)
    for dist in (1, 2, 4):
        c = [jnp.maximum(lst[v], pltpu.roll(lst[k - 1 - v], dist, 0)) for v in range(k)]
        d = k // 2
        while d >= 1:
            for i in range(k):
                if i & d == 0:
                    c[i], c[i + d] = jnp.maximum(c[i], c[i + d]), jnp.minimum(c[i], c[i + d])
            d //= 2
        lst = c
    top = [v[sub - 1:sub, :] for v in lst]
    v16 = top[k - 1]
    ge = s >= v16
    cnt = jnp.sum(jnp.where(ge, 1.0, 0.0), axis=0, keepdims=True)
    below = jnp.max(jnp.where(ge, -jnp.inf, s), axis=0, keepdims=True)
    top.append(jnp.where(cnt > k, v16, below))
    return top


def _peer_select_kernel(s_ref, a2_ref, e2_ref, thr_ref, e1_ref):
    k = PEER_TOPK
    nh = PEER_HEADS
    half = lambda h, c: s_ref[(2 * h + c) * N_KEYS:(2 * h + c + 1) * N_KEYS, :]
    tops1 = [_top17(half(h, 0)) for h in range(nh)]
    tops2 = [_top17(half(h, 1)) for h in range(nh)]
    v1 = [jnp.concatenate([tops1[h][a] for h in range(nh)], axis=0) for a in range(k + 1)]
    v2 = [jnp.concatenate([tops2[h][a] for h in range(nh)], axis=0) for a in range(k + 1)]
    d1 = [v - v1[0] for v in v1]
    d2 = [v - v2[0] for v in v2]
    cands = [d1[a] + d2[b] for a, b in _CAND_PAIRS]
    pad = [jnp.full_like(cands[0], -jnp.inf)] * (_CAND_WIRES - len(cands))
    srt = _apply_network(cands + pad, _SELECT_16_17)
    tau = 0.5 * (srt[k - 1] + srt[k])
    x1 = [jnp.exp(d) for d in d1]
    x2 = [jnp.exp(d) for d in d2]
    z = jnp.zeros_like(tau)
    for (a, b), c in zip(_CAND_PAIRS, cands):
        z = z + jnp.where(c >= tau, x1[a] * x2[b], 0.0)
    rz = 1.0 / z
    for h in range(nh):
        a1 = half(h, 0) - v1[0][h:h + 1]
        a2 = half(h, 1) - v2[0][h:h + 1]
        thr_ref[h] = tau[h:h + 1] - a1
        e1_ref[h] = jnp.exp(a1) * rz[h:h + 1]
        a2_ref[h * N_KEYS:(h + 1) * N_KEYS, :] = a2
        e2_ref[h * N_KEYS:(h + 1) * N_KEYS, :] = jnp.exp(a2)


def _peer_select(s_t):
    nrow, tp = s_t.shape
    assert N_KEYS == 8 * PEER_TOPK
    tn = LANES
    hk = PEER_HEADS * N_KEYS
    flat = pl.BlockSpec((hk, tn), lambda t: (0, t))
    cube = pl.BlockSpec((PEER_HEADS, N_KEYS, tn), lambda t: (0, 0, t))
    return pl.pallas_call(
        _peer_select_kernel,
        grid=(tp // tn,),
        in_specs=[pl.BlockSpec((nrow, tn), lambda t: (0, t))],
        out_specs=(flat, flat, cube, cube),
        out_shape=(jax.ShapeDtypeStruct((hk, tp), F32),) * 2
        + (jax.ShapeDtypeStruct((PEER_HEADS, N_KEYS, tp), F32),) * 2,
        compiler_params=_cparams("arbitrary"),
        name="peer_select",
    )(s_t)


EXPERT_ROWS_PER_STEP = 4
PEER_TOKEN_TILE = 768


def _peer_dense_kernel(h_ref, u_ref, vt_ref, thr_ref, e1_ref, a2_ref, e2_ref, out_ref):
    c = pl.program_id(1)

    @pl.when(c == 0)
    def _():
        out_ref[...] = jnp.zeros_like(out_ref)

    act = jax.nn.gelu(_dot(u_ref[...], h_ref[...]))
    row0 = (c % (8 // EXPERT_ROWS_PER_STEP)) * EXPERT_ROWS_PER_STEP
    ws = []
    for ii in range(EXPERT_ROWS_PER_STEP):
        w = None
        for h in range(PEER_HEADS):
            hs = slice(h * N_KEYS, (h + 1) * N_KEYS)
            keep = a2_ref[hs, :] >= thr_ref[h, pl.ds(row0 + ii, 1), :]
            term = jnp.where(keep, e1_ref[h, pl.ds(row0 + ii, 1), :] * e2_ref[hs, :], 0.0)
            w = term if w is None else w + term
        ws.append(w)
    wg = (jnp.concatenate(ws, axis=0) * act).astype(BF16)
    out_ref[...] += _dot(vt_ref[...], wg)


def _peer_dense(h_t, u_bf, vt_bf, thr, e1, a2, e2, tm):
    d, tp = h_t.shape
    ne = u_bf.shape[0]
    ec = EXPERT_ROWS_PER_STEP * N_KEYS
    hk = PEER_HEADS * N_KEYS
    per8 = 8 // EXPERT_ROWS_PER_STEP
    tok = lambda rows: pl.BlockSpec((rows, tm), lambda t, c: (0, t))
    rows8 = pl.BlockSpec((PEER_HEADS, 8, tm), lambda t, c: (0, c // per8, t))
    return pl.pallas_call(
        _peer_dense_kernel,
        grid=(tp // tm, ne // ec),
        in_specs=[tok(d), pl.BlockSpec((ec, d), lambda t, c: (c, 0)), pl.BlockSpec((d, ec), lambda t, c: (0, c)),
                  rows8, rows8, tok(hk), tok(hk)],
        out_specs=tok(d),
        out_shape=jax.ShapeDtypeStruct((d, tp), F32),
        compiler_params=_cparams("arbitrary", "arbitrary"),
        name="peer_dense",
    )(h_t, u_bf, vt_bf, thr, e1, a2, e2)


def _final_kernel(x1_ref, f_ref, ga2_ref, fg_ref, y_ref):
    x2 = x1_ref[...] + ga2_ref[...] * f_ref[...]
    y = x2 * lax.rsqrt(jnp.mean(x2 * x2, axis=-1, keepdims=True) + EPS)
    y_ref[...] = y * fg_ref[...]


def _final(x1, f, ga2, fg, per_row):
    r, d = x1.shape
    tm = r if per_row else 256
    row_spec = pl.BlockSpec((tm, d), lambda i: (i, 0))
    vec_spec = pl.BlockSpec((1, d), lambda i: (0, 0))
    return pl.pallas_call(
        _final_kernel,
        grid=(r // tm,),
        in_specs=[row_spec, row_spec, row_spec if per_row else vec_spec, vec_spec],
        out_specs=row_spec,
        out_shape=jax.ShapeDtypeStruct((r, d), F32),
        compiler_params=_cparams("arbitrary"),
        name="final_sample" if per_row else "final_prompt",
    )(x1, f, ga2, fg)


def _prep_w_in(w_in):
    d = w_in.shape[0]
    q0 = 3 * (d // 2)
    wq = w_in[:, q0:q0 + D_ATTN].reshape(d, N_KV, GROUP, HEAD_DIM).transpose(0, 2, 1, 3).reshape(d, D_ATTN)
    rest = w_in[:, q0 + D_ATTN:]
    pad = jnp.zeros((d, LANES - 3 * N_HEADS), w_in.dtype)
    return jnp.concatenate([w_in[:, :q0], wq, rest, pad], axis=1).astype(BF16)


def _prep_w_out(w_out):
    d = w_out.shape[0]
    dc = d // 2
    wa = w_out[dc:].reshape(N_KV, GROUP, HEAD_DIM, d).transpose(1, 0, 2, 3).reshape(D_ATTN, d)
    return jnp.concatenate([w_out[:dc], wa], axis=0).astype(BF16)


def kernel(x_prompt, x_sample, c_prompt, c_sample, cache_kv, state_win, state_conv, page_table, w_ada, b_ada,
           norm1_g, norm2_g, w_in, conv_w, conv_b, cmp_pe, cmp_w1, cmp_b1, cmp_w2, cmp_b2, w_out, peer_wq,
           peer_keys, peer_u, peer_v, final_g):
    depth = w_ada.shape[0]
    assert depth == 1
    bp, t, d = x_prompt.shape
    assert bp == 1 and d - d // 2 == D_ATTN
    bs, ts, _ = x_sample.shape
    assert ts == 4
    dc = d // 2
    n_pages = page_table.shape[1]
    past = n_pages * PAGE
    wkeep = state_win.shape[2]
    assert wkeep == WINDOW and t % KEY_TILE == 0 and n_pages % PAGES_PER_STEP == 0
    rs = bs * ts

    n_c = 1 + bs
    n_cp = -(-n_c // 8) * 8
    c_all = jnp.concatenate([c_prompt, c_sample, jnp.zeros((n_cp - n_c, d), F32)], axis=0)
    ada = _ada(c_all, w_ada[0], b_ada[0])
    sh1, sc1, ga1, sh2, sc2, ga2 = [ada[:, k * d:(k + 1) * d] for k in range(6)]
    pr = lambda a: a[0:1]
    sm = lambda a: jnp.repeat(a[1:1 + bs], ts, axis=0)

    w_perm = _prep_w_in(w_in[0])
    wo_perm = _prep_w_out(w_out[0])
    g1 = norm1_g[0].reshape(1, d)
    g2 = norm2_g[0].reshape(1, d)
    cw = conv_w[0]
    cb = conv_b[0].reshape(1, dc)
    cmpw = _prep_cmp_weights(cmp_pe[0], cmp_w1[0], cmp_b1[0], cmp_w2[0], cmp_b2[0])

    xp = x_prompt[0]
    cos_p, sin_p = _rope_tables(jnp.arange(t))
    convy_p, q_p, rows_p, win_p, kvb_p, gates_p, zc_p = _proj(xp, g1, pr(sc1), pr(sh1), w_perm, cos_p, sin_p, cw, cb)
    ident = jnp.arange(t // PAGE, dtype=jnp.int32).reshape(1, t // PAGE)
    kc_p, vc_p = _compress(rows_p.reshape(t // PAGE, PAGE, 4 * KV_W), ident, cmpw, False)
    attn_p = _nsa_prompt(q_p, gates_p, kc_p[0], vc_p[0], kvb_p)
    wq_bf = peer_wq[0].astype(BF16)
    keys_bf = peer_keys[0].reshape(2 * PEER_HEADS, N_KEYS, -1).astype(BF16)
    x1_p, h2_p, s_p = _outproj(xp, convy_p, attn_p, pr(ga1), g2, pr(sc2), pr(sh2), wo_perm, wq_bf, keys_bf, False)

    xs = x_sample.reshape(rs, d)
    pos_s = jnp.tile(past + jnp.arange(ts), bs)
    cos_s, sin_s = _rope_tables(pos_s)
    st = state_conv[0]
    a1 = jnp.repeat(st[:, 1], ts, axis=0)
    a2 = jnp.stack([st[:, 0], st[:, 1], st[:, 1], st[:, 1]], axis=1).reshape(rs, dc)
    convy_s, q_s, rows_s, win_s, kvb_s, gates_s, zc_s = _proj(
        xs, g1, sm(sc1), sm(sh1), w_perm, cos_s, sin_s, cw, cb, a1, a2)
    pool = cache_kv[0].transpose(0, 2, 3, 4, 1).reshape(cache_kv.shape[1], 4 * KV_W, PAGE)
    kc_s, vc_s = _compress(pool, page_table, cmpw, True)
    qrep = jnp.broadcast_to(q_s.reshape(bs, ts, GROUP, 1, KV_W).transpose(0, 2, 3, 1, 4),
                            (bs, GROUP, N_KV, ts, KV_W)).reshape(bs, GROUP * N_KV * ts, KV_W)
    gts = gates_s[:, :3 * N_HEADS].reshape(bs, ts, N_KV, GROUP, 3).transpose(0, 4, 3, 2, 1)
    gx = jnp.broadcast_to(gts.reshape(bs, 3, GROUP * N_KV * ts, 1), (bs, 3, GROUP * N_KV * ts, KV_W))
    new8 = jnp.pad(kvb_s.reshape(bs, ts, 4 * KV_W), ((0, 0), (0, 8 - ts), (0, 0)))
    sw = state_win[0].transpose(0, 2, 3, 4, 1).reshape(bs, 2 * KV_W, wkeep)
    o_s = _nsa_sample(pool, page_table, qrep, gx, kc_s, vc_s, sw, new8)
    o6 = o_s.reshape(bs, GROUP, N_KV, ts, N_KV, HEAD_DIM)
    gi = jnp.arange(N_KV)
    attn_s = o6[:, :, gi, :, gi, :]
    attn_s = attn_s.transpose(1, 3, 2, 0, 4).reshape(rs, D_ATTN).astype(BF16)
    x1_s, h2_s, s_s = _outproj(xs, convy_s, attn_s, sm(ga1), g2, sm(sc2), sm(sh2), wo_perm, wq_bf, keys_bf, True)

    ntok = t + rs
    tm = PEER_TOKEN_TILE
    tp = -(-ntok // tm) * tm
    padt = lambda a: jnp.pad(jnp.concatenate(a, axis=0), ((0, tp - ntok), (0, 0))).T
    h_t = padt([h2_p, h2_s])
    s_t = padt([s_p, s_s])
    a2p, e2p, thr, e1p = _peer_select(s_t)
    u_bf = peer_u[0].astype(BF16)
    vt_bf = peer_v[0].astype(BF16).T
    f_t = _peer_dense(h_t, u_bf, vt_bf, thr, e1p, a2p, e2p, tm)
    f = f_t.T
    fg = final_g.reshape(1, d)
    y_p = _final(x1_p, f[:t], pr(ga2), fg, False)
    y_s = _final(x1_s, f[t:ntok], sm(ga2), fg, True)

    wmin = min(WINDOW, t)
    win_ctx = jnp.concatenate([state_win[0], win_s.reshape(bs, ts, 2, N_KV, HEAD_DIM)], axis=1)[:, ts:]
    zc3 = zc_s.reshape(bs, ts, dc)
    return (y_p.reshape(1, t, d), y_s.reshape(bs, ts, d),
            rows_p.reshape(1, 1, t, 4, N_KV, HEAD_DIM), rows_s.reshape(1, bs, ts, 4, N_KV, HEAD_DIM),
            win_p[t - wmin:].reshape(1, 1, wmin, 2, N_KV, HEAD_DIM), win_ctx[None],
            zc_p[8 - (3 - 1):].reshape(1, 1, 2, dc), zc3[:, ts - 2:][None])
```

```python
import functools

import numpy as np
import jax
import jax.numpy as jnp
from jax import lax
from jax.experimental import pallas as pl
from jax.experimental.pallas import tpu as pltpu

F32 = jnp.float32
BF16 = jnp.bfloat16

HEAD_DIM = 64
N_KV = 4
GROUP = 4
N_HEADS = N_KV * GROUP
KV_W = N_KV * HEAD_DIM
D_ATTN = N_HEADS * HEAD_DIM
CMP_STRIDE = 16
SLC_BLOCK = 64
TOP_N = 16
WINDOW = 512
Q_BLOCK = 128
PAGE = 128
FORCE_SCORE = 1.0e4
ROPE_THETA = 10000.0
N_KEYS = 128
PEER_HEADS = 8
PEER_TOPK = 16
EPS = 1e-6
LANES = 128
PAGES_PER_STEP = 16
KEY_TILE = 256
V7X_VMEM_LIMIT = 58 * 1024 * 1024
NEG_BIG = -1e30


def _cparams(*sem):
    return pltpu.CompilerParams(dimension_semantics=sem, vmem_limit_bytes=V7X_VMEM_LIMIT)


def _vmem():
    return pl.BlockSpec(memory_space=pltpu.VMEM)


def _dot(a, b):
    return jnp.dot(a, b, preferred_element_type=F32)


def _dot_nt(a, b):
    return lax.dot_general(a, b, (((1,), (1,)), ((), ())), preferred_element_type=F32)


def _split_dot(x, m):
    hi = x.astype(BF16)
    lo = (x - hi.astype(F32)).astype(BF16)
    return _dot(hi, m) + _dot(lo, m)


def _rope(x, cos, sinsg):
    w = x.shape[1]
    n = w // LANES
    c = jnp.concatenate([cos] * n, axis=1) if n > 1 else cos
    s = jnp.concatenate([sinsg] * n, axis=1) if n > 1 else sinsg
    lane = lax.broadcasted_iota(jnp.int32, x.shape, 1)
    first = (lane % HEAD_DIM) < (HEAD_DIM // 2)
    partner = jnp.where(first, pltpu.roll(x, w - HEAD_DIM // 2, 1), pltpu.roll(x, HEAD_DIM // 2, 1))
    return x * c + partner * s


def _rope_tables(pos):
    half = HEAD_DIM // 2
    inv = ROPE_THETA ** (-jnp.arange(half, dtype=F32) / half)
    ang = pos.astype(F32)[:, None] * inv[None, :]
    cos = jnp.cos(ang)
    sin = jnp.sin(ang)
    cos = jnp.concatenate([cos, cos, cos, cos], axis=1)
    sinsg = jnp.concatenate([-sin, sin, -sin, sin], axis=1)
    return cos, sinsg


def _rms_mod(xf, g, sc, sh):
    y = xf * lax.rsqrt(jnp.mean(xf * xf, axis=-1, keepdims=True) + EPS)
    return y * g * (1.0 + sc) + sh


def _ada_kernel(c_ref, w_ref, b_ref, o_ref):
    s = jax.nn.silu(c_ref[...]).astype(BF16)
    o_ref[...] = _dot(s, w_ref[...].astype(BF16)) + b_ref[...]


def _ada(c_all, w_ada, b_ada):
    r, d = c_all.shape
    n = w_ada.shape[1]
    tn = 1024
    return pl.pallas_call(
        _ada_kernel,
        grid=(n // tn,),
        in_specs=[pl.BlockSpec((r, d), lambda j: (0, 0)),
                  pl.BlockSpec((d, tn), lambda j: (0, j)),
                  pl.BlockSpec((1, tn), lambda j: (0, j))],
        out_specs=pl.BlockSpec((r, tn), lambda j: (0, j)),
        out_shape=jax.ShapeDtypeStruct((r, n), F32),
        compiler_params=_cparams("arbitrary"),
        name="ada",
    )(c_all, w_ada, b_ada.reshape(1, n))


def _proj_kernel(*refs, sample, tm, dc):
    if sample:
        (x_ref, g1_ref, sc_ref, sh_ref, w_ref, cos_ref, sin_ref, cw_ref, cb_ref, a1_ref, a2_ref,
         convy_ref, q_ref, rows_ref, win_ref, kvb_ref, gates_ref, zc_ref) = refs
    else:
        (x_ref, g1_ref, sc_ref, sh_ref, w_ref, cos_ref, sin_ref, cw_ref, cb_ref,
         convy_ref, q_ref, rows_ref, win_ref, kvb_ref, gates_ref, zc_ref, carry_ref) = refs
    h = _rms_mod(x_ref[...], g1_ref[...], sc_ref[...], sh_ref[...]).astype(BF16)

    def mm(a, b):
        return _dot(h, w_ref[:, a:b])

    cos = cos_ref[...]
    sin = sin_ref[...]
    q0 = 3 * dc
    kv0 = q0 + D_ATTN
    g0 = kv0 + 6 * KV_W

    b_g = mm(0, dc)
    zc = mm(dc, 2 * dc) * mm(2 * dc, 3 * dc)
    row = lax.broadcasted_iota(jnp.int32, zc.shape, 0)
    if sample:
        t = row % 4
        zm1 = jnp.where(t >= 1, pltpu.roll(zc, 1, 0), a1_ref[...])
        zm2 = jnp.where(t >= 2, pltpu.roll(zc, 2, 0), a2_ref[...])
        zc_ref[...] = zc
    else:
        @pl.when(pl.program_id(0) == 0)
        def _():
            carry_ref[...] = jnp.zeros_like(carry_ref)
        p1 = carry_ref[7:8, :]
        p2 = carry_ref[6:7, :]
        zm1 = jnp.where(row == 0, p1, pltpu.roll(zc, 1, 0))
        zm2 = jnp.where(row == 0, p2, jnp.where(row == 1, p1, pltpu.roll(zc, 2, 0)))
        carry_ref[...] = zc[tm - 8:tm, :]
        zc_ref[...] = zc[tm - 8:tm, :]
    conv = cb_ref[...] + cw_ref[0:1, :] * zm2
    conv = conv + cw_ref[1:2, :] * zm1
    conv = conv + cw_ref[2:3, :] * zc
    convy_ref[...] = (b_g * conv).astype(BF16)

    q = _rope(mm(q0, kv0), cos, sin) * (HEAD_DIM ** -0.5)
    q_ref[...] = q.astype(BF16)

    kv = mm(kv0, g0)
    k_cmp = kv[:, 0:KV_W]
    v_cmp = kv[:, KV_W:2 * KV_W]
    k_slc = _rope(kv[:, 2 * KV_W:3 * KV_W], cos, sin)
    v_slc = kv[:, 3 * KV_W:4 * KV_W]
    k_win = _rope(kv[:, 4 * KV_W:5 * KV_W], cos, sin)
    v_win = kv[:, 5 * KV_W:6 * KV_W]
    rows_ref[:, 0:KV_W] = k_cmp
    rows_ref[:, KV_W:2 * KV_W] = v_cmp
    rows_ref[:, 2 * KV_W:3 * KV_W] = k_slc
    rows_ref[:, 3 * KV_W:4 * KV_W] = v_slc
    win_ref[:, 0:KV_W] = k_win
    win_ref[:, KV_W:2 * KV_W] = v_win
    kvb_ref[:, 0:KV_W] = k_slc.astype(BF16)
    kvb_ref[:, KV_W:2 * KV_W] = v_slc.astype(BF16)
    kvb_ref[:, 2 * KV_W:3 * KV_W] = k_win.astype(BF16)
    kvb_ref[:, 3 * KV_W:4 * KV_W] = v_win.astype(BF16)
    gates_ref[...] = jax.nn.sigmoid(mm(g0, g0 + LANES))


def _proj(x, g1, sc, sh, w_perm, cos, sin, conv_w, conv_b, a1=None, a2=None):
    sample = a1 is not None
    r, d = x.shape
    dc = d // 2
    tm = r if sample else 256
    nw = w_perm.shape[1]
    row_spec = lambda wdt: pl.BlockSpec((tm, wdt), lambda i: (i, 0))
    full = lambda a: pl.BlockSpec(a.shape, lambda i: (0,) * a.ndim)
    mod_spec = row_spec(d) if sample else pl.BlockSpec((1, d), lambda i: (0, 0))
    in_specs = [row_spec(d), full(g1), mod_spec, mod_spec, _vmem(), row_spec(LANES), row_spec(LANES),
                full(conv_w), full(conv_b)]
    args = [x, g1, sc, sh, w_perm, cos, sin, conv_w, conv_b]
    scratch = []
    if sample:
        in_specs += [row_spec(dc), row_spec(dc)]
        args += [a1, a2]
        zc_shape = jax.ShapeDtypeStruct((r, dc), F32)
        zc_spec = row_spec(dc)
    else:
        zc_shape = jax.ShapeDtypeStruct((8, dc), F32)
        zc_spec = pl.BlockSpec((8, dc), lambda i: (0, 0))
        scratch = [pltpu.VMEM((8, dc), F32)]
    out_shape = (jax.ShapeDtypeStruct((r, dc), BF16), jax.ShapeDtypeStruct((r, D_ATTN), BF16),
                 jax.ShapeDtypeStruct((r, 4 * KV_W), F32), jax.ShapeDtypeStruct((r, 2 * KV_W), F32),
                 jax.ShapeDtypeStruct((r, 4 * KV_W), BF16), jax.ShapeDtypeStruct((r, LANES), F32), zc_shape)
    out_specs = (row_spec(dc), row_spec(D_ATTN), row_spec(4 * KV_W), row_spec(2 * KV_W),
                 row_spec(4 * KV_W), row_spec(LANES), zc_spec)
    return pl.pallas_call(
        functools.partial(_proj_kernel, sample=sample, tm=tm, dc=dc),
        grid=(r // tm,),
        in_specs=in_specs, out_specs=out_specs, out_shape=out_shape, scratch_shapes=scratch,
        compiler_params=_cparams("arbitrary"),
        name="proj_sample" if sample else "proj_prompt",
    )(*args)


def _cmp_kernel(pt_ref, *refs, transposed):
    npg = PAGES_PER_STEP
    pages = refs[:npg]
    (perm_ref, w1a_ref, w1b_ref, pea_ref, peb_ref, b1_ref, w2_ref, b2_ref, cos_ref, sin_ref,
     kc_ref, vc_ref, xa_s, xb_s, hprev_s) = refs[npg:]
    del pt_ref
    nch = PAGE // CMP_STRIDE
    rows = npg * nch

    @pl.when(pl.program_id(1) == 0)
    def _():
        hprev_s[...] = jnp.zeros_like(hprev_s)

    lane = lax.broadcasted_iota(jnp.int32, (nch, 2 * KV_W), 1)
    low = (lane % LANES) < HEAD_DIM
    perm = perm_ref[...]
    for k in range(npg):
        page = pages[k][...]
        for pe_ref, x_s in ((pea_ref, xa_s), (peb_ref, xb_s)):
            src = (page + pe_ref[...]).astype(BF16)
            y = _dot_nt(perm, src) if transposed else _dot(perm, src)
            for p in range(CMP_STRIDE // 2):
                ev = y[2 * p * nch:(2 * p + 1) * nch]
                od = y[(2 * p + 1) * nch:(2 * p + 2) * nch]
                a = jnp.where(low, ev, pltpu.roll(od, HEAD_DIM, 1))
                b = jnp.where(low, pltpu.roll(ev, 2 * KV_W - HEAD_DIM, 1), od)
                for j in range(4):
                    kv, ge = j // 2, 2 * (j % 2)
                    rsl = slice(nch * k, nch * (k + 1))
                    csl = slice(LANES * p, LANES * (p + 1))
                    x_s[kv, ge, rsl, csl] = a[:, LANES * j:LANES * (j + 1)]
                    x_s[kv, ge + 1, rsl, csl] = b[:, LANES * j:LANES * (j + 1)]

    row = lax.broadcasted_iota(jnp.int32, (N_KV * rows, 1), 0)
    for kv in range(2):
        ha = _dot(xa_s[kv].reshape(N_KV * rows, CMP_STRIDE * HEAD_DIM).astype(BF16), w1a_ref[kv])
        hb = _dot(xb_s[kv].reshape(N_KV * rows, CMP_STRIDE * HEAD_DIM).astype(BF16), w1b_ref[kv])
        hp = hprev_s[kv]
        prev = jnp.where(row % rows == 0, pltpu.roll(hp, N_KV * rows - (rows - 1), 0), pltpu.roll(ha, 1, 0))
        hprev_s[kv] = ha
        hid = jax.nn.gelu(prev + hb + b1_ref[kv])
        hcat = jnp.concatenate([hid[rows * g:rows * (g + 1)] for g in range(N_KV)], axis=1).astype(BF16)
        o = _dot(hcat, w2_ref[kv]) + b2_ref[kv]
        if kv == 0:
            kc_ref[...] = _rope(o, cos_ref[...], sin_ref[...]).astype(BF16)
        else:
            vc_ref[...] = o.astype(BF16)


def _compress(pool, page_table, cw, transposed):
    b, n_pages = page_table.shape
    npg = PAGES_PER_STEP
    nsteps = n_pages // npg
    rows = npg * (PAGE // CMP_STRIDE)
    nc = nsteps * rows
    cpos = CMP_STRIDE * jnp.arange(nc) + (CMP_STRIDE - 1)
    cos, sin = _rope_tables(cpos)
    page_block = (None, 2 * KV_W, PAGE) if transposed else (None, PAGE, 2 * KV_W)

    def page_spec(k):
        return pl.BlockSpec(page_block, lambda bi, i, pt: (pt[bi, i * npg + k], 0, 0))

    const = lambda a: pl.BlockSpec(a.shape, lambda bi, i, pt: (0,) * a.ndim)
    nch = PAGE // CMP_STRIDE
    perm = np.zeros((PAGE, PAGE), np.float32)
    for s in range(CMP_STRIDE):
        for c in range(nch):
            perm[s * nch + c, CMP_STRIDE * c + s] = 1.0
    pea, peb = (cw["pea"].T, cw["peb"].T) if transposed else (cw["pea"], cw["peb"])
    weights = [jnp.asarray(perm, BF16), cw["w1a"], cw["w1b"], pea, peb, cw["b1"], cw["w2bd"], cw["b2"]]
    tab_spec = pl.BlockSpec((rows, LANES), lambda bi, i, pt: (i, 0))
    out_spec = pl.BlockSpec((None, rows, KV_W), lambda bi, i, pt: (bi, i, 0))
    x_scratch = pltpu.VMEM((2, N_KV, rows, CMP_STRIDE * HEAD_DIM), F32)
    grid_spec = pltpu.PrefetchScalarGridSpec(
        num_scalar_prefetch=1,
        grid=(b, nsteps),
        in_specs=[page_spec(k) for k in range(npg)] + [const(a) for a in weights] + [tab_spec, tab_spec],
        out_specs=(out_spec, out_spec),
        scratch_shapes=[x_scratch, x_scratch, pltpu.VMEM((2, N_KV * rows, 2 * HEAD_DIM), F32)],
    )
    return pl.pallas_call(
        functools.partial(_cmp_kernel, transposed=transposed),
        grid_spec=grid_spec,
        out_shape=(jax.ShapeDtypeStruct((b, nc, KV_W), BF16), jax.ShapeDtypeStruct((b, nc, KV_W), BF16)),
        compiler_params=_cparams("arbitrary", "arbitrary"),
        name="compress",
    )(page_table, *([pool] * npg), *weights, cos, sin)


def _prep_cmp_weights(cmp_pe, cmp_w1, cmp_b1, cmp_w2, cmp_b2):
    s = CMP_STRIDE
    w2bd = jnp.zeros((2, N_KV * 2 * HEAD_DIM, KV_W), F32)
    for g in range(N_KV):
        w2bd = w2bd.at[:, g * 2 * HEAD_DIM:(g + 1) * 2 * HEAD_DIM, g * HEAD_DIM:(g + 1) * HEAD_DIM].set(cmp_w2)
    def pe_page(pe_half):
        t = jnp.broadcast_to(pe_half.transpose(1, 0, 2)[:, :, None, :], (s, 2, N_KV, HEAD_DIM))
        return jnp.tile(t.reshape(s, 2 * KV_W), (PAGE // s, 1))

    return {
        "w1a": cmp_w1[:, :s].reshape(2, s * HEAD_DIM, 2 * HEAD_DIM).astype(BF16),
        "w1b": cmp_w1[:, s:].reshape(2, s * HEAD_DIM, 2 * HEAD_DIM).astype(BF16),
        "pea": pe_page(cmp_pe[:, :s]),
        "peb": pe_page(cmp_pe[:, s:]),
        "b1": cmp_b1.reshape(2, 1, 2 * HEAD_DIM),
        "w2bd": w2bd.astype(BF16),
        "b2": jnp.tile(cmp_b2, (1, N_KV)).reshape(2, 1, KV_W),
    }


def _cmp_to_slc(nc, ns):
    j = np.arange(nc)[:, None] - 1
    b = np.arange(ns)[None, :]
    ov = (CMP_STRIDE * j < SLC_BLOCK * (b + 1)) & (CMP_STRIDE * j + 2 * CMP_STRIDE > SLC_BLOCK * b) & (j >= 0)
    return ov.astype(np.float32)


def _masked_softmax(s, mask):
    s = jnp.where(mask, s, -jnp.inf)
    m = jnp.max(s, axis=-1, keepdims=True)
    m = jnp.where(m > -jnp.inf, m, 0.0)
    p = jnp.exp(s - m)
    return p * (1.0 / jnp.maximum(jnp.sum(p, axis=-1, keepdims=True), 1e-30))


def _select_blocks(score, n_top):
    lane = lax.broadcasted_iota(jnp.int32, score.shape, 1)
    nb = score.shape[1]
    sel = jnp.zeros(score.shape, F32)
    cur = score
    for _ in range(n_top):
        mx = jnp.max(cur, axis=-1, keepdims=True)
        first = jnp.min(jnp.where(cur == mx, lane, nb), axis=-1, keepdims=True)
        hit = (lane == first) & (mx > -jnp.inf)
        sel = jnp.where(hit, 1.0, sel)
        cur = jnp.where(lane == first, -jnp.inf, cur)
    return sel


def _masked_softmax0(s, mask):
    s = jnp.where(mask, s, -jnp.inf)
    m = jnp.max(s, axis=0, keepdims=True)
    m = jnp.where(m > -jnp.inf, m, 0.0)
    p = jnp.exp(s - m)
    return p * (1.0 / jnp.maximum(jnp.sum(p, axis=0, keepdims=True), 1e-30))


def _select_blocks_t(score, n_top):
    rowi = lax.broadcasted_iota(jnp.int32, score.shape, 0)
    nb = score.shape[0]
    sel = jnp.zeros(score.shape, F32)
    cur = score
    for _ in range(n_top):
        mx = jnp.max(cur, axis=0, keepdims=True)
        first = jnp.min(jnp.where(cur == mx, rowi, nb), axis=0, keepdims=True)
        is_first = rowi == first
        sel = jnp.where(is_first & (mx > -jnp.inf), 1.0, sel)
        cur = jnp.where(is_first, -jnp.inf, cur)
    return sel


def _own_blocks(x, tq):
    rq = GROUP * tq
    return jnp.concatenate([x[g * HEAD_DIM:(g + 1) * HEAD_DIM, g * rq:(g + 1) * rq] for g in range(N_KV)], axis=0)


def _nsa_prompt_kernel(qt_ref, gt_ref, kc_ref, vct_ref, ks_ref, vst_ref, kw_ref, vwt_ref, mt_ref,
                       out_ref, qm_s, bias_s, acc_s, m_s, l_s, sa_s, sb_s, *, nc, ns):
    i = pl.program_id(0)
    tq = Q_BLOCK
    rq = GROUP * tq
    ncol = N_KV * rq
    colq = i * tq + lax.broadcasted_iota(jnp.int32, (1, rq), 1) % tq
    colq_g = i * tq + lax.broadcasted_iota(jnp.int32, (1, N_KV * tq), 1) % tq
    grows = [slice(g * HEAD_DIM, (g + 1) * HEAD_DIM) for g in range(N_KV)]
    gcols = [slice(g * rq, (g + 1) * rq) for g in range(N_KV)]

    row_g = lax.broadcasted_iota(jnp.int32, (KV_W, tq), 0) // HEAD_DIM
    qt = qt_ref[...]
    for g in range(N_KV):
        for r in range(GROUP):
            c0 = (g * GROUP + r) * tq
            qm_s[:, c0:c0 + tq] = jnp.where(row_g == g, qt[r * KV_W:(r + 1) * KV_W, :], 0).astype(BF16)

    crow = lax.broadcasted_iota(jnp.int32, (nc, 1), 0)
    cvalid = (crow >= 1) & (CMP_STRIDE * crow + (CMP_STRIDE - 1) <= colq)
    kc = kc_ref[...]
    o_c, psum = [], []
    scores = [_dot(kc, qm_s[:, gcols[g]]) for g in range(N_KV)]
    for g in range(N_KV):
        p = _masked_softmax0(scores[g], cvalid)
        o_c.append(_dot(vct_ref[grows[g], :], p.astype(BF16)))
        psum.append(sum(p[:, r * tq:(r + 1) * tq] for r in range(GROUP)))
    o_c = jnp.concatenate(o_c, axis=0)
    psum = jnp.concatenate(psum, axis=1)
    hi = psum.astype(BF16)
    lo = (psum - hi.astype(F32)).astype(BF16)
    imp = _dot(mt_ref[...], hi) + _dot(mt_ref[...], lo)
    blk = lax.broadcasted_iota(jnp.int32, (ns, 1), 0)
    cur = colq_g // SLC_BLOCK
    forced = (blk == 0) | (blk == cur) | (blk == cur - 1)
    score = jnp.where(blk <= cur, jnp.where(forced, FORCE_SCORE, imp), -jnp.inf)
    bias = (_select_blocks_t(score, min(TOP_N, ns)) - 1.0) * (-NEG_BIG)
    for g in range(N_KV):
        for r in range(GROUP):
            c0 = (g * GROUP + r) * tq
            bias_s[:, c0:c0 + tq] = bias[:, g * tq:(g + 1) * tq]

    w0 = pl.multiple_of(i * tq, tq)
    kw = kw_ref[pl.ds(w0, WINDOW + tq), :]
    nwt = (WINDOW + tq) // tq
    vwt = vwt_ref[pl.ds(i, nwt)]
    vwt = jnp.concatenate([vwt[j] for j in range(nwt)], axis=1)
    wpos = i * tq - WINDOW + lax.broadcasted_iota(jnp.int32, (WINDOW + tq, 1), 0)
    dist = colq - wpos
    wmask = (dist >= 0) & (dist < WINDOW) & (wpos >= 0)
    o_w = []
    scores = [_dot(kw, qm_s[:, gcols[g]]) for g in range(N_KV)]
    for g in range(N_KV):
        p = _masked_softmax0(scores[g], wmask)
        o_w.append(_dot(vwt[grows[g], :], p.astype(BF16)))
    o_w = jnp.concatenate(o_w, axis=0)

    acc_s[...] = jnp.zeros_like(acc_s)
    m_s[...] = jnp.full(m_s.shape, NEG_BIG, F32)
    l_s[...] = jnp.zeros_like(l_s)
    blocks_per_tile = KEY_TILE // SLC_BLOCK
    krow = lax.broadcasted_iota(jnp.int32, (KEY_TILE, 1), 0)

    def scores(kt, buf):
        k0 = pl.multiple_of(kt * KEY_TILE, KEY_TILE)
        kk = ks_ref[pl.ds(k0, KEY_TILE), :]
        for g in range(N_KV):
            buf[g] = _dot(kk, qm_s[:, gcols[g]])

    def tile(kt, buf, causal):
        k0 = pl.multiple_of(kt * KEY_TILE, KEY_TILE)
        vt = vst_ref[kt]
        probs, alphas = [], []
        for g in range(N_KV):
            b = jnp.concatenate(
                [jnp.broadcast_to(bias_s[pl.ds(kt * blocks_per_tile + j, 1), gcols[g]], (SLC_BLOCK, rq))
                 for j in range(blocks_per_tile)], axis=0)
            s = buf[g] + b
            if causal:
                s = jnp.where(k0 + krow <= colq, s, NEG_BIG)
            m_old = m_s[:, gcols[g]]
            m_new = jnp.maximum(m_old, jnp.max(s, axis=0, keepdims=True))
            alpha = jnp.exp(m_old - m_new)
            p = jnp.exp(s - m_new)
            l_s[:, gcols[g]] = alpha * l_s[:, gcols[g]] + jnp.sum(p, axis=0, keepdims=True)
            m_s[:, gcols[g]] = m_new
            probs.append(p.astype(BF16))
            alphas.append(alpha)
        for g in range(N_KV):
            acc_s[grows[g], :] = acc_s[grows[g], :] * alphas[g] + _dot(vt[grows[g], :], probs[g])

    last = (i * tq) // KEY_TILE

    scores(0, sa_s)

    def body(j, carry):
        scores(2 * j + 1, sb_s)
        tile(2 * j, sa_s, False)
        scores(2 * j + 2, sa_s)
        tile(2 * j + 1, sb_s, False)
        return carry

    lax.fori_loop(0, last // 2, body, 0)

    @pl.when(last % 2 == 0)
    def _():
        tile(last, sa_s, True)

    @pl.when(last % 2 == 1)
    def _():
        scores(last, sb_s)
        tile(last - 1, sa_s, False)
        tile(last, sb_s, True)

    rl = 1.0 / l_s[...]
    o_s = acc_s[...] * jnp.concatenate(
        [jnp.broadcast_to(rl[:, g * rq:(g + 1) * rq], (HEAD_DIM, rq)) for g in range(N_KV)], axis=0)

    gt = gt_ref[...]

    def gate(br):
        return jnp.concatenate(
            [jnp.concatenate(
                [jnp.broadcast_to(gt[(GROUP * g + r) * 3 + br:(GROUP * g + r) * 3 + br + 1, :], (HEAD_DIM, tq))
                 for r in range(GROUP)], axis=1) for g in range(N_KV)], axis=0)

    o = gate(0) * o_c + gate(1) * o_s + gate(2) * o_w
    for r in range(GROUP):
        out_ref[r * KV_W:(r + 1) * KV_W, :] = o[:, r * tq:(r + 1) * tq].astype(BF16)


def _nsa_prompt(q, gates, kc, vc, kvb):
    t = q.shape[0]
    nc = kc.shape[0]
    ns = t // SLC_BLOCK
    tq = Q_BLOCK
    qt = q.T
    gt = gates.T
    vct = vc.T
    ks = kvb[:, 0:KV_W]
    vst = kvb[:, KV_W:2 * KV_W].reshape(t // KEY_TILE, KEY_TILE, KV_W).transpose(0, 2, 1)
    kw = jnp.pad(kvb[:, 2 * KV_W:3 * KV_W], ((WINDOW, 0), (0, 0)))
    vwt = jnp.pad(kvb[:, 3 * KV_W:4 * KV_W], ((WINDOW, 0), (0, 0)))
    vwt = vwt.reshape((t + WINDOW) // tq, tq, KV_W).transpose(0, 2, 1)
    mt = jnp.asarray(_cmp_to_slc(nc, ns).T, BF16)
    full = lambda a: pl.BlockSpec(a.shape, lambda i: (0,) * a.ndim)
    ncol = N_KV * GROUP * tq
    out_t = pl.pallas_call(
        functools.partial(_nsa_prompt_kernel, nc=nc, ns=ns),
        grid=(t // tq,),
        in_specs=[pl.BlockSpec((D_ATTN, tq), lambda i: (0, i)),
                  pl.BlockSpec((LANES, tq), lambda i: (0, i)),
                  full(kc), full(vct), _vmem(), _vmem(), _vmem(), _vmem(), full(mt)],
        out_specs=pl.BlockSpec((D_ATTN, tq), lambda i: (0, i)),
        out_shape=jax.ShapeDtypeStruct((D_ATTN, t), BF16),
        scratch_shapes=[pltpu.VMEM((KV_W, ncol), BF16),
                        pltpu.VMEM((ns, ncol), F32),
                        pltpu.VMEM((KV_W, GROUP * tq), F32),
                        pltpu.VMEM((1, ncol), F32),
                        pltpu.VMEM((1, ncol), F32),
                        pltpu.VMEM((N_KV, KEY_TILE, GROUP * tq), F32),
                        pltpu.VMEM((N_KV, KEY_TILE, GROUP * tq), F32)],
        compiler_params=_cparams("arbitrary"),
        name="nsa_prompt",
    )(qt, gt, kc, vct, ks, vst, kw, vwt, mt)
    return out_t.T


def _nsa_sample_kernel(pt_ref, *refs, nc, ns, nsteps):
    npg = PAGES_PER_STEP
    pages = refs[:npg]
    (qrep_ref, gx_ref, kc_ref, vc_ref, sw_ref, new_ref, m_ref, e_ref, en_ref,
     out_ref, qall_s, sel_s, oc_s, ow_s, m_s, l_s, acc_s) = refs[npg:]
    del pt_ref
    i = pl.program_id(1)
    nr = GROUP * N_KV * 4
    row = lax.broadcasted_iota(jnp.int32, (nr, 1), 0)
    row_t = row % 4
    row_g = (row // 4) % N_KV
    lane_g = lax.broadcasted_iota(jnp.int32, (1, KV_W), 1) // HEAD_DIM
    own = row_g == lane_g
    new = new_ref[...]
    ucol = lax.broadcasted_iota(jnp.int32, (1, 8), 1)
    new_ok = (ucol <= row_t) & (ucol < 4)

    @pl.when(i == 0)
    def _():
        qall = jnp.where(own, qrep_ref[...], 0).astype(BF16)
        qall_s[...] = qall
        cidx = lax.broadcasted_iota(jnp.int32, (1, nc), 1)
        p = _masked_softmax(_dot_nt(qall, kc_ref[...]), cidx >= 1)
        oc_s[...] = _dot(p.astype(BF16), vc_ref[...])
        n16 = N_KV * 4
        psum = p[0:n16] + p[n16:2 * n16] + p[2 * n16:3 * n16] + p[3 * n16:4 * n16]
        imp = _split_dot(psum, m_ref[...])
        nb = imp.shape[1]
        blk = lax.broadcasted_iota(jnp.int32, (1, nb), 1)
        cur = ns - 1
        forced = (blk == 0) | (blk == cur) | (blk == cur - 1)
        score = jnp.where(blk <= cur, jnp.where(forced, FORCE_SCORE, imp), -jnp.inf)
        sel_s[...] = _select_blocks(score, min(TOP_N, ns)).astype(BF16)
        kw_t = sw_ref[0:KV_W, :].astype(BF16)
        vw_t = sw_ref[KV_W:2 * KV_W, :].astype(BF16)
        wcol = lax.broadcasted_iota(jnp.int32, (1, kw_t.shape[1]), 1)
        s1 = jnp.where(wcol > row_t, _dot(qall, kw_t), -jnp.inf)
        s2 = jnp.where(new_ok, _dot_nt(qall, new[:, 2 * KV_W:3 * KV_W]), -jnp.inf)
        mw = jnp.maximum(jnp.max(s1, axis=-1, keepdims=True), jnp.max(s2, axis=-1, keepdims=True))
        p1 = jnp.exp(s1 - mw)
        p2 = jnp.exp(s2 - mw)
        den = jnp.maximum(jnp.sum(p1, axis=-1, keepdims=True) + jnp.sum(p2, axis=-1, keepdims=True), 1e-30)
        ow_s[...] = (_dot_nt((p1 / den).astype(BF16), vw_t)
                     + _dot((p2 / den).astype(BF16), new[:, 3 * KV_W:4 * KV_W]))
        m_s[...] = jnp.full(m_s.shape, NEG_BIG, F32)
        l_s[...] = jnp.zeros_like(l_s)
        acc_s[...] = jnp.zeros_like(acc_s)

    qall = qall_s[...]
    kk_t = jnp.concatenate([pages[k][0:KV_W, :] for k in range(npg)], axis=1).astype(BF16)
    vv_t = jnp.concatenate([pages[k][KV_W:2 * KV_W, :] for k in range(npg)], axis=1).astype(BF16)
    mk = _dot(sel_s[...], e_ref[i]) > 0.5
    mk = jnp.concatenate([mk] * GROUP, axis=0)
    s = jnp.where(mk, _dot(qall, kk_t), NEG_BIG)
    m_old = m_s[...]
    m_new = jnp.maximum(m_old, jnp.max(s, axis=-1, keepdims=True))
    alpha = jnp.exp(m_old - m_new)
    p = jnp.exp(s - m_new)
    l_new = alpha * l_s[...] + jnp.sum(p, axis=-1, keepdims=True)
    acc_new = alpha * acc_s[...] + _dot_nt(p.astype(BF16), vv_t)
    m_s[...] = m_new
    l_s[...] = l_new
    acc_s[...] = acc_new

    @pl.when(i == nsteps - 1)
    def _():
        mkn = _dot(sel_s[...], en_ref[...]) > 0.5
        mkn = jnp.concatenate([mkn] * GROUP, axis=0) & new_ok
        sn = jnp.where(mkn, _dot_nt(qall, new[:, 0:KV_W]), NEG_BIG)
        m_f = jnp.maximum(m_new, jnp.max(sn, axis=-1, keepdims=True))
        al = jnp.exp(m_new - m_f)
        pn = jnp.exp(sn - m_f)
        l_f = al * l_new + jnp.sum(pn, axis=-1, keepdims=True)
        acc_f = al * acc_new + _dot(pn.astype(BF16), new[:, KV_W:2 * KV_W])
        o_s = acc_f / l_f
        out_ref[...] = gx_ref[0] * oc_s[...] + gx_ref[1] * o_s + gx_ref[2] * ow_s[...]


def _nsa_sample(pool, page_table, qrep, gx, kc, vc, sw, new8):
    b, n_pages = page_table.shape
    npg = PAGES_PER_STEP
    nsteps = n_pages // npg
    nc = kc.shape[1]
    past = n_pages * PAGE
    ns = past // SLC_BLOCK + 1
    nb = -(-ns // LANES) * LANES
    keys_step = npg * PAGE
    m = np.zeros((nc, nb), np.float32)
    m[:, :ns] = _cmp_to_slc(nc, ns)
    e = np.zeros((nsteps, nb, keys_step), np.float32)
    for st in range(nsteps):
        for j in range(keys_step):
            e[st, (st * keys_step + j) // SLC_BLOCK, j] = 1.0
    en = np.zeros((nb, 8), np.float32)
    en[ns - 1, :] = 1.0
    m, e, en = jnp.asarray(m, BF16), jnp.asarray(e, BF16), jnp.asarray(en, BF16)
    nr = GROUP * N_KV * 4

    def page_spec(k):
        return pl.BlockSpec((None, 2 * KV_W, PAGE), lambda bi, i, pt: (pt[bi, i * npg + k], 1, 0))

    per_b = lambda a: pl.BlockSpec((None,) + a.shape[1:], lambda bi, i, pt: (bi,) + (0,) * (a.ndim - 1))
    const = lambda a: pl.BlockSpec(a.shape, lambda bi, i, pt: (0,) * a.ndim)
    grid_spec = pltpu.PrefetchScalarGridSpec(
        num_scalar_prefetch=1,
        grid=(b, nsteps),
        in_specs=[page_spec(k) for k in range(npg)]
        + [per_b(qrep), per_b(gx), per_b(kc), per_b(vc), per_b(sw), per_b(new8), const(m), _vmem(), const(en)],
        out_specs=pl.BlockSpec((None, nr, KV_W), lambda bi, i, pt: (bi, 0, 0)),
        scratch_shapes=[pltpu.VMEM((nr, KV_W), BF16), pltpu.VMEM((N_KV * 4, nb), BF16),
                        pltpu.VMEM((nr, KV_W), F32), pltpu.VMEM((nr, KV_W), F32),
                        pltpu.VMEM((nr, 1), F32), pltpu.VMEM((nr, 1), F32), pltpu.VMEM((nr, KV_W), F32)],
    )
    return pl.pallas_call(
        functools.partial(_nsa_sample_kernel, nc=nc, ns=ns, nsteps=nsteps),
        grid_spec=grid_spec,
        out_shape=jax.ShapeDtypeStruct((b, nr, KV_W), F32),
        compiler_params=_cparams("arbitrary", "arbitrary"),
        name="nsa_sample",
    )(page_table, *([pool] * npg), qrep, gx, kc, vc, sw, new8, m, e, en)


def _outproj_kernel(x_ref, cy_ref, at_ref, ga1_ref, g2_ref, sc2_ref, sh2_ref, wo_ref, wq_ref, keys_ref,
                    x1_ref, h2_ref, s_ref, *, dc):
    u = _dot(cy_ref[...], wo_ref[0:dc, :]) + _dot(at_ref[...], wo_ref[dc:, :])
    x1 = x_ref[...] + ga1_ref[...] * u
    x1_ref[...] = x1
    h2 = _rms_mod(x1, g2_ref[...], sc2_ref[...], sh2_ref[...]).astype(BF16)
    h2_ref[...] = h2
    qp = _dot(h2, wq_ref[...]).astype(BF16)
    for hc in range(2 * PEER_HEADS):
        s_ref[:, hc * N_KEYS:(hc + 1) * N_KEYS] = _dot_nt(qp[:, hc * LANES:(hc + 1) * LANES], keys_ref[hc])


def _outproj(x, convy, attn, ga1, g2, sc2, sh2, wo, wq, keys, per_row):
    r, d = x.shape
    dc = d // 2
    tm = r if per_row else 256
    row_spec = lambda wdt: pl.BlockSpec((tm, wdt), lambda i: (i, 0))
    mod_spec = row_spec(d) if per_row else pl.BlockSpec((1, d), lambda i: (0, 0))
    nsc = keys.shape[0] * N_KEYS
    return pl.pallas_call(
        functools.partial(_outproj_kernel, dc=dc),
        grid=(r // tm,),
        in_specs=[row_spec(d), row_spec(dc), row_spec(d - dc), mod_spec,
                  pl.BlockSpec((1, d), lambda i: (0, 0)), mod_spec, mod_spec, _vmem(), _vmem(), _vmem()],
        out_specs=(row_spec(d), row_spec(d), row_spec(nsc)),
        out_shape=(jax.ShapeDtypeStruct((r, d), F32), jax.ShapeDtypeStruct((r, d), BF16),
                   jax.ShapeDtypeStruct((r, nsc), F32)),
        compiler_params=_cparams("arbitrary"),
        name="outproj_sample" if per_row else "outproj_prompt",
    )(x, convy, attn, ga1, g2, sc2, sh2, wo, wq, keys)


def _sort_network(n):
    pairs = []

    def merge(lo, cnt, r):
        step = r * 2
        if step < cnt:
            merge(lo, cnt, step)
            merge(lo + r, cnt, step)
            for i in range(lo + r, lo + cnt - r, step):
                pairs.append((i, i + r))
        else:
            pairs.append((lo, lo + r))

    def sort(lo, cnt):
        if cnt > 1:
            half = cnt // 2
            sort(lo, half)
            sort(lo + half, half)
            merge(lo, cnt, 1)

    sort(0, n)
    return pairs


def _prune_network(pairs, wanted):
    need = set(wanted)
    keep = []
    for i, j in reversed(pairs):
        if i in need or j in need:
            keep.append((i, j))
            need.update((i, j))
    return keep[::-1]


def _apply_network(x, pairs):
    x = list(x)
    for i, j in pairs:
        x[i], x[j] = jnp.maximum(x[i], x[j]), jnp.minimum(x[i], x[j])
    return x


_SORT16 = _sort_network(PEER_TOPK)
_CAND_PAIRS = [(a, b) for a in range(PEER_TOPK + 1) for b in range(PEER_TOPK + 1)
               if (a + 1) * (b + 1) <= PEER_TOPK + 1]
_CAND_WIRES = 64
_SELECT_16_17 = _prune_network(_sort_network(_CAND_WIRES), (PEER_TOPK - 1, PEER_TOPK))


def _top17(s):
    k = PEER_TOPK
    sub = 8
    lst = _apply_network([s[sub * v:sub * (v + 1)] for v in range(k)], _SORT16)
    for dist in (1, 2, 4):
        c = [jnp.maximum(lst[v], pltpu.roll(lst[k - 1 - v], dist, 0)) for v in range(k)]
        d = k // 2
        while d >= 1:
            for i in range(k):
                if i & d == 0:
                    c[i], c[i + d] = jnp.maximum(c[i], c[i + d]), jnp.minimum(c[i], c[i + d])
            d //= 2
        lst = c
    top = [v[sub - 1:sub, :] for v in lst]
    v16 = top[k - 1]
    ge = s >= v16
    cnt = jnp.sum(jnp.where(ge, 1.0, 0.0), axis=0, keepdims=True)
    below = jnp.max(jnp.where(ge, -jnp.inf, s), axis=0, keepdims=True)
    top.append(jnp.where(cnt > k, v16, below))
    return top


def _peer_select_kernel(s_ref, a2_ref, e2_ref, thr_ref, e1_ref):
    k = PEER_TOPK
    nh = PEER_HEADS
    half = lambda h, c: s_ref[(2 * h + c) * N_KEYS:(2 * h + c + 1) * N_KEYS, :]
    tops1 = [_top17(half(h, 0)) for h in range(nh)]
    tops2 = [_top17(half(h, 1)) for h in range(nh)]
    v1 = [jnp.concatenate([tops1[h][a] for h in range(nh)], axis=0) for a in range(k + 1)]
    v2 = [jnp.concatenate([tops2[h][a] for h in range(nh)], axis=0) for a in range(k + 1)]
    d1 = [v - v1[0] for v in v1]
    d2 = [v - v2[0] for v in v2]
    cands = [d1[a] + d2[b] for a, b in _CAND_PAIRS]
    pad = [jnp.full_like(cands[0], -jnp.inf)] * (_CAND_WIRES - len(cands))
    srt = _apply_network(cands + pad, _SELECT_16_17)
    tau = 0.5 * (srt[k - 1] + srt[k])
    x1 = [jnp.exp(d) for d in d1]
    x2 = [jnp.exp(d) for d in d2]
    z = jnp.zeros_like(tau)
    for (a, b), c in zip(_CAND_PAIRS, cands):
        z = z + jnp.where(c >= tau, x1[a] * x2[b], 0.0)
    rz = 1.0 / z
    ic = EXPERT_ROWS_PER_STEP
    for h in range(nh):
        a1 = half(h, 0) - v1[0][h:h + 1]
        a2 = half(h, 1) - v2[0][h:h + 1]
        thr = tau[h:h + 1] - a1
        e1 = jnp.exp(a1) * rz[h:h + 1]
        for c in range(N_KEYS // ic):
            thr_ref[c, h * ic:(h + 1) * ic, :] = thr[c * ic:(c + 1) * ic]
            e1_ref[c, h * ic:(h + 1) * ic, :] = e1[c * ic:(c + 1) * ic]
        a2_ref[h * N_KEYS:(h + 1) * N_KEYS, :] = a2
        e2_ref[h * N_KEYS:(h + 1) * N_KEYS, :] = jnp.exp(a2)


def _peer_select(s_t):
    nrow, tp = s_t.shape
    assert N_KEYS == 8 * PEER_TOPK
    tn = LANES
    hk = PEER_HEADS * N_KEYS
    ic = EXPERT_ROWS_PER_STEP
    flat = pl.BlockSpec((hk, tn), lambda t: (0, t))
    cube = pl.BlockSpec((N_KEYS // ic, PEER_HEADS * ic, tn), lambda t: (0, 0, t))
    return pl.pallas_call(
        _peer_select_kernel,
        grid=(tp // tn,),
        in_specs=[pl.BlockSpec((nrow, tn), lambda t: (0, t))],
        out_specs=(flat, flat, cube, cube),
        out_shape=(jax.ShapeDtypeStruct((hk, tp), F32),) * 2
        + (jax.ShapeDtypeStruct((N_KEYS // ic, PEER_HEADS * ic, tp), F32),) * 2,
        compiler_params=_cparams("arbitrary"),
        name="peer_select",
    )(s_t)


EXPERT_ROWS_PER_STEP = 8
PEER_TOKEN_TILE = 768


def _peer_dense_kernel(h_ref, u_ref, vt_ref, thr_ref, e1_ref, a2_ref, e2_ref, out_ref, act_s, wg_s, acc_s):
    c = pl.program_id(1)

    @pl.when(c == 0)
    def _():
        acc_s[...] = jnp.zeros_like(acc_s)

    ic = EXPERT_ROWS_PER_STEP
    halves = [slice(k * (ic // 2) * N_KEYS, (k + 1) * (ic // 2) * N_KEYS) for k in range(2)]
    acts = [_dot(u_ref[hf, :], h_ref[...]) for hf in halves]
    for k, hf in enumerate(halves):
        act_s[hf, :] = jax.nn.gelu(acts[k])
        for ii in range(k * (ic // 2), (k + 1) * (ic // 2)):
            es = slice(ii * N_KEYS, (ii + 1) * N_KEYS)
            for lt in range(h_ref.shape[1] // LANES):
                ls = slice(lt * LANES, (lt + 1) * LANES)
                w = None
                for h in range(PEER_HEADS):
                    hs = slice(h * N_KEYS, (h + 1) * N_KEYS)
                    row = slice(h * ic + ii, h * ic + ii + 1)
                    keep = a2_ref[hs, ls] >= thr_ref[row, ls]
                    term = jnp.where(keep, e1_ref[row, ls] * e2_ref[hs, ls], 0.0)
                    w = term if w is None else w + term
                wg_s[es, ls] = (w * act_s[es, ls]).astype(BF16)
    acc_s[...] += _dot(vt_ref[:, halves[0]], wg_s[halves[0], :]) + _dot(vt_ref[:, halves[1]], wg_s[halves[1], :])

    @pl.when(c == pl.num_programs(1) - 1)
    def _():
        out_ref[...] = acc_s[...].T


def _peer_dense(h_t, u_bf, vt_bf, thr, e1, a2, e2, tm):
    d, tp = h_t.shape
    ne = u_bf.shape[0]
    ec = EXPERT_ROWS_PER_STEP * N_KEYS
    hk = PEER_HEADS * N_KEYS
    once = pl.Buffered(1)
    tok = lambda rows: pl.BlockSpec((rows, tm), lambda t, c: (0, t), pipeline_mode=once)
    rows = pl.BlockSpec((None, PEER_HEADS * EXPERT_ROWS_PER_STEP, tm), lambda t, c: (c, 0, t))
    return pl.pallas_call(
        _peer_dense_kernel,
        grid=(tp // tm, ne // ec),
        in_specs=[tok(d), pl.BlockSpec((ec, d), lambda t, c: (c, 0)), pl.BlockSpec((d, ec), lambda t, c: (0, c)),
                  rows, rows, tok(hk), tok(hk)],
        out_specs=pl.BlockSpec((tm, d), lambda t, c: (t, 0), pipeline_mode=once),
        out_shape=jax.ShapeDtypeStruct((tp, d), F32),
        scratch_shapes=[pltpu.VMEM((ec, tm), F32), pltpu.VMEM((ec, tm), BF16), pltpu.VMEM((d, tm), F32)],
        compiler_params=_cparams("arbitrary", "arbitrary"),
        name="peer_dense",
    )(h_t, u_bf, vt_bf, thr, e1, a2, e2)


def _final_kernel(x1_ref, f_ref, ga2_ref, fg_ref, y_ref):
    x2 = x1_ref[...] + ga2_ref[...] * f_ref[...]
    y = x2 * lax.rsqrt(jnp.mean(x2 * x2, axis=-1, keepdims=True) + EPS)
    y_ref[...] = y * fg_ref[...]


def _final(x1, f, ga2, fg, per_row):
    r, d = x1.shape
    tm = r if per_row else 256
    row_spec = pl.BlockSpec((tm, d), lambda i: (i, 0))
    vec_spec = pl.BlockSpec((1, d), lambda i: (0, 0))
    return pl.pallas_call(
        _final_kernel,
        grid=(r // tm,),
        in_specs=[row_spec, row_spec, row_spec if per_row else vec_spec, vec_spec],
        out_specs=row_spec,
        out_shape=jax.ShapeDtypeStruct((r, d), F32),
        compiler_params=_cparams("arbitrary"),
        name="final_sample" if per_row else "final_prompt",
    )(x1, f, ga2, fg)


def _prep_w_in(w_in):
    d = w_in.shape[0]
    q0 = 3 * (d // 2)
    wq = w_in[:, q0:q0 + D_ATTN].reshape(d, N_KV, GROUP, HEAD_DIM).transpose(0, 2, 1, 3).reshape(d, D_ATTN)
    rest = w_in[:, q0 + D_ATTN:]
    pad = jnp.zeros((d, LANES - 3 * N_HEADS), w_in.dtype)
    return jnp.concatenate([w_in[:, :q0], wq, rest, pad], axis=1).astype(BF16)


def _prep_w_out(w_out):
    d = w_out.shape[0]
    dc = d // 2
    wa = w_out[dc:].reshape(N_KV, GROUP, HEAD_DIM, d).transpose(1, 0, 2, 3).reshape(D_ATTN, d)
    return jnp.concatenate([w_out[:dc], wa], axis=0).astype(BF16)


def kernel(x_prompt, x_sample, c_prompt, c_sample, cache_kv, state_win, state_conv, page_table, w_ada, b_ada,
           norm1_g, norm2_g, w_in, conv_w, conv_b, cmp_pe, cmp_w1, cmp_b1, cmp_w2, cmp_b2, w_out, peer_wq,
           peer_keys, peer_u, peer_v, final_g):
    depth = w_ada.shape[0]
    assert depth == 1
    bp, t, d = x_prompt.shape
    assert bp == 1 and d - d // 2 == D_ATTN
    bs, ts, _ = x_sample.shape
    assert ts == 4
    dc = d // 2
    n_pages = page_table.shape[1]
    past = n_pages * PAGE
    wkeep = state_win.shape[2]
    assert wkeep == WINDOW and t % KEY_TILE == 0 and n_pages % PAGES_PER_STEP == 0
    rs = bs * ts

    n_c = 1 + bs
    n_cp = -(-n_c // 8) * 8
    c_all = jnp.concatenate([c_prompt, c_sample, jnp.zeros((n_cp - n_c, d), F32)], axis=0)
    ada = _ada(c_all, w_ada[0], b_ada[0])
    sh1, sc1, ga1, sh2, sc2, ga2 = [ada[:, k * d:(k + 1) * d] for k in range(6)]
    pr = lambda a: a[0:1]
    sm = lambda a: jnp.repeat(a[1:1 + bs], ts, axis=0)

    w_perm = _prep_w_in(w_in[0])
    wo_perm = _prep_w_out(w_out[0])
    g1 = norm1_g[0].reshape(1, d)
    g2 = norm2_g[0].reshape(1, d)
    cw = conv_w[0]
    cb = conv_b[0].reshape(1, dc)
    cmpw = _prep_cmp_weights(cmp_pe[0], cmp_w1[0], cmp_b1[0], cmp_w2[0], cmp_b2[0])

    xp = x_prompt[0]
    cos_p, sin_p = _rope_tables(jnp.arange(t))
    convy_p, q_p, rows_p, win_p, kvb_p, gates_p, zc_p = _proj(xp, g1, pr(sc1), pr(sh1), w_perm, cos_p, sin_p, cw, cb)
    ident = jnp.arange(t // PAGE, dtype=jnp.int32).reshape(1, t // PAGE)
    kc_p, vc_p = _compress(rows_p.reshape(t // PAGE, PAGE, 4 * KV_W), ident, cmpw, False)
    attn_p = _nsa_prompt(q_p, gates_p, kc_p[0], vc_p[0], kvb_p)
    wq_bf = peer_wq[0].astype(BF16)
    keys_bf = peer_keys[0].reshape(2 * PEER_HEADS, N_KEYS, -1).astype(BF16)
    x1_p, h2_p, s_p = _outproj(xp, convy_p, attn_p, pr(ga1), g2, pr(sc2), pr(sh2), wo_perm, wq_bf, keys_bf, False)

    xs = x_sample.reshape(rs, d)
    pos_s = jnp.tile(past + jnp.arange(ts), bs)
    cos_s, sin_s = _rope_tables(pos_s)
    st = state_conv[0]
    a1 = jnp.repeat(st[:, 1], ts, axis=0)
    a2 = jnp.stack([st[:, 0], st[:, 1], st[:, 1], st[:, 1]], axis=1).reshape(rs, dc)
    convy_s, q_s, rows_s, win_s, kvb_s, gates_s, zc_s = _proj(
        xs, g1, sm(sc1), sm(sh1), w_perm, cos_s, sin_s, cw, cb, a1, a2)
    pool = cache_kv[0].transpose(0, 2, 3, 4, 1).reshape(cache_kv.shape[1], 4 * KV_W, PAGE)
    kc_s, vc_s = _compress(pool, page_table, cmpw, True)
    qrep = jnp.broadcast_to(q_s.reshape(bs, ts, GROUP, 1, KV_W).transpose(0, 2, 3, 1, 4),
                            (bs, GROUP, N_KV, ts, KV_W)).reshape(bs, GROUP * N_KV * ts, KV_W)
    gts = gates_s[:, :3 * N_HEADS].reshape(bs, ts, N_KV, GROUP, 3).transpose(0, 4, 3, 2, 1)
    gx = jnp.broadcast_to(gts.reshape(bs, 3, GROUP * N_KV * ts, 1), (bs, 3, GROUP * N_KV * ts, KV_W))
    new8 = jnp.pad(kvb_s.reshape(bs, ts, 4 * KV_W), ((0, 0), (0, 8 - ts), (0, 0)))
    sw = state_win[0].transpose(0, 2, 3, 4, 1).reshape(bs, 2 * KV_W, wkeep)
    o_s = _nsa_sample(pool, page_table, qrep, gx, kc_s, vc_s, sw, new8)
    o6 = o_s.reshape(bs, GROUP, N_KV, ts, N_KV, HEAD_DIM)
    gi = jnp.arange(N_KV)
    attn_s = o6[:, :, gi, :, gi, :]
    attn_s = attn_s.transpose(1, 3, 2, 0, 4).reshape(rs, D_ATTN).astype(BF16)
    x1_s, h2_s, s_s = _outproj(xs, convy_s, attn_s, sm(ga1), g2, sm(sc2), sm(sh2), wo_perm, wq_bf, keys_bf, True)

    ntok = t + rs
    tm = PEER_TOKEN_TILE
    tp = -(-ntok // tm) * tm
    padt = lambda a: jnp.pad(jnp.concatenate(a, axis=0), ((0, tp - ntok), (0, 0))).T
    h_t = padt([h2_p, h2_s])
    s_t = padt([s_p, s_s])
    a2p, e2p, thr, e1p = _peer_select(s_t)
    u_bf = peer_u[0].astype(BF16)
    vt_bf = peer_v[0].astype(BF16).T
    f = _peer_dense(h_t, u_bf, vt_bf, thr, e1p, a2p, e2p, tm)
    fg = final_g.reshape(1, d)
    y_p = _final(x1_p, f[:t], pr(ga2), fg, False)
    y_s = _final(x1_s, f[t:ntok], sm(ga2), fg, True)

    wmin = min(WINDOW, t)
    win_ctx = jnp.concatenate([state_win[0], win_s.reshape(bs, ts, 2, N_KV, HEAD_DIM)], axis=1)[:, ts:]
    zc3 = zc_s.reshape(bs, ts, dc)
    return (y_p.reshape(1, t, d), y_s.reshape(bs, ts, d),
            rows_p.reshape(1, 1, t, 4, N_KV, HEAD_DIM), rows_s.reshape(1, bs, ts, 4, N_KV, HEAD_DIM),
            win_p[t - wmin:].reshape(1, 1, wmin, 2, N_KV, HEAD_DIM), win_ctx[None],
            zc_p[8 - (3 - 1):].reshape(1, 1, 2, dc), zc3[:, ts - 2:][None])
```

```python
import functools

import numpy as np
import jax
import jax.numpy as jnp
from jax import lax
from jax.experimental import pallas as pl
from jax.experimental.pallas import tpu as pltpu

F32 = jnp.float32
BF16 = jnp.bfloat16

HEAD_DIM = 64
N_KV = 4
GROUP = 4
N_HEADS = N_KV * GROUP
KV_W = N_KV * HEAD_DIM
D_ATTN = N_HEADS * HEAD_DIM
CMP_STRIDE = 16
SLC_BLOCK = 64
TOP_N = 16
WINDOW = 512
Q_BLOCK = 128
PAGE = 128
FORCE_SCORE = 1.0e4
ROPE_THETA = 10000.0
N_KEYS = 128
PEER_HEADS = 8
PEER_TOPK = 16
EPS = 1e-6
LANES = 128
PAGES_PER_STEP = 16
KEY_TILE = 256
V7X_VMEM_LIMIT = 58 * 1024 * 1024
NEG_BIG = -1e30


def _cparams(*sem):
    return pltpu.CompilerParams(dimension_semantics=sem, vmem_limit_bytes=V7X_VMEM_LIMIT)


def _vmem():
    return pl.BlockSpec(memory_space=pltpu.VMEM)


def _dot(a, b):
    return jnp.dot(a, b, preferred_element_type=F32)


def _dot_nt(a, b):
    return lax.dot_general(a, b, (((1,), (1,)), ((), ())), preferred_element_type=F32)


def _split_dot(x, m):
    hi = x.astype(BF16)
    lo = (x - hi.astype(F32)).astype(BF16)
    return _dot(hi, m) + _dot(lo, m)


def _rope(x, cos, sinsg):
    w = x.shape[1]
    n = w // LANES
    c = jnp.concatenate([cos] * n, axis=1) if n > 1 else cos
    s = jnp.concatenate([sinsg] * n, axis=1) if n > 1 else sinsg
    lane = lax.broadcasted_iota(jnp.int32, x.shape, 1)
    first = (lane % HEAD_DIM) < (HEAD_DIM // 2)
    partner = jnp.where(first, pltpu.roll(x, w - HEAD_DIM // 2, 1), pltpu.roll(x, HEAD_DIM // 2, 1))
    return x * c + partner * s


def _rope_tables(pos):
    half = HEAD_DIM // 2
    inv = ROPE_THETA ** (-jnp.arange(half, dtype=F32) / half)
    ang = pos.astype(F32)[:, None] * inv[None, :]
    cos = jnp.cos(ang)
    sin = jnp.sin(ang)
    cos = jnp.concatenate([cos, cos, cos, cos], axis=1)
    sinsg = jnp.concatenate([-sin, sin, -sin, sin], axis=1)
    return cos, sinsg


def _rms_mod(xf, g, sc, sh):
    y = xf * lax.rsqrt(jnp.mean(xf * xf, axis=-1, keepdims=True) + EPS)
    return y * g * (1.0 + sc) + sh


def _ada_kernel(c_ref, w_ref, b_ref, o_ref):
    s = jax.nn.silu(c_ref[...]).astype(BF16)
    o_ref[...] = _dot(s, w_ref[...].astype(BF16)) + b_ref[...]


def _ada(c_all, w_ada, b_ada):
    r, d = c_all.shape
    n = w_ada.shape[1]
    tn = 1024
    return pl.pallas_call(
        _ada_kernel,
        grid=(n // tn,),
        in_specs=[pl.BlockSpec((r, d), lambda j: (0, 0)),
                  pl.BlockSpec((d, tn), lambda j: (0, j)),
                  pl.BlockSpec((1, tn), lambda j: (0, j))],
        out_specs=pl.BlockSpec((r, tn), lambda j: (0, j)),
        out_shape=jax.ShapeDtypeStruct((r, n), F32),
        compiler_params=_cparams("arbitrary"),
        name="ada",
    )(c_all, w_ada, b_ada.reshape(1, n))


def _proj_kernel(*refs, sample, tm, dc):
    if sample:
        (x_ref, g1_ref, sc_ref, sh_ref, w_ref, wg_ref, cos_ref, sin_ref, cw_ref, cb_ref, a1_ref, a2_ref,
         convy_ref, q_ref, rows_ref, win_ref, kvb_ref, gates_ref, zc_ref) = refs
    else:
        (x_ref, g1_ref, sc_ref, sh_ref, w_ref, wg_ref, cos_ref, sin_ref, cw_ref, cb_ref,
         convy_ref, q_ref, rows_ref, win_ref, kvb_ref, gates_ref, zc_ref, carry_ref) = refs
    h = _rms_mod(x_ref[...], g1_ref[...], sc_ref[...], sh_ref[...]).astype(BF16)

    def mm(a, b):
        return _dot(h, w_ref[:, a:b])

    cos = cos_ref[...]
    sin = sin_ref[...]
    q0 = 3 * dc
    kv0 = q0 + D_ATTN
    g0 = kv0 + 6 * KV_W

    b_g = mm(0, dc)
    zc = mm(dc, 2 * dc) * mm(2 * dc, 3 * dc)
    row = lax.broadcasted_iota(jnp.int32, zc.shape, 0)
    if sample:
        t = row % 4
        zm1 = jnp.where(t >= 1, pltpu.roll(zc, 1, 0), a1_ref[...])
        zm2 = jnp.where(t >= 2, pltpu.roll(zc, 2, 0), a2_ref[...])
        zc_ref[...] = zc
    else:
        @pl.when(pl.program_id(0) == 0)
        def _():
            carry_ref[...] = jnp.zeros_like(carry_ref)
        p1 = carry_ref[7:8, :]
        p2 = carry_ref[6:7, :]
        zm1 = jnp.where(row == 0, p1, pltpu.roll(zc, 1, 0))
        zm2 = jnp.where(row == 0, p2, jnp.where(row == 1, p1, pltpu.roll(zc, 2, 0)))
        carry_ref[...] = zc[tm - 8:tm, :]
        zc_ref[...] = zc[tm - 8:tm, :]
    conv = cb_ref[...] + cw_ref[0:1, :] * zm2
    conv = conv + cw_ref[1:2, :] * zm1
    conv = conv + cw_ref[2:3, :] * zc
    convy_ref[...] = (b_g * conv).astype(BF16)

    q = _rope(mm(q0, kv0), cos, sin) * (HEAD_DIM ** -0.5)
    q_ref[...] = q.astype(BF16)

    kv = mm(kv0, g0)
    k_cmp = kv[:, 0:KV_W]
    v_cmp = kv[:, KV_W:2 * KV_W]
    k_slc = _rope(kv[:, 2 * KV_W:3 * KV_W], cos, sin)
    v_slc = kv[:, 3 * KV_W:4 * KV_W]
    k_win = _rope(kv[:, 4 * KV_W:5 * KV_W], cos, sin)
    v_win = kv[:, 5 * KV_W:6 * KV_W]
    rows_ref[:, 0:KV_W] = k_cmp
    rows_ref[:, KV_W:2 * KV_W] = v_cmp
    rows_ref[:, 2 * KV_W:3 * KV_W] = k_slc
    rows_ref[:, 3 * KV_W:4 * KV_W] = v_slc
    win_ref[:, 0:KV_W] = k_win
    win_ref[:, KV_W:2 * KV_W] = v_win
    kvb_ref[:, 0:KV_W] = k_slc.astype(BF16)
    kvb_ref[:, KV_W:2 * KV_W] = v_slc.astype(BF16)
    kvb_ref[:, 2 * KV_W:3 * KV_W] = k_win.astype(BF16)
    kvb_ref[:, 3 * KV_W:4 * KV_W] = v_win.astype(BF16)
    gates_ref[...] = jax.nn.sigmoid(_dot(h, wg_ref[...]))


def _proj(x, g1, sc, sh, w_main, w_gate, cos, sin, conv_w, conv_b, a1=None, a2=None):
    sample = a1 is not None
    r, d = x.shape
    dc = d // 2
    tm = r if sample else 256
    row_spec = lambda wdt: pl.BlockSpec((tm, wdt), lambda i: (i, 0))
    full = lambda a: pl.BlockSpec(a.shape, lambda i: (0,) * a.ndim)
    mod_spec = row_spec(d) if sample else pl.BlockSpec((1, d), lambda i: (0, 0))
    in_specs = [row_spec(d), full(g1), mod_spec, mod_spec, _vmem(), _vmem(), row_spec(LANES), row_spec(LANES),
                full(conv_w), full(conv_b)]
    args = [x, g1, sc, sh, w_main, w_gate, cos, sin, conv_w, conv_b]
    scratch = []
    if sample:
        in_specs += [row_spec(dc), row_spec(dc)]
        args += [a1, a2]
        zc_shape = jax.ShapeDtypeStruct((r, dc), F32)
        zc_spec = row_spec(dc)
    else:
        zc_shape = jax.ShapeDtypeStruct((8, dc), F32)
        zc_spec = pl.BlockSpec((8, dc), lambda i: (0, 0))
        scratch = [pltpu.VMEM((8, dc), F32)]
    out_shape = (jax.ShapeDtypeStruct((r, dc), BF16), jax.ShapeDtypeStruct((r, D_ATTN), BF16),
                 jax.ShapeDtypeStruct((r, 4 * KV_W), F32), jax.ShapeDtypeStruct((r, 2 * KV_W), F32),
                 jax.ShapeDtypeStruct((r, 4 * KV_W), BF16), jax.ShapeDtypeStruct((r, LANES), F32), zc_shape)
    out_specs = (row_spec(dc), row_spec(D_ATTN), row_spec(4 * KV_W), row_spec(2 * KV_W),
                 row_spec(4 * KV_W), row_spec(LANES), zc_spec)
    return pl.pallas_call(
        functools.partial(_proj_kernel, sample=sample, tm=tm, dc=dc),
        grid=(r // tm,),
        in_specs=in_specs, out_specs=out_specs, out_shape=out_shape, scratch_shapes=scratch,
        compiler_params=_cparams("arbitrary"),
        name="proj_sample" if sample else "proj_prompt",
    )(*args)


def _cmp_kernel(pt_ref, *refs, transposed):
    npg = PAGES_PER_STEP
    pages = refs[:npg]
    (perm_ref, w1a_ref, w1b_ref, pea_ref, peb_ref, b1_ref, w2_ref, b2_ref, cos_ref, sin_ref,
     kc_ref, vc_ref, xa_s, xb_s, hprev_s) = refs[npg:]
    del pt_ref
    nch = PAGE // CMP_STRIDE
    rows = npg * nch

    @pl.when(pl.program_id(1) == 0)
    def _():
        hprev_s[...] = jnp.zeros_like(hprev_s)

    lane = lax.broadcasted_iota(jnp.int32, (nch, 2 * KV_W), 1)
    low = (lane % LANES) < HEAD_DIM
    perm = perm_ref[...]
    for k in range(npg):
        page = pages[k][...]
        for pe_ref, x_s in ((pea_ref, xa_s), (peb_ref, xb_s)):
            src = (page + pe_ref[...]).astype(BF16)
            y = _dot_nt(perm, src) if transposed else _dot(perm, src)
            for p in range(CMP_STRIDE // 2):
                ev = y[2 * p * nch:(2 * p + 1) * nch]
                od = y[(2 * p + 1) * nch:(2 * p + 2) * nch]
                a = jnp.where(low, ev, pltpu.roll(od, HEAD_DIM, 1))
                b = jnp.where(low, pltpu.roll(ev, 2 * KV_W - HEAD_DIM, 1), od)
                for j in range(4):
                    kv, ge = j // 2, 2 * (j % 2)
                    rsl = slice(nch * k, nch * (k + 1))
                    csl = slice(LANES * p, LANES * (p + 1))
                    x_s[kv, ge, rsl, csl] = a[:, LANES * j:LANES * (j + 1)]
                    x_s[kv, ge + 1, rsl, csl] = b[:, LANES * j:LANES * (j + 1)]

    row = lax.broadcasted_iota(jnp.int32, (N_KV * rows, 1), 0)
    for kv in range(2):
        ha = _dot(xa_s[kv].reshape(N_KV * rows, CMP_STRIDE * HEAD_DIM).astype(BF16), w1a_ref[kv])
        hb = _dot(xb_s[kv].reshape(N_KV * rows, CMP_STRIDE * HEAD_DIM).astype(BF16), w1b_ref[kv])
        hp = hprev_s[kv]
        prev = jnp.where(row % rows == 0, pltpu.roll(hp, N_KV * rows - (rows - 1), 0), pltpu.roll(ha, 1, 0))
        hprev_s[kv] = ha
        hid = jax.nn.gelu(prev + hb + b1_ref[kv])
        hcat = jnp.concatenate([hid[rows * g:rows * (g + 1)] for g in range(N_KV)], axis=1).astype(BF16)
        o = _dot(hcat, w2_ref[kv]) + b2_ref[kv]
        if kv == 0:
            kc_ref[...] = _rope(o, cos_ref[...], sin_ref[...]).astype(BF16)
        else:
            vc_ref[...] = o.astype(BF16)


def _compress(pool, page_table, cw, transposed):
    b, n_pages = page_table.shape
    npg = PAGES_PER_STEP
    nsteps = n_pages // npg
    rows = npg * (PAGE // CMP_STRIDE)
    nc = nsteps * rows
    cpos = CMP_STRIDE * jnp.arange(nc) + (CMP_STRIDE - 1)
    cos, sin = _rope_tables(cpos)
    page_block = (None, 2 * KV_W, PAGE) if transposed else (None, PAGE, 2 * KV_W)

    def page_spec(k):
        return pl.BlockSpec(page_block, lambda bi, i, pt: (pt[bi, i * npg + k], 0, 0))

    const = lambda a: pl.BlockSpec(a.shape, lambda bi, i, pt: (0,) * a.ndim)
    nch = PAGE // CMP_STRIDE
    perm = np.zeros((PAGE, PAGE), np.float32)
    for s in range(CMP_STRIDE):
        for c in range(nch):
            perm[s * nch + c, CMP_STRIDE * c + s] = 1.0
    pea, peb = (cw["pea"].T, cw["peb"].T) if transposed else (cw["pea"], cw["peb"])
    weights = [jnp.asarray(perm, BF16), cw["w1a"], cw["w1b"], pea, peb, cw["b1"], cw["w2bd"], cw["b2"]]
    tab_spec = pl.BlockSpec((rows, LANES), lambda bi, i, pt: (i, 0))
    out_spec = pl.BlockSpec((None, rows, KV_W), lambda bi, i, pt: (bi, i, 0))
    x_scratch = pltpu.VMEM((2, N_KV, rows, CMP_STRIDE * HEAD_DIM), F32)
    grid_spec = pltpu.PrefetchScalarGridSpec(
        num_scalar_prefetch=1,
        grid=(b, nsteps),
        in_specs=[page_spec(k) for k in range(npg)] + [const(a) for a in weights] + [tab_spec, tab_spec],
        out_specs=(out_spec, out_spec),
        scratch_shapes=[x_scratch, x_scratch, pltpu.VMEM((2, N_KV * rows, 2 * HEAD_DIM), F32)],
    )
    return pl.pallas_call(
        functools.partial(_cmp_kernel, transposed=transposed),
        grid_spec=grid_spec,
        out_shape=(jax.ShapeDtypeStruct((b, nc, KV_W), BF16), jax.ShapeDtypeStruct((b, nc, KV_W), BF16)),
        compiler_params=_cparams("arbitrary", "arbitrary"),
        name="compress",
    )(page_table, *([pool] * npg), *weights, cos, sin)


def _prep_cmp_weights(cmp_pe, cmp_w1, cmp_b1, cmp_w2, cmp_b2):
    s = CMP_STRIDE
    w2bd = jnp.zeros((2, N_KV * 2 * HEAD_DIM, KV_W), F32)
    for g in range(N_KV):
        w2bd = w2bd.at[:, g * 2 * HEAD_DIM:(g + 1) * 2 * HEAD_DIM, g * HEAD_DIM:(g + 1) * HEAD_DIM].set(cmp_w2)
    def pe_page(pe_half):
        t = jnp.broadcast_to(pe_half.transpose(1, 0, 2)[:, :, None, :], (s, 2, N_KV, HEAD_DIM))
        return jnp.tile(t.reshape(s, 2 * KV_W), (PAGE // s, 1))

    return {
        "w1a": cmp_w1[:, :s].reshape(2, s * HEAD_DIM, 2 * HEAD_DIM).astype(BF16),
        "w1b": cmp_w1[:, s:].reshape(2, s * HEAD_DIM, 2 * HEAD_DIM).astype(BF16),
        "pea": pe_page(cmp_pe[:, :s]),
        "peb": pe_page(cmp_pe[:, s:]),
        "b1": cmp_b1.reshape(2, 1, 2 * HEAD_DIM),
        "w2bd": w2bd.astype(BF16),
        "b2": jnp.tile(cmp_b2, (1, N_KV)).reshape(2, 1, KV_W),
    }


def _cmp_to_slc(nc, ns):
    j = np.arange(nc)[:, None] - 1
    b = np.arange(ns)[None, :]
    ov = (CMP_STRIDE * j < SLC_BLOCK * (b + 1)) & (CMP_STRIDE * j + 2 * CMP_STRIDE > SLC_BLOCK * b) & (j >= 0)
    return ov.astype(np.float32)


def _masked_softmax(s, mask):
    s = jnp.where(mask, s, -jnp.inf)
    m = jnp.max(s, axis=-1, keepdims=True)
    m = jnp.where(m > -jnp.inf, m, 0.0)
    p = jnp.exp(s - m)
    return p * (1.0 / jnp.maximum(jnp.sum(p, axis=-1, keepdims=True), 1e-30))


def _select_blocks(score, n_top):
    lane = lax.broadcasted_iota(jnp.int32, score.shape, 1)
    nb = score.shape[1]
    sel = jnp.zeros(score.shape, F32)
    cur = score
    for _ in range(n_top):
        mx = jnp.max(cur, axis=-1, keepdims=True)
        first = jnp.min(jnp.where(cur == mx, lane, nb), axis=-1, keepdims=True)
        hit = (lane == first) & (mx > -jnp.inf)
        sel = jnp.where(hit, 1.0, sel)
        cur = jnp.where(lane == first, -jnp.inf, cur)
    return sel


def _masked_softmax0(s, mask):
    s = jnp.where(mask, s, -jnp.inf)
    m = jnp.max(s, axis=0, keepdims=True)
    m = jnp.where(m > -jnp.inf, m, 0.0)
    p = jnp.exp(s - m)
    return p * (1.0 / jnp.maximum(jnp.sum(p, axis=0, keepdims=True), 1e-30))


def _select_blocks_t(score, n_top):
    rowi = lax.broadcasted_iota(jnp.int32, score.shape, 0)
    nb = score.shape[0]
    sel = jnp.zeros(score.shape, F32)
    cur = score
    for _ in range(n_top):
        mx = jnp.max(cur, axis=0, keepdims=True)
        first = jnp.min(jnp.where(cur == mx, rowi, nb), axis=0, keepdims=True)
        is_first = rowi == first
        sel = jnp.where(is_first & (mx > -jnp.inf), 1.0, sel)
        cur = jnp.where(is_first, -jnp.inf, cur)
    return sel


def _own_blocks(x, tq):
    rq = GROUP * tq
    return jnp.concatenate([x[g * HEAD_DIM:(g + 1) * HEAD_DIM, g * rq:(g + 1) * rq] for g in range(N_KV)], axis=0)


def _nsa_prompt_kernel(qt_ref, gt_ref, kc_ref, vct_ref, ks_ref, vst_ref, kw_ref, vwt_ref, mt_ref,
                       out_ref, qm_s, bias_s, acc_s, m_s, l_s, sa_s, sb_s, *, nc, ns):
    i = pl.program_id(0)
    tq = Q_BLOCK
    rq = GROUP * tq
    ncol = N_KV * rq
    colq = i * tq + lax.broadcasted_iota(jnp.int32, (1, rq), 1) % tq
    colq_g = i * tq + lax.broadcasted_iota(jnp.int32, (1, N_KV * tq), 1) % tq
    grows = [slice(g * HEAD_DIM, (g + 1) * HEAD_DIM) for g in range(N_KV)]
    gcols = [slice(g * rq, (g + 1) * rq) for g in range(N_KV)]

    @pl.when(i == 0)
    def _():
        qm_s[...] = jnp.zeros_like(qm_s)

    for g in range(N_KV):
        for r in range(GROUP):
            hd = (g * GROUP + r) * HEAD_DIM
            qm_s[grows[g], (g * GROUP + r) * tq:(g * GROUP + r + 1) * tq] = qt_ref[hd:hd + HEAD_DIM, :]

    crow = lax.broadcasted_iota(jnp.int32, (nc, 1), 0)
    cvalid = (crow >= 1) & (CMP_STRIDE * crow + (CMP_STRIDE - 1) <= colq)
    kc = kc_ref[...]
    o_c, psum = [], []
    scores = [_dot(kc, qm_s[:, gcols[g]]) for g in range(N_KV)]
    for g in range(N_KV):
        p = _masked_softmax0(scores[g], cvalid)
        o_c.append(_dot(vct_ref[grows[g], :], p.astype(BF16)))
        psum.append(sum(p[:, r * tq:(r + 1) * tq] for r in range(GROUP)))
    o_c = jnp.concatenate(o_c, axis=0)
    psum = jnp.concatenate(psum, axis=1)
    hi = psum.astype(BF16)
    lo = (psum - hi.astype(F32)).astype(BF16)
    imp = _dot(mt_ref[...], hi) + _dot(mt_ref[...], lo)
    blk = lax.broadcasted_iota(jnp.int32, (ns, 1), 0)
    cur = colq_g // SLC_BLOCK
    forced = (blk == 0) | (blk == cur) | (blk == cur - 1)
    score = jnp.where(blk <= cur, jnp.where(forced, FORCE_SCORE, imp), -jnp.inf)
    bias = (_select_blocks_t(score, min(TOP_N, ns)) - 1.0) * (-NEG_BIG)
    for g in range(N_KV):
        for r in range(GROUP):
            c0 = (g * GROUP + r) * tq
            bias_s[:, c0:c0 + tq] = bias[:, g * tq:(g + 1) * tq]

    w0 = pl.multiple_of(i * tq, tq)
    kw = kw_ref[pl.ds(w0, WINDOW + tq), :]
    nwt = (WINDOW + tq) // tq
    vwt = vwt_ref[pl.ds(i, nwt)]
    vwt = jnp.concatenate([vwt[j] for j in range(nwt)], axis=1)
    wpos = i * tq - WINDOW + lax.broadcasted_iota(jnp.int32, (WINDOW + tq, 1), 0)
    dist = colq - wpos
    wmask = (dist >= 0) & (dist < WINDOW) & (wpos >= 0)
    o_w = []
    scores = [_dot(kw, qm_s[:, gcols[g]]) for g in range(N_KV)]
    for g in range(N_KV):
        p = _masked_softmax0(scores[g], wmask)
        o_w.append(_dot(vwt[grows[g], :], p.astype(BF16)))
    o_w = jnp.concatenate(o_w, axis=0)

    acc_s[...] = jnp.zeros_like(acc_s)
    m_s[...] = jnp.full(m_s.shape, NEG_BIG, F32)
    l_s[...] = jnp.zeros_like(l_s)
    blocks_per_tile = KEY_TILE // SLC_BLOCK
    krow = lax.broadcasted_iota(jnp.int32, (KEY_TILE, 1), 0)

    def scores(kt, buf):
        k0 = pl.multiple_of(kt * KEY_TILE, KEY_TILE)
        kk = ks_ref[pl.ds(k0, KEY_TILE), :]
        for g in range(N_KV):
            buf[g] = _dot(kk, qm_s[:, gcols[g]])

    def tile(kt, buf, causal):
        k0 = pl.multiple_of(kt * KEY_TILE, KEY_TILE)
        vt = vst_ref[kt]
        probs, alphas = [], []
        for g in range(N_KV):
            b = jnp.concatenate(
                [jnp.broadcast_to(bias_s[pl.ds(kt * blocks_per_tile + j, 1), gcols[g]], (SLC_BLOCK, rq))
                 for j in range(blocks_per_tile)], axis=0)
            s = buf[g] + b
            if causal:
                s = jnp.where(k0 + krow <= colq, s, NEG_BIG)
            m_old = m_s[:, gcols[g]]
            m_new = jnp.maximum(m_old, jnp.max(s, axis=0, keepdims=True))
            alpha = jnp.exp(m_old - m_new)
            p = jnp.exp(s - m_new)
            l_s[:, gcols[g]] = alpha * l_s[:, gcols[g]] + jnp.sum(p, axis=0, keepdims=True)
            m_s[:, gcols[g]] = m_new
            probs.append(p.astype(BF16))
            alphas.append(alpha)
        for g in range(N_KV):
            acc_s[grows[g], :] = acc_s[grows[g], :] * alphas[g] + _dot(vt[grows[g], :], probs[g])

    last = (i * tq) // KEY_TILE

    scores(0, sa_s)

    def body(j, carry):
        scores(2 * j + 1, sb_s)
        tile(2 * j, sa_s, False)
        scores(2 * j + 2, sa_s)
        tile(2 * j + 1, sb_s, False)
        return carry

    lax.fori_loop(0, last // 2, body, 0)

    @pl.when(last % 2 == 0)
    def _():
        tile(last, sa_s, True)

    @pl.when(last % 2 == 1)
    def _():
        scores(last, sb_s)
        tile(last - 1, sa_s, False)
        tile(last, sb_s, True)

    rl = 1.0 / l_s[...]
    o_s = acc_s[...] * jnp.concatenate(
        [jnp.broadcast_to(rl[:, g * rq:(g + 1) * rq], (HEAD_DIM, rq)) for g in range(N_KV)], axis=0)

    gt = gt_ref[...]

    def gate(br):
        return jnp.concatenate(
            [jnp.concatenate(
                [jnp.broadcast_to(gt[(GROUP * g + r) * 3 + br:(GROUP * g + r) * 3 + br + 1, :], (HEAD_DIM, tq))
                 for r in range(GROUP)], axis=1) for g in range(N_KV)], axis=0)

    o = gate(0) * o_c + gate(1) * o_s + gate(2) * o_w
    for r in range(GROUP):
        out_ref[:, r * KV_W:(r + 1) * KV_W] = o[:, r * tq:(r + 1) * tq].T.astype(BF16)


def _nsa_prompt(q, gates, kc, vc, kvb):
    t = q.shape[0]
    nc = kc.shape[0]
    ns = t // SLC_BLOCK
    tq = Q_BLOCK
    qt = q.T
    gt = gates.T
    vct = vc.T
    ks = kvb[:, 0:KV_W]
    vst = kvb[:, KV_W:2 * KV_W].reshape(t // KEY_TILE, KEY_TILE, KV_W).transpose(0, 2, 1)
    kw = jnp.pad(kvb[:, 2 * KV_W:3 * KV_W], ((WINDOW, 0), (0, 0)))
    vwt = jnp.pad(kvb[:, 3 * KV_W:4 * KV_W], ((WINDOW, 0), (0, 0)))
    vwt = vwt.reshape((t + WINDOW) // tq, tq, KV_W).transpose(0, 2, 1)
    mt = jnp.asarray(_cmp_to_slc(nc, ns).T, BF16)
    full = lambda a: pl.BlockSpec(a.shape, lambda i: (0,) * a.ndim)
    ncol = N_KV * GROUP * tq
    return pl.pallas_call(
        functools.partial(_nsa_prompt_kernel, nc=nc, ns=ns),
        grid=(t // tq,),
        in_specs=[pl.BlockSpec((D_ATTN, tq), lambda i: (0, i)),
                  pl.BlockSpec((LANES, tq), lambda i: (0, i)),
                  full(kc), full(vct), _vmem(), _vmem(), _vmem(), _vmem(), full(mt)],
        out_specs=pl.BlockSpec((tq, D_ATTN), lambda i: (i, 0)),
        out_shape=jax.ShapeDtypeStruct((t, D_ATTN), BF16),
        scratch_shapes=[pltpu.VMEM((KV_W, ncol), BF16),
                        pltpu.VMEM((ns, ncol), F32),
                        pltpu.VMEM((KV_W, GROUP * tq), F32),
                        pltpu.VMEM((1, ncol), F32),
                        pltpu.VMEM((1, ncol), F32),
                        pltpu.VMEM((N_KV, KEY_TILE, GROUP * tq), F32),
                        pltpu.VMEM((N_KV, KEY_TILE, GROUP * tq), F32)],
        compiler_params=_cparams("arbitrary"),
        name="nsa_prompt",
    )(qt, gt, kc, vct, ks, vst, kw, vwt, mt)


def _nsa_sample_kernel(pt_ref, *refs, nc, ns, nsteps):
    npg = PAGES_PER_STEP
    pages = refs[:npg]
    (qrep_ref, gx_ref, kc_ref, vc_ref, sw_ref, new_ref, m_ref, e_ref, en_ref,
     out_ref, qall_s, sel_s, oc_s, ow_s, m_s, l_s, acc_s) = refs[npg:]
    del pt_ref
    i = pl.program_id(1)
    nr = GROUP * N_KV * 4
    row = lax.broadcasted_iota(jnp.int32, (nr, 1), 0)
    row_t = row % 4
    row_g = (row // 4) % N_KV
    lane_g = lax.broadcasted_iota(jnp.int32, (1, KV_W), 1) // HEAD_DIM
    own = row_g == lane_g
    new = new_ref[...]
    ucol = lax.broadcasted_iota(jnp.int32, (1, 8), 1)
    new_ok = (ucol <= row_t) & (ucol < 4)

    @pl.when(i == 0)
    def _():
        qall = jnp.where(own, qrep_ref[...], 0).astype(BF16)
        qall_s[...] = qall
        cidx = lax.broadcasted_iota(jnp.int32, (1, nc), 1)
        p = _masked_softmax(_dot_nt(qall, kc_ref[...]), cidx >= 1)
        oc_s[...] = _dot(p.astype(BF16), vc_ref[...])
        n16 = N_KV * 4
        psum = p[0:n16] + p[n16:2 * n16] + p[2 * n16:3 * n16] + p[3 * n16:4 * n16]
        imp = _split_dot(psum, m_ref[...])
        nb = imp.shape[1]
        blk = lax.broadcasted_iota(jnp.int32, (1, nb), 1)
        cur = ns - 1
        forced = (blk == 0) | (blk == cur) | (blk == cur - 1)
        score = jnp.where(blk <= cur, jnp.where(forced, FORCE_SCORE, imp), -jnp.inf)
        sel_s[...] = _select_blocks(score, min(TOP_N, ns)).astype(BF16)
        kw_t = sw_ref[0:KV_W, :].astype(BF16)
        vw_t = sw_ref[KV_W:2 * KV_W, :].astype(BF16)
        wcol = lax.broadcasted_iota(jnp.int32, (1, kw_t.shape[1]), 1)
        s1 = jnp.where(wcol > row_t, _dot(qall, kw_t), -jnp.inf)
        s2 = jnp.where(new_ok, _dot_nt(qall, new[:, 2 * KV_W:3 * KV_W]), -jnp.inf)
        mw = jnp.maximum(jnp.max(s1, axis=-1, keepdims=True), jnp.max(s2, axis=-1, keepdims=True))
        p1 = jnp.exp(s1 - mw)
        p2 = jnp.exp(s2 - mw)
        den = jnp.maximum(jnp.sum(p1, axis=-1, keepdims=True) + jnp.sum(p2, axis=-1, keepdims=True), 1e-30)
        ow_s[...] = (_dot_nt((p1 / den).astype(BF16), vw_t)
                     + _dot((p2 / den).astype(BF16), new[:, 3 * KV_W:4 * KV_W]))
        m_s[...] = jnp.full(m_s.shape, NEG_BIG, F32)
        l_s[...] = jnp.zeros_like(l_s)
        acc_s[...] = jnp.zeros_like(acc_s)

    qall = qall_s[...]
    kk_t = jnp.concatenate([pages[k][0:KV_W, :] for k in range(npg)], axis=1).astype(BF16)
    vv_t = jnp.concatenate([pages[k][KV_W:2 * KV_W, :] for k in range(npg)], axis=1).astype(BF16)
    mk = _dot(sel_s[...], e_ref[i]) > 0.5
    mk = jnp.concatenate([mk] * GROUP, axis=0)
    s = jnp.where(mk, _dot(qall, kk_t), NEG_BIG)
    m_old = m_s[...]
    m_new = jnp.maximum(m_old, jnp.max(s, axis=-1, keepdims=True))
    alpha = jnp.exp(m_old - m_new)
    p = jnp.exp(s - m_new)
    l_new = alpha * l_s[...] + jnp.sum(p, axis=-1, keepdims=True)
    acc_new = alpha * acc_s[...] + _dot_nt(p.astype(BF16), vv_t)
    m_s[...] = m_new
    l_s[...] = l_new
    acc_s[...] = acc_new

    @pl.when(i == nsteps - 1)
    def _():
        mkn = _dot(sel_s[...], en_ref[...]) > 0.5
        mkn = jnp.concatenate([mkn] * GROUP, axis=0) & new_ok
        sn = jnp.where(mkn, _dot_nt(qall, new[:, 0:KV_W]), NEG_BIG)
        m_f = jnp.maximum(m_new, jnp.max(sn, axis=-1, keepdims=True))
        al = jnp.exp(m_new - m_f)
        pn = jnp.exp(sn - m_f)
        l_f = al * l_new + jnp.sum(pn, axis=-1, keepdims=True)
        acc_f = al * acc_new + _dot(pn.astype(BF16), new[:, KV_W:2 * KV_W])
        o_s = acc_f / l_f
        out_ref[...] = gx_ref[0] * oc_s[...] + gx_ref[1] * o_s + gx_ref[2] * ow_s[...]


def _nsa_sample(pool, page_table, qrep, gx, kc, vc, sw, new8):
    b, n_pages = page_table.shape
    npg = PAGES_PER_STEP
    nsteps = n_pages // npg
    nc = kc.shape[1]
    past = n_pages * PAGE
    ns = past // SLC_BLOCK + 1
    nb = -(-ns // LANES) * LANES
    keys_step = npg * PAGE
    m = np.zeros((nc, nb), np.float32)
    m[:, :ns] = _cmp_to_slc(nc, ns)
    e = np.zeros((nsteps, nb, keys_step), np.float32)
    for st in range(nsteps):
        for j in range(keys_step):
            e[st, (st * keys_step + j) // SLC_BLOCK, j] = 1.0
    en = np.zeros((nb, 8), np.float32)
    en[ns - 1, :] = 1.0
    m, e, en = jnp.asarray(m, BF16), jnp.asarray(e, BF16), jnp.asarray(en, BF16)
    nr = GROUP * N_KV * 4

    def page_spec(k):
        return pl.BlockSpec((None, 2 * KV_W, PAGE), lambda bi, i, pt: (pt[bi, i * npg + k], 1, 0))

    per_b = lambda a: pl.BlockSpec((None,) + a.shape[1:], lambda bi, i, pt: (bi,) + (0,) * (a.ndim - 1))
    const = lambda a: pl.BlockSpec(a.shape, lambda bi, i, pt: (0,) * a.ndim)
    grid_spec = pltpu.PrefetchScalarGridSpec(
        num_scalar_prefetch=1,
        grid=(b, nsteps),
        in_specs=[page_spec(k) for k in range(npg)]
        + [per_b(qrep), per_b(gx), per_b(kc), per_b(vc), per_b(sw), per_b(new8), const(m), _vmem(), const(en)],
        out_specs=pl.BlockSpec((None, nr, KV_W), lambda bi, i, pt: (bi, 0, 0)),
        scratch_shapes=[pltpu.VMEM((nr, KV_W), BF16), pltpu.VMEM((N_KV * 4, nb), BF16),
                        pltpu.VMEM((nr, KV_W), F32), pltpu.VMEM((nr, KV_W), F32),
                        pltpu.VMEM((nr, 1), F32), pltpu.VMEM((nr, 1), F32), pltpu.VMEM((nr, KV_W), F32)],
    )
    return pl.pallas_call(
        functools.partial(_nsa_sample_kernel, nc=nc, ns=ns, nsteps=nsteps),
        grid_spec=grid_spec,
        out_shape=jax.ShapeDtypeStruct((b, nr, KV_W), F32),
        compiler_params=_cparams("arbitrary", "arbitrary"),
        name="nsa_sample",
    )(page_table, *([pool] * npg), qrep, gx, kc, vc, sw, new8, m, e, en)


def _outproj_kernel(x_ref, cy_ref, at_ref, ga1_ref, g2_ref, sc2_ref, sh2_ref, wo_ref, wq_ref, keys_ref,
                    x1_ref, h2_ref, s_ref, *, dc):
    u = _dot(cy_ref[...], wo_ref[0:dc, :]) + _dot(at_ref[...], wo_ref[dc:, :])
    x1 = x_ref[...] + ga1_ref[...] * u
    x1_ref[...] = x1
    h2 = _rms_mod(x1, g2_ref[...], sc2_ref[...], sh2_ref[...]).astype(BF16)
    h2_ref[...] = h2
    qp = _dot(h2, wq_ref[...]).astype(BF16)
    for hc in range(2 * PEER_HEADS):
        s_ref[:, hc * N_KEYS:(hc + 1) * N_KEYS] = _dot_nt(qp[:, hc * LANES:(hc + 1) * LANES], keys_ref[hc])


def _outproj(x, convy, attn, ga1, g2, sc2, sh2, wo, wq, keys, per_row):
    r, d = x.shape
    dc = d // 2
    tm = r if per_row else 256
    row_spec = lambda wdt: pl.BlockSpec((tm, wdt), lambda i: (i, 0))
    mod_spec = row_spec(d) if per_row else pl.BlockSpec((1, d), lambda i: (0, 0))
    nsc = keys.shape[0] * N_KEYS
    return pl.pallas_call(
        functools.partial(_outproj_kernel, dc=dc),
        grid=(r // tm,),
        in_specs=[row_spec(d), row_spec(dc), row_spec(d - dc), mod_spec,
                  pl.BlockSpec((1, d), lambda i: (0, 0)), mod_spec, mod_spec, _vmem(), _vmem(), _vmem()],
        out_specs=(row_spec(d), row_spec(d), row_spec(nsc)),
        out_shape=(jax.ShapeDtypeStruct((r, d), F32), jax.ShapeDtypeStruct((r, d), BF16),
                   jax.ShapeDtypeStruct((r, nsc), F32)),
        compiler_params=_cparams("arbitrary"),
        name="outproj_sample" if per_row else "outproj_prompt",
    )(x, convy, attn, ga1, g2, sc2, sh2, wo, wq, keys)


def _sort_network(n):
    pairs = []

    def merge(lo, cnt, r):
        step = r * 2
        if step < cnt:
            merge(lo, cnt, step)
            merge(lo + r, cnt, step)
            for i in range(lo + r, lo + cnt - r, step):
                pairs.append((i, i + r))
        else:
            pairs.append((lo, lo + r))

    def sort(lo, cnt):
        if cnt > 1:
            half = cnt // 2
            sort(lo, half)
            sort(lo + half, half)
            merge(lo, cnt, 1)

    sort(0, n)
    return pairs


def _prune_network(pairs, wanted):
    need = set(wanted)
    keep = []
    for i, j in reversed(pairs):
        if i in need or j in need:
            keep.append((i, j))
            need.update((i, j))
    return keep[::-1]


def _apply_network(x, pairs):
    x = list(x)
    for i, j in pairs:
        x[i], x[j] = jnp.maximum(x[i], x[j]), jnp.minimum(x[i], x[j])
    return x


_SORT16 = _sort_network(PEER_TOPK)
_CAND_PAIRS = [(a, b) for a in range(PEER_TOPK + 1) for b in range(PEER_TOPK + 1)
               if (a + 1) * (b + 1) <= PEER_TOPK + 1]
_CAND_WIRES = 64
_SELECT_16_17 = _prune_network(_sort_network(_CAND_WIRES), (PEER_TOPK - 1, PEER_TOPK))


def _top17(s):
    k = PEER_TOPK
    sub = 8
    lst = _apply_network([s[sub * v:sub * (v + 1)] for v in range(k)], _SORT16)
    for dist in (1, 2, 4):
        c = [jnp.maximum(lst[v], pltpu.roll(lst[k - 1 - v], dist, 0)) for v in range(k)]
        d = k // 2
        while d >= 1:
            for i in range(k):
                if i & d == 0:
                    c[i], c[i + d] = jnp.maximum(c[i], c[i + d]), jnp.minimum(c[i], c[i + d])
            d //= 2
        lst = c
    top = [v[sub - 1:sub, :] for v in lst]
    v16 = top[k - 1]
    ge = s >= v16
    cnt = jnp.sum(jnp.where(ge, 1.0, 0.0), axis=0, keepdims=True)
    below = jnp.max(jnp.where(ge, -jnp.inf, s), axis=0, keepdims=True)
    top.append(jnp.where(cnt > k, v16, below))
    return top


def _peer_select_kernel(s_ref, a2_ref, e2_ref, thr_ref, e1_ref):
    k = PEER_TOPK
    nh = PEER_HEADS
    half = lambda h, c: s_ref[(2 * h + c) * N_KEYS:(2 * h + c + 1) * N_KEYS, :]
    tops1 = [_top17(half(h, 0)) for h in range(nh)]
    tops2 = [_top17(half(h, 1)) for h in range(nh)]
    v1 = [jnp.concatenate([tops1[h][a] for h in range(nh)], axis=0) for a in range(k + 1)]
    v2 = [jnp.concatenate([tops2[h][a] for h in range(nh)], axis=0) for a in range(k + 1)]
    d1 = [v - v1[0] for v in v1]
    d2 = [v - v2[0] for v in v2]
    cands = [d1[a] + d2[b] for a, b in _CAND_PAIRS]
    pad = [jnp.full_like(cands[0], -jnp.inf)] * (_CAND_WIRES - len(cands))
    srt = _apply_network(cands + pad, _SELECT_16_17)
    tau = 0.5 * (srt[k - 1] + srt[k])
    x1 = [jnp.exp(d) for d in d1]
    x2 = [jnp.exp(d) for d in d2]
    z = jnp.zeros_like(tau)
    for (a, b), c in zip(_CAND_PAIRS, cands):
        z = z + jnp.where(c >= tau, x1[a] * x2[b], 0.0)
    rz = 1.0 / z
    ic = EXPERT_ROWS_PER_STEP
    for h in range(nh):
        a1 = half(h, 0) - v1[0][h:h + 1]
        a2 = half(h, 1) - v2[0][h:h + 1]
        thr = tau[h:h + 1] - a1
        e1 = jnp.exp(a1) * rz[h:h + 1]
        for c in range(N_KEYS // ic):
            thr_ref[c, h * ic:(h + 1) * ic, :] = thr[c * ic:(c + 1) * ic]
            e1_ref[c, h * ic:(h + 1) * ic, :] = e1[c * ic:(c + 1) * ic]
        a2_ref[h * N_KEYS:(h + 1) * N_KEYS, :] = a2
        e2_ref[h * N_KEYS:(h + 1) * N_KEYS, :] = jnp.exp(a2)


def _peer_select(s_t):
    nrow, tp = s_t.shape
    assert N_KEYS == 8 * PEER_TOPK
    tn = LANES
    hk = PEER_HEADS * N_KEYS
    ic = EXPERT_ROWS_PER_STEP
    flat = pl.BlockSpec((None, hk, tn), lambda t: (t, 0, 0))
    cube = pl.BlockSpec((None, N_KEYS // ic, PEER_HEADS * ic, tn), lambda t: (t, 0, 0, 0))
    return pl.pallas_call(
        _peer_select_kernel,
        grid=(tp // tn,),
        in_specs=[pl.BlockSpec((nrow, tn), lambda t: (0, t))],
        out_specs=(flat, flat, cube, cube),
        out_shape=(jax.ShapeDtypeStruct((tp // tn, hk, tn), F32),) * 2
        + (jax.ShapeDtypeStruct((tp // tn, N_KEYS // ic, PEER_HEADS * ic, tn), F32),) * 2,
        compiler_params=_cparams("arbitrary"),
        name="peer_select",
    )(s_t)


EXPERT_ROWS_PER_STEP = 8
PEER_TOKEN_TILE = 768


def _peer_dense_kernel(h_ref, u_ref, vt_ref, thr_ref, e1_ref, a2_ref, e2_ref, out_ref, gate_s, wg_s, acc_s):
    c = pl.program_id(1)

    @pl.when(c == 0)
    def _():
        acc_s[...] = jnp.zeros_like(acc_s)

    ic = EXPERT_ROWS_PER_STEP
    for ii in range(ic):
        es = slice(ii * N_KEYS, (ii + 1) * N_KEYS)
        for lt in range(h_ref.shape[1] // LANES):
            w = None
            for h in range(PEER_HEADS):
                hs = slice(h * N_KEYS, (h + 1) * N_KEYS)
                row = slice(h * ic + ii, h * ic + ii + 1)
                keep = a2_ref[lt, hs, :] >= thr_ref[lt, row, :]
                term = jnp.where(keep, e1_ref[lt, row, :] * e2_ref[lt, hs, :], 0.0)
                w = term if w is None else w + term
            gate_s[es, lt * LANES:(lt + 1) * LANES] = w
    halves = [slice(k * (ic // 2) * N_KEYS, (k + 1) * (ic // 2) * N_KEYS) for k in range(2)]
    acts = [_dot(u_ref[hf, :], h_ref[...]) for hf in halves]
    for k, hf in enumerate(halves):
        wg_s[hf, :] = (gate_s[hf, :] * jax.nn.gelu(acts[k])).astype(BF16)
    acc_s[...] += _dot(vt_ref[:, halves[0]], wg_s[halves[0], :]) + _dot(vt_ref[:, halves[1]], wg_s[halves[1], :])

    @pl.when(c == pl.num_programs(1) - 1)
    def _():
        out_ref[...] = acc_s[...].T


def _peer_dense(h_t, u_bf, vt_bf, thr, e1, a2, e2, tm):
    d, tp = h_t.shape
    ne = u_bf.shape[0]
    ec = EXPERT_ROWS_PER_STEP * N_KEYS
    hk = PEER_HEADS * N_KEYS
    once = pl.Buffered(1)
    nlt = tm // LANES
    slab = pl.BlockSpec((nlt, hk, LANES), lambda t, c: (t, 0, 0), pipeline_mode=once)
    rows = pl.BlockSpec((nlt, None, PEER_HEADS * EXPERT_ROWS_PER_STEP, LANES), lambda t, c: (t, c, 0, 0))
    return pl.pallas_call(
        _peer_dense_kernel,
        grid=(tp // tm, ne // ec),
        in_specs=[pl.BlockSpec((d, tm), lambda t, c: (0, t), pipeline_mode=once),
                  pl.BlockSpec((ec, d), lambda t, c: (c, 0)), pl.BlockSpec((d, ec), lambda t, c: (0, c)),
                  rows, rows, slab, slab],
        out_specs=pl.BlockSpec((tm, d), lambda t, c: (t, 0), pipeline_mode=once),
        out_shape=jax.ShapeDtypeStruct((tp, d), F32),
        scratch_shapes=[pltpu.VMEM((ec, tm), F32), pltpu.VMEM((ec, tm), BF16), pltpu.VMEM((d, tm), F32)],
        compiler_params=_cparams("arbitrary", "arbitrary"),
        name="peer_dense",
    )(h_t, u_bf, vt_bf, thr, e1, a2, e2)


def _final_kernel(x1_ref, f_ref, ga2_ref, fg_ref, y_ref):
    x2 = x1_ref[...] + ga2_ref[...] * f_ref[...]
    y = x2 * lax.rsqrt(jnp.mean(x2 * x2, axis=-1, keepdims=True) + EPS)
    y_ref[...] = y * fg_ref[...]


def _final(x1, f, f_row0, ga2, fg, per_row):
    r, d = x1.shape
    tm = r if per_row else 256
    assert f_row0 % tm == 0
    row_spec = pl.BlockSpec((tm, d), lambda i: (i, 0))
    f_spec = pl.BlockSpec((tm, d), lambda i: (i + f_row0 // tm, 0))
    vec_spec = pl.BlockSpec((1, d), lambda i: (0, 0))
    return pl.pallas_call(
        _final_kernel,
        grid=(r // tm,),
        in_specs=[row_spec, f_spec, row_spec if per_row else vec_spec, vec_spec],
        out_specs=row_spec,
        out_shape=jax.ShapeDtypeStruct((r, d), F32),
        compiler_params=_cparams("arbitrary"),
        name="final_sample" if per_row else "final_prompt",
    )(x1, f, ga2, fg)


def _prep_w_in(w_in):
    d = w_in.shape[0]
    g0 = 3 * (d // 2) + D_ATTN + 6 * KV_W
    w_gate = jnp.pad(w_in[:, g0:], ((0, 0), (0, LANES - 3 * N_HEADS)))
    return w_in[:, :g0].astype(BF16), w_gate.astype(BF16)


def _prep_w_out(w_out):
    d = w_out.shape[0]
    dc = d // 2
    wa = w_out[dc:].reshape(N_KV, GROUP, HEAD_DIM, d).transpose(1, 0, 2, 3).reshape(D_ATTN, d)
    return jnp.concatenate([w_out[:dc], wa], axis=0).astype(BF16)


def kernel(x_prompt, x_sample, c_prompt, c_sample, cache_kv, state_win, state_conv, page_table, w_ada, b_ada,
           norm1_g, norm2_g, w_in, conv_w, conv_b, cmp_pe, cmp_w1, cmp_b1, cmp_w2, cmp_b2, w_out, peer_wq,
           peer_keys, peer_u, peer_v, final_g):
    depth = w_ada.shape[0]
    assert depth == 1
    bp, t, d = x_prompt.shape
    assert bp == 1 and d - d // 2 == D_ATTN
    bs, ts, _ = x_sample.shape
    assert ts == 4
    dc = d // 2
    n_pages = page_table.shape[1]
    past = n_pages * PAGE
    wkeep = state_win.shape[2]
    assert wkeep == WINDOW and t % KEY_TILE == 0 and n_pages % PAGES_PER_STEP == 0
    rs = bs * ts

    n_c = 1 + bs
    n_cp = -(-n_c // 8) * 8
    c_all = jnp.concatenate([c_prompt, c_sample, jnp.zeros((n_cp - n_c, d), F32)], axis=0)
    ada = _ada(c_all, w_ada[0], b_ada[0])
    sh1, sc1, ga1, sh2, sc2, ga2 = [ada[:, k * d:(k + 1) * d] for k in range(6)]
    pr = lambda a: a[0:1]
    sm = lambda a: jnp.repeat(a[1:1 + bs], ts, axis=0)

    w_main, w_gate = _prep_w_in(w_in[0])
    wo_perm = _prep_w_out(w_out[0])
    g1 = norm1_g[0].reshape(1, d)
    g2 = norm2_g[0].reshape(1, d)
    cw = conv_w[0]
    cb = conv_b[0].reshape(1, dc)
    cmpw = _prep_cmp_weights(cmp_pe[0], cmp_w1[0], cmp_b1[0], cmp_w2[0], cmp_b2[0])

    xp = x_prompt[0]
    cos_p, sin_p = _rope_tables(jnp.arange(t))
    convy_p, q_p, rows_p, win_p, kvb_p, gates_p, zc_p = _proj(
        xp, g1, pr(sc1), pr(sh1), w_main, w_gate, cos_p, sin_p, cw, cb)
    ident = jnp.arange(t // PAGE, dtype=jnp.int32).reshape(1, t // PAGE)
    kc_p, vc_p = _compress(rows_p.reshape(t // PAGE, PAGE, 4 * KV_W), ident, cmpw, False)
    attn_p = _nsa_prompt(q_p, gates_p, kc_p[0], vc_p[0], kvb_p)
    wq_bf = peer_wq[0].astype(BF16)
    keys_bf = peer_keys[0].reshape(2 * PEER_HEADS, N_KEYS, -1).astype(BF16)
    x1_p, h2_p, s_p = _outproj(xp, convy_p, attn_p, pr(ga1), g2, pr(sc2), pr(sh2), wo_perm, wq_bf, keys_bf, False)

    xs = x_sample.reshape(rs, d)
    pos_s = jnp.tile(past + jnp.arange(ts), bs)
    cos_s, sin_s = _rope_tables(pos_s)
    st = state_conv[0]
    a1 = jnp.repeat(st[:, 1], ts, axis=0)
    a2 = jnp.stack([st[:, 0], st[:, 1], st[:, 1], st[:, 1]], axis=1).reshape(rs, dc)
    convy_s, q_s, rows_s, win_s, kvb_s, gates_s, zc_s = _proj(
        xs, g1, sm(sc1), sm(sh1), w_main, w_gate, cos_s, sin_s, cw, cb, a1, a2)
    pool = cache_kv[0].transpose(0, 2, 3, 4, 1).reshape(cache_kv.shape[1], 4 * KV_W, PAGE)
    kc_s, vc_s = _compress(pool, page_table, cmpw, True)
    q5 = q_s.reshape(bs, ts, N_KV, GROUP, 1, HEAD_DIM).transpose(0, 3, 2, 1, 4, 5)
    qrep = jnp.broadcast_to(q5, (bs, GROUP, N_KV, ts, N_KV, HEAD_DIM)).reshape(bs, GROUP * N_KV * ts, KV_W)
    gts = gates_s[:, :3 * N_HEADS].reshape(bs, ts, N_KV, GROUP, 3).transpose(0, 4, 3, 2, 1)
    gx = jnp.broadcast_to(gts.reshape(bs, 3, GROUP * N_KV * ts, 1), (bs, 3, GROUP * N_KV * ts, KV_W))
    new8 = jnp.pad(kvb_s.reshape(bs, ts, 4 * KV_W), ((0, 0), (0, 8 - ts), (0, 0)))
    sw = state_win[0].transpose(0, 2, 3, 4, 1).reshape(bs, 2 * KV_W, wkeep)
    o_s = _nsa_sample(pool, page_table, qrep, gx, kc_s, vc_s, sw, new8)
    o6 = o_s.reshape(bs, GROUP, N_KV, ts, N_KV, HEAD_DIM)
    gi = jnp.arange(N_KV)
    attn_s = o6[:, :, gi, :, gi, :]
    attn_s = attn_s.transpose(1, 3, 2, 0, 4).reshape(rs, D_ATTN).astype(BF16)
    x1_s, h2_s, s_s = _outproj(xs, convy_s, attn_s, sm(ga1), g2, sm(sc2), sm(sh2), wo_perm, wq_bf, keys_bf, True)

    ntok = t + rs
    tm = PEER_TOKEN_TILE
    tp = -(-ntok // tm) * tm
    padt = lambda a: jnp.pad(jnp.concatenate(a, axis=0), ((0, tp - ntok), (0, 0))).T
    h_t = padt([h2_p, h2_s])
    s_t = padt([s_p, s_s])
    a2p, e2p, thr, e1p = _peer_select(s_t)
    u_bf = peer_u[0].astype(BF16)
    vt_bf = peer_v[0].astype(BF16).T
    f = _peer_dense(h_t, u_bf, vt_bf, thr, e1p, a2p, e2p, tm)
    fg = final_g.reshape(1, d)
    y_p = _final(x1_p, f, 0, pr(ga2), fg, False)
    y_s = _final(x1_s, f, t, sm(ga2), fg, True)

    wmin = min(WINDOW, t)
    win_ctx = jnp.concatenate([state_win[0], win_s.reshape(bs, ts, 2, N_KV, HEAD_DIM)], axis=1)[:, ts:]
    zc3 = zc_s.reshape(bs, ts, dc)
    return (y_p.reshape(1, t, d), y_s.reshape(bs, ts, d),
            rows_p.reshape(1, 1, t, 4, N_KV, HEAD_DIM), rows_s.reshape(1, bs, ts, 4, N_KV, HEAD_DIM),
            win_p[t - wmin:].reshape(1, 1, wmin, 2, N_KV, HEAD_DIM), win_ctx[None],
            zc_p[8 - (3 - 1):].reshape(1, 1, 2, dc), zc3[:, ts - 2:][None])
```

```python
import functools

import numpy as np
import jax
import jax.numpy as jnp
from jax import lax
from jax.experimental import pallas as pl
from jax.experimental.pallas import tpu as pltpu

F32 = jnp.float32
BF16 = jnp.bfloat16

HEAD_DIM = 64
N_KV = 4
GROUP = 4
N_HEADS = N_KV * GROUP
KV_W = N_KV * HEAD_DIM
D_ATTN = N_HEADS * HEAD_DIM
CMP_STRIDE = 16
SLC_BLOCK = 64
TOP_N = 16
WINDOW = 512
Q_BLOCK = 128
PAGE = 128
FORCE_SCORE = 1.0e4
ROPE_THETA = 10000.0
N_KEYS = 128
PEER_HEADS = 8
PEER_TOPK = 16
EPS = 1e-6
LANES = 128
PAGES_PER_STEP = 16
KEY_TILE = 256
V7X_VMEM_LIMIT = 58 * 1024 * 1024
NEG_BIG = -1e30
LOG2E = 1.4426950408889634


def _cparams(*sem):
    return pltpu.CompilerParams(dimension_semantics=sem, vmem_limit_bytes=V7X_VMEM_LIMIT)


def _vmem():
    return pl.BlockSpec(memory_space=pltpu.VMEM)


def _dot(a, b):
    return jnp.dot(a, b, preferred_element_type=F32)


def _dot_nt(a, b):
    return lax.dot_general(a, b, (((1,), (1,)), ((), ())), preferred_element_type=F32)


def _split_dot(x, m):
    hi = x.astype(BF16)
    lo = (x - hi.astype(F32)).astype(BF16)
    return _dot(hi, m) + _dot(lo, m)


def _rope(x, cos, sinsg):
    w = x.shape[1]
    n = w // LANES
    c = jnp.concatenate([cos] * n, axis=1) if n > 1 else cos
    s = jnp.concatenate([sinsg] * n, axis=1) if n > 1 else sinsg
    lane = lax.broadcasted_iota(jnp.int32, x.shape, 1)
    first = (lane % HEAD_DIM) < (HEAD_DIM // 2)
    partner = jnp.where(first, pltpu.roll(x, w - HEAD_DIM // 2, 1), pltpu.roll(x, HEAD_DIM // 2, 1))
    return x * c + partner * s


def _rope_tables(pos):
    half = HEAD_DIM // 2
    inv = ROPE_THETA ** (-jnp.arange(half, dtype=F32) / half)
    ang = pos.astype(F32)[:, None] * inv[None, :]
    cos = jnp.cos(ang)
    sin = jnp.sin(ang)
    cos = jnp.concatenate([cos, cos, cos, cos], axis=1)
    sinsg = jnp.concatenate([-sin, sin, -sin, sin], axis=1)
    return cos, sinsg


def _rms_mod(xf, g, sc, sh):
    y = xf * lax.rsqrt(jnp.mean(xf * xf, axis=-1, keepdims=True) + EPS)
    return y * g * (1.0 + sc) + sh


def _ada_kernel(c_ref, w_ref, b_ref, o_ref):
    s = jax.nn.silu(c_ref[...]).astype(BF16)
    o_ref[...] = _dot(s, w_ref[...].astype(BF16)) + b_ref[...]


def _ada(c_all, w_ada, b_ada):
    r, d = c_all.shape
    n = w_ada.shape[1]
    tn = 1024
    return pl.pallas_call(
        _ada_kernel,
        grid=(n // tn,),
        in_specs=[pl.BlockSpec((r, d), lambda j: (0, 0)),
                  pl.BlockSpec((d, tn), lambda j: (0, j)),
                  pl.BlockSpec((1, tn), lambda j: (0, j))],
        out_specs=pl.BlockSpec((r, tn), lambda j: (0, j)),
        out_shape=jax.ShapeDtypeStruct((r, n), F32),
        compiler_params=_cparams("arbitrary"),
        name="ada",
    )(c_all, w_ada, b_ada.reshape(1, n))


def _proj_kernel(*refs, sample, tm, dc):
    if sample:
        (x_ref, g1_ref, sc_ref, sh_ref, w_ref, wg_ref, cos_ref, sin_ref, cw_ref, cb_ref, a1_ref, a2_ref,
         convy_ref, q_ref, rows_ref, win_ref, kvb_ref, gates_ref, zc_ref) = refs
    else:
        (x_ref, g1_ref, sc_ref, sh_ref, w_ref, wg_ref, cos_ref, sin_ref, cw_ref, cb_ref,
         convy_ref, q_ref, rows_ref, win_ref, kvb_ref, gates_ref, zc_ref, carry_ref) = refs
    h = _rms_mod(x_ref[...], g1_ref[...], sc_ref[...], sh_ref[...]).astype(BF16)

    def mm(a, b):
        return _dot(h, w_ref[:, a:b])

    cos = cos_ref[...]
    sin = sin_ref[...]
    q0 = 3 * dc
    kv0 = q0 + D_ATTN
    g0 = kv0 + 6 * KV_W

    b_g = mm(0, dc)
    zc = mm(dc, 2 * dc) * mm(2 * dc, 3 * dc)
    row = lax.broadcasted_iota(jnp.int32, zc.shape, 0)
    if sample:
        t = row % 4
        zm1 = jnp.where(t >= 1, pltpu.roll(zc, 1, 0), a1_ref[...])
        zm2 = jnp.where(t >= 2, pltpu.roll(zc, 2, 0), a2_ref[...])
        zc_ref[...] = zc
    else:
        @pl.when(pl.program_id(0) == 0)
        def _():
            carry_ref[...] = jnp.zeros_like(carry_ref)
        p1 = carry_ref[7:8, :]
        p2 = carry_ref[6:7, :]
        zm1 = jnp.where(row == 0, p1, pltpu.roll(zc, 1, 0))
        zm2 = jnp.where(row == 0, p2, jnp.where(row == 1, p1, pltpu.roll(zc, 2, 0)))
        carry_ref[...] = zc[tm - 8:tm, :]
        zc_ref[...] = zc[tm - 8:tm, :]
    conv = cb_ref[...] + cw_ref[0:1, :] * zm2
    conv = conv + cw_ref[1:2, :] * zm1
    conv = conv + cw_ref[2:3, :] * zc
    convy_ref[...] = (b_g * conv).astype(BF16)

    q = _rope(mm(q0, kv0), cos, sin) * (HEAD_DIM ** -0.5)
    q_ref[...] = q.astype(BF16)

    kv = mm(kv0, g0)
    k_cmp = kv[:, 0:KV_W]
    v_cmp = kv[:, KV_W:2 * KV_W]
    k_slc = _rope(kv[:, 2 * KV_W:3 * KV_W], cos, sin)
    v_slc = kv[:, 3 * KV_W:4 * KV_W]
    k_win = _rope(kv[:, 4 * KV_W:5 * KV_W], cos, sin)
    v_win = kv[:, 5 * KV_W:6 * KV_W]
    rows_ref[:, 0:KV_W] = k_cmp
    rows_ref[:, KV_W:2 * KV_W] = v_cmp
    rows_ref[:, 2 * KV_W:3 * KV_W] = k_slc
    rows_ref[:, 3 * KV_W:4 * KV_W] = v_slc
    win_ref[:, 0:KV_W] = k_win
    win_ref[:, KV_W:2 * KV_W] = v_win
    kvb_ref[:, 0:KV_W] = k_slc.astype(BF16)
    kvb_ref[:, KV_W:2 * KV_W] = v_slc.astype(BF16)
    kvb_ref[:, 2 * KV_W:3 * KV_W] = k_win.astype(BF16)
    kvb_ref[:, 3 * KV_W:4 * KV_W] = v_win.astype(BF16)
    gates_ref[...] = jax.nn.sigmoid(_dot(h, wg_ref[...]))


def _proj(x, g1, sc, sh, w_main, w_gate, cos, sin, conv_w, conv_b, a1=None, a2=None):
    sample = a1 is not None
    r, d = x.shape
    dc = d // 2
    tm = r if sample else 256
    row_spec = lambda wdt: pl.BlockSpec((tm, wdt), lambda i: (i, 0))
    full = lambda a: pl.BlockSpec(a.shape, lambda i: (0,) * a.ndim)
    mod_spec = row_spec(d) if sample else pl.BlockSpec((1, d), lambda i: (0, 0))
    in_specs = [row_spec(d), full(g1), mod_spec, mod_spec, _vmem(), _vmem(), row_spec(LANES), row_spec(LANES),
                full(conv_w), full(conv_b)]
    args = [x, g1, sc, sh, w_main, w_gate, cos, sin, conv_w, conv_b]
    scratch = []
    if sample:
        in_specs += [row_spec(dc), row_spec(dc)]
        args += [a1, a2]
        zc_shape = jax.ShapeDtypeStruct((r, dc), F32)
        zc_spec = row_spec(dc)
    else:
        zc_shape = jax.ShapeDtypeStruct((8, dc), F32)
        zc_spec = pl.BlockSpec((8, dc), lambda i: (0, 0))
        scratch = [pltpu.VMEM((8, dc), F32)]
    out_shape = (jax.ShapeDtypeStruct((r, dc), BF16), jax.ShapeDtypeStruct((r, D_ATTN), BF16),
                 jax.ShapeDtypeStruct((r, 4 * KV_W), F32), jax.ShapeDtypeStruct((r, 2 * KV_W), F32),
                 jax.ShapeDtypeStruct((r, 4 * KV_W), BF16), jax.ShapeDtypeStruct((r, LANES), F32), zc_shape)
    out_specs = (row_spec(dc), row_spec(D_ATTN), row_spec(4 * KV_W), row_spec(2 * KV_W),
                 row_spec(4 * KV_W), row_spec(LANES), zc_spec)
    return pl.pallas_call(
        functools.partial(_proj_kernel, sample=sample, tm=tm, dc=dc),
        grid=(r // tm,),
        in_specs=in_specs, out_specs=out_specs, out_shape=out_shape, scratch_shapes=scratch,
        compiler_params=_cparams("arbitrary"),
        name="proj_sample" if sample else "proj_prompt",
    )(*args)


def _cmp_kernel(pt_ref, *refs, transposed):
    npg = PAGES_PER_STEP
    pages = refs[:npg]
    (perm_ref, w1a_ref, w1b_ref, pea_ref, peb_ref, b1_ref, w2_ref, b2_ref, cos_ref, sin_ref,
     kc_ref, vc_ref, xa_s, xb_s, hprev_s) = refs[npg:]
    del pt_ref
    nch = PAGE // CMP_STRIDE
    rows = npg * nch

    @pl.when(pl.program_id(1) == 0)
    def _():
        hprev_s[...] = jnp.zeros_like(hprev_s)

    lane = lax.broadcasted_iota(jnp.int32, (nch, 2 * KV_W), 1)
    low = (lane % LANES) < HEAD_DIM
    perm = perm_ref[...]
    for k in range(npg):
        page = pages[k][...]
        for pe_ref, x_s in ((pea_ref, xa_s), (peb_ref, xb_s)):
            src = (page + pe_ref[...]).astype(BF16)
            y = _dot_nt(perm, src) if transposed else _dot(perm, src)
            for p in range(CMP_STRIDE // 2):
                ev = y[2 * p * nch:(2 * p + 1) * nch]
                od = y[(2 * p + 1) * nch:(2 * p + 2) * nch]
                a = jnp.where(low, ev, pltpu.roll(od, HEAD_DIM, 1))
                b = jnp.where(low, pltpu.roll(ev, 2 * KV_W - HEAD_DIM, 1), od)
                for j in range(4):
                    kv, ge = j // 2, 2 * (j % 2)
                    rsl = slice(nch * k, nch * (k + 1))
                    csl = slice(LANES * p, LANES * (p + 1))
                    x_s[kv, ge, rsl, csl] = a[:, LANES * j:LANES * (j + 1)]
                    x_s[kv, ge + 1, rsl, csl] = b[:, LANES * j:LANES * (j + 1)]

    row = lax.broadcasted_iota(jnp.int32, (N_KV * rows, 1), 0)
    for kv in range(2):
        ha = _dot(xa_s[kv].reshape(N_KV * rows, CMP_STRIDE * HEAD_DIM).astype(BF16), w1a_ref[kv])
        hb = _dot(xb_s[kv].reshape(N_KV * rows, CMP_STRIDE * HEAD_DIM).astype(BF16), w1b_ref[kv])
        hp = hprev_s[kv]
        prev = jnp.where(row % rows == 0, pltpu.roll(hp, N_KV * rows - (rows - 1), 0), pltpu.roll(ha, 1, 0))
        hprev_s[kv] = ha
        hid = jax.nn.gelu(prev + hb + b1_ref[kv])
        hcat = jnp.concatenate([hid[rows * g:rows * (g + 1)] for g in range(N_KV)], axis=1).astype(BF16)
        o = _dot(hcat, w2_ref[kv]) + b2_ref[kv]
        if kv == 0:
            kc_ref[...] = _rope(o, cos_ref[...], sin_ref[...]).astype(BF16)
        else:
            vc_ref[...] = o.astype(BF16)


def _compress(pool, page_table, cw, transposed):
    b, n_pages = page_table.shape
    npg = PAGES_PER_STEP
    nsteps = n_pages // npg
    rows = npg * (PAGE // CMP_STRIDE)
    nc = nsteps * rows
    cpos = CMP_STRIDE * jnp.arange(nc) + (CMP_STRIDE - 1)
    cos, sin = _rope_tables(cpos)
    page_block = (None, 2 * KV_W, PAGE) if transposed else (None, PAGE, 2 * KV_W)

    def page_spec(k):
        return pl.BlockSpec(page_block, lambda bi, i, pt: (pt[bi, i * npg + k], 0, 0))

    const = lambda a: pl.BlockSpec(a.shape, lambda bi, i, pt: (0,) * a.ndim)
    nch = PAGE // CMP_STRIDE
    perm = np.zeros((PAGE, PAGE), np.float32)
    for s in range(CMP_STRIDE):
        for c in range(nch):
            perm[s * nch + c, CMP_STRIDE * c + s] = 1.0
    pea, peb = (cw["pea"].T, cw["peb"].T) if transposed else (cw["pea"], cw["peb"])
    weights = [jnp.asarray(perm, BF16), cw["w1a"], cw["w1b"], pea, peb, cw["b1"], cw["w2bd"], cw["b2"]]
    tab_spec = pl.BlockSpec((rows, LANES), lambda bi, i, pt: (i, 0))
    out_spec = pl.BlockSpec((None, rows, KV_W), lambda bi, i, pt: (bi, i, 0))
    x_scratch = pltpu.VMEM((2, N_KV, rows, CMP_STRIDE * HEAD_DIM), F32)
    grid_spec = pltpu.PrefetchScalarGridSpec(
        num_scalar_prefetch=1,
        grid=(b, nsteps),
        in_specs=[page_spec(k) for k in range(npg)] + [const(a) for a in weights] + [tab_spec, tab_spec],
        out_specs=(out_spec, out_spec),
        scratch_shapes=[x_scratch, x_scratch, pltpu.VMEM((2, N_KV * rows, 2 * HEAD_DIM), F32)],
    )
    return pl.pallas_call(
        functools.partial(_cmp_kernel, transposed=transposed),
        grid_spec=grid_spec,
        out_shape=(jax.ShapeDtypeStruct((b, nc, KV_W), BF16), jax.ShapeDtypeStruct((b, nc, KV_W), BF16)),
        compiler_params=_cparams("arbitrary", "arbitrary"),
        name="compress",
    )(page_table, *([pool] * npg), *weights, cos, sin)


def _prep_cmp_weights(cmp_pe, cmp_w1, cmp_b1, cmp_w2, cmp_b2):
    s = CMP_STRIDE
    w2bd = jnp.zeros((2, N_KV * 2 * HEAD_DIM, KV_W), F32)
    for g in range(N_KV):
        w2bd = w2bd.at[:, g * 2 * HEAD_DIM:(g + 1) * 2 * HEAD_DIM, g * HEAD_DIM:(g + 1) * HEAD_DIM].set(cmp_w2)
    def pe_page(pe_half):
        t = jnp.broadcast_to(pe_half.transpose(1, 0, 2)[:, :, None, :], (s, 2, N_KV, HEAD_DIM))
        return jnp.tile(t.reshape(s, 2 * KV_W), (PAGE // s, 1))

    return {
        "w1a": cmp_w1[:, :s].reshape(2, s * HEAD_DIM, 2 * HEAD_DIM).astype(BF16),
        "w1b": cmp_w1[:, s:].reshape(2, s * HEAD_DIM, 2 * HEAD_DIM).astype(BF16),
        "pea": pe_page(cmp_pe[:, :s]),
        "peb": pe_page(cmp_pe[:, s:]),
        "b1": cmp_b1.reshape(2, 1, 2 * HEAD_DIM),
        "w2bd": w2bd.astype(BF16),
        "b2": jnp.tile(cmp_b2, (1, N_KV)).reshape(2, 1, KV_W),
    }


def _cmp_to_slc(nc, ns):
    j = np.arange(nc)[:, None] - 1
    b = np.arange(ns)[None, :]
    ov = (CMP_STRIDE * j < SLC_BLOCK * (b + 1)) & (CMP_STRIDE * j + 2 * CMP_STRIDE > SLC_BLOCK * b) & (j >= 0)
    return ov.astype(np.float32)


def _masked_softmax(s, mask):
    s = jnp.where(mask, s, -jnp.inf)
    m = jnp.max(s, axis=-1, keepdims=True)
    m = jnp.where(m > -jnp.inf, m, 0.0)
    p = jnp.exp(s - m)
    return p * (1.0 / jnp.maximum(jnp.sum(p, axis=-1, keepdims=True), 1e-30))


def _select_blocks(score, n_top):
    lane = lax.broadcasted_iota(jnp.int32, score.shape, 1)
    nb = score.shape[1]
    sel = jnp.zeros(score.shape, F32)
    cur = score
    for _ in range(n_top):
        mx = jnp.max(cur, axis=-1, keepdims=True)
        first = jnp.min(jnp.where(cur == mx, lane, nb), axis=-1, keepdims=True)
        hit = (lane == first) & (mx > -jnp.inf)
        sel = jnp.where(hit, 1.0, sel)
        cur = jnp.where(lane == first, -jnp.inf, cur)
    return sel


def _masked_softmax0(s, mask):
    s = jnp.where(mask, s, -jnp.inf)
    m = jnp.max(s, axis=0, keepdims=True)
    m = jnp.where(m > -jnp.inf, m, 0.0)
    p = jnp.exp(s - m)
    return p * (1.0 / jnp.maximum(jnp.sum(p, axis=0, keepdims=True), 1e-30))


def _select_blocks_t(score, n_top):
    rowi = lax.broadcasted_iota(jnp.int32, score.shape, 0)
    nb = score.shape[0]
    sel = jnp.zeros(score.shape, F32)
    cur = score
    for _ in range(n_top):
        mx = jnp.max(cur, axis=0, keepdims=True)
        first = jnp.min(jnp.where(cur == mx, rowi, nb), axis=0, keepdims=True)
        is_first = rowi == first
        sel = jnp.where(is_first & (mx > -jnp.inf), 1.0, sel)
        cur = jnp.where(is_first, -jnp.inf, cur)
    return sel


def _own_blocks(x, tq):
    rq = GROUP * tq
    return jnp.concatenate([x[g * HEAD_DIM:(g + 1) * HEAD_DIM, g * rq:(g + 1) * rq] for g in range(N_KV)], axis=0)


def _nsa_prompt_kernel(qt_ref, gt_ref, kc_ref, vct_ref, ks_ref, vst_ref, kw_ref, vwt_ref, mt_ref,
                       out_ref, qm_s, bias_s, acc_s, m_s, l_s, sa_s, sb_s, ma_s, mb_s, *, nc, ns):
    i = pl.program_id(0)
    tq = Q_BLOCK
    rq = GROUP * tq
    ncol = N_KV * rq
    colq = i * tq + lax.broadcasted_iota(jnp.int32, (1, rq), 1) % tq
    colq_g = i * tq + lax.broadcasted_iota(jnp.int32, (1, N_KV * tq), 1) % tq
    grows = [slice(g * HEAD_DIM, (g + 1) * HEAD_DIM) for g in range(N_KV)]
    gcols = [slice(g * rq, (g + 1) * rq) for g in range(N_KV)]

    @pl.when(i == 0)
    def _():
        qm_s[...] = jnp.zeros_like(qm_s)

    for g in range(N_KV):
        for r in range(GROUP):
            hd = (g * GROUP + r) * HEAD_DIM
            qm_s[grows[g], (g * GROUP + r) * tq:(g * GROUP + r + 1) * tq] = qt_ref[hd:hd + HEAD_DIM, :]

    crow = lax.broadcasted_iota(jnp.int32, (nc, 1), 0)
    cvalid = (crow >= 1) & (CMP_STRIDE * crow + (CMP_STRIDE - 1) <= colq)
    kc = kc_ref[...]
    o_c, psum = [], []
    scores = [_dot(kc, qm_s[:, gcols[g]]) for g in range(N_KV)]
    for g in range(N_KV):
        p = _masked_softmax0(scores[g], cvalid)
        o_c.append(_dot(vct_ref[grows[g], :], p.astype(BF16)))
        psum.append(sum(p[:, r * tq:(r + 1) * tq] for r in range(GROUP)))
    o_c = jnp.concatenate(o_c, axis=0)
    psum = jnp.concatenate(psum, axis=1)
    hi = psum.astype(BF16)
    lo = (psum - hi.astype(F32)).astype(BF16)
    imp = _dot(mt_ref[...], hi) + _dot(mt_ref[...], lo)
    blk = lax.broadcasted_iota(jnp.int32, (ns, 1), 0)
    cur = colq_g // SLC_BLOCK
    forced = (blk == 0) | (blk == cur) | (blk == cur - 1)
    score = jnp.where(blk <= cur, jnp.where(forced, FORCE_SCORE, imp), -jnp.inf)
    bias = (_select_blocks_t(score, min(TOP_N, ns)) - 1.0) * (-NEG_BIG)
    for g in range(N_KV):
        for r in range(GROUP):
            c0 = (g * GROUP + r) * tq
            bias_s[:, c0:c0 + tq] = bias[:, g * tq:(g + 1) * tq]

    w0 = pl.multiple_of(i * tq, tq)
    kw = kw_ref[pl.ds(w0, WINDOW + tq), :]
    nwt = (WINDOW + tq) // tq
    vwt = vwt_ref[pl.ds(i, nwt)]
    wpos = i * tq - WINDOW + lax.broadcasted_iota(jnp.int32, (WINDOW + tq, 1), 0)
    dist = colq - wpos
    wmask = (dist >= 0) & (dist < WINDOW) & (wpos >= 0)
    o_w = []
    scores = [_dot(kw, qm_s[:, gcols[g]]) for g in range(N_KV)]
    for g in range(N_KV):
        s = jnp.where(wmask, scores[g], -jnp.inf)
        p = jnp.exp2(((s - jnp.max(s, axis=0, keepdims=True)) * LOG2E).astype(BF16))
        r = _dot(jnp.concatenate([vwt[j, g] for j in range(nwt)], axis=1), p)
        o_w.append(r[0:HEAD_DIM] * (1.0 / r[HEAD_DIM:HEAD_DIM + 1]))
    o_w = jnp.concatenate(o_w, axis=0)

    acc_s[...] = jnp.zeros_like(acc_s)
    m_s[...] = jnp.full(m_s.shape, NEG_BIG, F32)
    l_s[...] = jnp.zeros_like(l_s)
    blocks_per_tile = KEY_TILE // SLC_BLOCK
    krow = lax.broadcasted_iota(jnp.int32, (KEY_TILE, 1), 0)

    def scores(kt, buf, mx, causal):
        k0 = pl.multiple_of(kt * KEY_TILE, KEY_TILE)
        kk = ks_ref[pl.ds(k0, KEY_TILE), :]
        raw = [_dot(kk, qm_s[:, gcols[g]]) for g in range(N_KV)]
        for g in range(N_KV):
            s = jnp.concatenate(
                [raw[g][j * SLC_BLOCK:(j + 1) * SLC_BLOCK] + bias_s[pl.ds(kt * blocks_per_tile + j, 1), gcols[g]]
                 for j in range(blocks_per_tile)], axis=0)
            if causal:
                s = jnp.where(k0 + krow <= colq, s, NEG_BIG)
            buf[g] = s
            mx[:, gcols[g]] = jnp.max(s, axis=0, keepdims=True)

    def tile(kt, buf, mx):
        va = vst_ref[kt]
        probs, alphas = [], []
        for g in range(N_KV):
            m_old = m_s[:, gcols[g]]
            m_new = jnp.maximum(m_old, mx[:, gcols[g]])
            m_s[:, gcols[g]] = m_new
            alphas.append(jnp.exp(m_old - m_new))
            probs.append(jnp.exp2(((buf[g] - m_new) * LOG2E).astype(BF16)))
        for g in range(N_KV):
            r = _dot(va[g], probs[g])
            acc_s[grows[g], :] = acc_s[grows[g], :] * alphas[g] + r[0:HEAD_DIM]
            l_s[:, gcols[g]] = alphas[g] * l_s[:, gcols[g]] + r[HEAD_DIM:HEAD_DIM + 1]

    last = (i * tq) // KEY_TILE

    scores(0, sa_s, ma_s, False)

    def body(j, carry):
        scores(2 * j + 1, sb_s, mb_s, False)
        tile(2 * j, sa_s, ma_s)
        scores(2 * j + 2, sa_s, ma_s, False)
        tile(2 * j + 1, sb_s, mb_s)
        return carry

    lax.fori_loop(0, last // 2, body, 0)

    @pl.when(last % 2 == 0)
    def _():
        scores(last, sa_s, ma_s, True)
        tile(last, sa_s, ma_s)

    @pl.when(last % 2 == 1)
    def _():
        scores(last, sb_s, mb_s, True)
        tile(last - 1, sa_s, ma_s)
        tile(last, sb_s, mb_s)

    rl = 1.0 / l_s[...]
    o_s = acc_s[...] * jnp.concatenate(
        [jnp.broadcast_to(rl[:, g * rq:(g + 1) * rq], (HEAD_DIM, rq)) for g in range(N_KV)], axis=0)

    gt = gt_ref[...]

    def gate(br):
        return jnp.concatenate(
            [jnp.concatenate(
                [jnp.broadcast_to(gt[(GROUP * g + r) * 3 + br:(GROUP * g + r) * 3 + br + 1, :], (HEAD_DIM, tq))
                 for r in range(GROUP)], axis=1) for g in range(N_KV)], axis=0)

    o = gate(0) * o_c + gate(1) * o_s + gate(2) * o_w
    for r in range(GROUP):
        out_ref[:, r * KV_W:(r + 1) * KV_W] = o[:, r * tq:(r + 1) * tq].T.astype(BF16)


def _nsa_prompt(q, gates, kc, vc, kvb):
    t = q.shape[0]
    nc = kc.shape[0]
    ns = t // SLC_BLOCK
    tq = Q_BLOCK
    qt = q.T
    gt = gates.T
    vct = vc.T
    ks = kvb[:, 0:KV_W]
    vst = kvb[:, KV_W:2 * KV_W].reshape(t // KEY_TILE, KEY_TILE, N_KV, HEAD_DIM).transpose(0, 2, 3, 1)
    vst = jnp.concatenate([vst, jnp.ones((t // KEY_TILE, N_KV, 16, KEY_TILE), BF16)], axis=2)
    kw = jnp.pad(kvb[:, 2 * KV_W:3 * KV_W], ((WINDOW, 0), (0, 0)))
    vwt = jnp.pad(kvb[:, 3 * KV_W:4 * KV_W], ((WINDOW, 0), (0, 0)))
    vwt = vwt.reshape((t + WINDOW) // tq, tq, N_KV, HEAD_DIM).transpose(0, 2, 3, 1)
    vwt = jnp.concatenate([vwt, jnp.ones(((t + WINDOW) // tq, N_KV, 16, tq), BF16)], axis=2)
    mt = jnp.asarray(_cmp_to_slc(nc, ns).T, BF16)
    full = lambda a: pl.BlockSpec(a.shape, lambda i: (0,) * a.ndim)
    ncol = N_KV * GROUP * tq
    return pl.pallas_call(
        functools.partial(_nsa_prompt_kernel, nc=nc, ns=ns),
        grid=(t // tq,),
        in_specs=[pl.BlockSpec((D_ATTN, tq), lambda i: (0, i)),
                  pl.BlockSpec((LANES, tq), lambda i: (0, i)),
                  full(kc), full(vct), _vmem(), _vmem(), _vmem(), _vmem(), full(mt)],
        out_specs=pl.BlockSpec((tq, D_ATTN), lambda i: (i, 0)),
        out_shape=jax.ShapeDtypeStruct((t, D_ATTN), BF16),
        scratch_shapes=[pltpu.VMEM((KV_W, ncol), BF16),
                        pltpu.VMEM((ns, ncol), F32),
                        pltpu.VMEM((KV_W, GROUP * tq), F32),
                        pltpu.VMEM((1, ncol), F32),
                        pltpu.VMEM((1, ncol), F32),
                        pltpu.VMEM((N_KV, KEY_TILE, GROUP * tq), F32),
                        pltpu.VMEM((N_KV, KEY_TILE, GROUP * tq), F32),
                        pltpu.VMEM((1, ncol), F32),
                        pltpu.VMEM((1, ncol), F32)],
        compiler_params=_cparams("arbitrary"),
        name="nsa_prompt",
    )(qt, gt, kc, vct, ks, vst, kw, vwt, mt)


def _nsa_sample_kernel(pt_ref, *refs, nc, ns, nsteps):
    npg = PAGES_PER_STEP
    pages = refs[:npg]
    (qrep_ref, gx_ref, kc_ref, vc_ref, sw_ref, new_ref, m_ref, e_ref, en_ref,
     out_ref, qall_s, sel_s, oc_s, ow_s, m_s, l_s, acc_s) = refs[npg:]
    del pt_ref
    i = pl.program_id(1)
    nr = GROUP * N_KV * 4
    row = lax.broadcasted_iota(jnp.int32, (nr, 1), 0)
    row_t = row % 4
    row_g = (row // 4) % N_KV
    lane_g = lax.broadcasted_iota(jnp.int32, (1, KV_W), 1) // HEAD_DIM
    own = row_g == lane_g
    new = new_ref[...]
    ucol = lax.broadcasted_iota(jnp.int32, (1, 8), 1)
    new_ok = (ucol <= row_t) & (ucol < 4)

    @pl.when(i == 0)
    def _():
        qall = jnp.where(own, qrep_ref[...], 0).astype(BF16)
        qall_s[...] = qall
        cidx = lax.broadcasted_iota(jnp.int32, (1, nc), 1)
        p = _masked_softmax(_dot_nt(qall, kc_ref[...]), cidx >= 1)
        oc_s[...] = _dot(p.astype(BF16), vc_ref[...])
        n16 = N_KV * 4
        psum = p[0:n16] + p[n16:2 * n16] + p[2 * n16:3 * n16] + p[3 * n16:4 * n16]
        imp = _split_dot(psum, m_ref[...])
        nb = imp.shape[1]
        blk = lax.broadcasted_iota(jnp.int32, (1, nb), 1)
        cur = ns - 1
        forced = (blk == 0) | (blk == cur) | (blk == cur - 1)
        score = jnp.where(blk <= cur, jnp.where(forced, FORCE_SCORE, imp), -jnp.inf)
        sel_s[...] = _select_blocks(score, min(TOP_N, ns)).astype(BF16)
        kw_t = sw_ref[0:KV_W, :].astype(BF16)
        vw_t = sw_ref[KV_W:2 * KV_W, :].astype(BF16)
        wcol = lax.broadcasted_iota(jnp.int32, (1, kw_t.shape[1]), 1)
        s1 = jnp.where(wcol > row_t, _dot(qall, kw_t), -jnp.inf)
        s2 = jnp.where(new_ok, _dot_nt(qall, new[:, 2 * KV_W:3 * KV_W]), -jnp.inf)
        mw = jnp.maximum(jnp.max(s1, axis=-1, keepdims=True), jnp.max(s2, axis=-1, keepdims=True))
        p1 = jnp.exp(s1 - mw)
        p2 = jnp.exp(s2 - mw)
        den = jnp.maximum(jnp.sum(p1, axis=-1, keepdims=True) + jnp.sum(p2, axis=-1, keepdims=True), 1e-30)
        ow_s[...] = (_dot_nt((p1 / den).astype(BF16), vw_t)
                     + _dot((p2 / den).astype(BF16), new[:, 3 * KV_W:4 * KV_W]))
        m_s[...] = jnp.full(m_s.shape, NEG_BIG, F32)
        l_s[...] = jnp.zeros_like(l_s)
        acc_s[...] = jnp.zeros_like(acc_s)

    qall = qall_s[...]
    kk_t = jnp.concatenate([pages[k][0:KV_W, :] for k in range(npg)], axis=1).astype(BF16)
    vv_t = jnp.concatenate([pages[k][KV_W:2 * KV_W, :] for k in range(npg)], axis=1).astype(BF16)
    mk = _dot(sel_s[...], e_ref[i]) > 0.5
    mk = jnp.concatenate([mk] * GROUP, axis=0)
    s = jnp.where(mk, _dot(qall, kk_t), NEG_BIG)
    m_old = m_s[...]
    m_new = jnp.maximum(m_old, jnp.max(s, axis=-1, keepdims=True))
    alpha = jnp.exp(m_old - m_new)
    p = jnp.exp(s - m_new)
    l_new = alpha * l_s[...] + jnp.sum(p, axis=-1, keepdims=True)
    acc_new = alpha * acc_s[...] + _dot_nt(p.astype(BF16), vv_t)
    m_s[...] = m_new
    l_s[...] = l_new
    acc_s[...] = acc_new

    @pl.when(i == nsteps - 1)
    def _():
        mkn = _dot(sel_s[...], en_ref[...]) > 0.5
        mkn = jnp.concatenate([mkn] * GROUP, axis=0) & new_ok
        sn = jnp.where(mkn, _dot_nt(qall, new[:, 0:KV_W]), NEG_BIG)
        m_f = jnp.maximum(m_new, jnp.max(sn, axis=-1, keepdims=True))
        al = jnp.exp(m_new - m_f)
        pn = jnp.exp(sn - m_f)
        l_f = al * l_new + jnp.sum(pn, axis=-1, keepdims=True)
        acc_f = al * acc_new + _dot(pn.astype(BF16), new[:, KV_W:2 * KV_W])
        o_s = acc_f / l_f
        out_ref[...] = gx_ref[0] * oc_s[...] + gx_ref[1] * o_s + gx_ref[2] * ow_s[...]


def _nsa_sample(pool, page_table, qrep, gx, kc, vc, sw, new8):
    b, n_pages = page_table.shape
    npg = PAGES_PER_STEP
    nsteps = n_pages // npg
    nc = kc.shape[1]
    past = n_pages * PAGE
    ns = past // SLC_BLOCK + 1
    nb = -(-ns // LANES) * LANES
    keys_step = npg * PAGE
    m = np.zeros((nc, nb), np.float32)
    m[:, :ns] = _cmp_to_slc(nc, ns)
    e = np.zeros((nsteps, nb, keys_step), np.float32)
    for st in range(nsteps):
        for j in range(keys_step):
            e[st, (st * keys_step + j) // SLC_BLOCK, j] = 1.0
    en = np.zeros((nb, 8), np.float32)
    en[ns - 1, :] = 1.0
    m, e, en = jnp.asarray(m, BF16), jnp.asarray(e, BF16), jnp.asarray(en, BF16)
    nr = GROUP * N_KV * 4

    def page_spec(k):
        return pl.BlockSpec((None, 2 * KV_W, PAGE), lambda bi, i, pt: (pt[bi, i * npg + k], 1, 0))

    per_b = lambda a: pl.BlockSpec((None,) + a.shape[1:], lambda bi, i, pt: (bi,) + (0,) * (a.ndim - 1))
    const = lambda a: pl.BlockSpec(a.shape, lambda bi, i, pt: (0,) * a.ndim)
    grid_spec = pltpu.PrefetchScalarGridSpec(
        num_scalar_prefetch=1,
        grid=(b, nsteps),
        in_specs=[page_spec(k) for k in range(npg)]
        + [per_b(qrep), per_b(gx), per_b(kc), per_b(vc), per_b(sw), per_b(new8), const(m), _vmem(), const(en)],
        out_specs=pl.BlockSpec((None, nr, KV_W), lambda bi, i, pt: (bi, 0, 0)),
        scratch_shapes=[pltpu.VMEM((nr, KV_W), BF16), pltpu.VMEM((N_KV * 4, nb), BF16),
                        pltpu.VMEM((nr, KV_W), F32), pltpu.VMEM((nr, KV_W), F32),
                        pltpu.VMEM((nr, 1), F32), pltpu.VMEM((nr, 1), F32), pltpu.VMEM((nr, KV_W), F32)],
    )
    return pl.pallas_call(
        functools.partial(_nsa_sample_kernel, nc=nc, ns=ns, nsteps=nsteps),
        grid_spec=grid_spec,
        out_shape=jax.ShapeDtypeStruct((b, nr, KV_W), F32),
        compiler_params=_cparams("arbitrary", "arbitrary"),
        name="nsa_sample",
    )(page_table, *([pool] * npg), qrep, gx, kc, vc, sw, new8, m, e, en)


def _outproj_kernel(x_ref, cy_ref, at_ref, ga1_ref, g2_ref, sc2_ref, sh2_ref, wo_ref, wq_ref, keys_ref,
                    x1_ref, h2_ref, s_ref, *, dc):
    u = _dot(cy_ref[...], wo_ref[0:dc, :]) + _dot(at_ref[...], wo_ref[dc:, :])
    x1 = x_ref[...] + ga1_ref[...] * u
    x1_ref[...] = x1
    h2 = _rms_mod(x1, g2_ref[...], sc2_ref[...], sh2_ref[...]).astype(BF16)
    h2_ref[...] = h2
    qp = _dot(h2, wq_ref[...]).astype(BF16)
    for hc in range(2 * PEER_HEADS):
        s_ref[:, hc * N_KEYS:(hc + 1) * N_KEYS] = _dot_nt(qp[:, hc * LANES:(hc + 1) * LANES], keys_ref[hc])


def _outproj(x, convy, attn, ga1, g2, sc2, sh2, wo, wq, keys, per_row):
    r, d = x.shape
    dc = d // 2
    tm = r if per_row else 256
    row_spec = lambda wdt: pl.BlockSpec((tm, wdt), lambda i: (i, 0))
    mod_spec = row_spec(d) if per_row else pl.BlockSpec((1, d), lambda i: (0, 0))
    nsc = keys.shape[0] * N_KEYS
    return pl.pallas_call(
        functools.partial(_outproj_kernel, dc=dc),
        grid=(r // tm,),
        in_specs=[row_spec(d), row_spec(dc), row_spec(d - dc), mod_spec,
                  pl.BlockSpec((1, d), lambda i: (0, 0)), mod_spec, mod_spec, _vmem(), _vmem(), _vmem()],
        out_specs=(row_spec(d), row_spec(d), row_spec(nsc)),
        out_shape=(jax.ShapeDtypeStruct((r, d), F32), jax.ShapeDtypeStruct((r, d), BF16),
                   jax.ShapeDtypeStruct((r, nsc), F32)),
        compiler_params=_cparams("arbitrary"),
        name="outproj_sample" if per_row else "outproj_prompt",
    )(x, convy, attn, ga1, g2, sc2, sh2, wo, wq, keys)


def _sort_network(n):
    pairs = []

    def merge(lo, cnt, r):
        step = r * 2
        if step < cnt:
            merge(lo, cnt, step)
            merge(lo + r, cnt, step)
            for i in range(lo + r, lo + cnt - r, step):
                pairs.append((i, i + r))
        else:
            pairs.append((lo, lo + r))

    def sort(lo, cnt):
        if cnt > 1:
            half = cnt // 2
            sort(lo, half)
            sort(lo + half, half)
            merge(lo, cnt, 1)

    sort(0, n)
    return pairs


def _prune_network(pairs, wanted):
    need = set(wanted)
    keep = []
    for i, j in reversed(pairs):
        if i in need or j in need:
            keep.append((i, j))
            need.update((i, j))
    return keep[::-1]


def _apply_network(x, pairs):
    x = list(x)
    for i, j in pairs:
        x[i], x[j] = jnp.maximum(x[i], x[j]), jnp.minimum(x[i], x[j])
    return x


_SORT16 = _sort_network(PEER_TOPK)
_CAND_PAIRS = [(a, b) for a in range(PEER_TOPK + 1) for b in range(PEER_TOPK + 1)
               if (a + 1) * (b + 1) <= PEER_TOPK + 1]
_CAND_WIRES = 64
_SELECT_16_17 = _prune_network(_sort_network(_CAND_WIRES), (PEER_TOPK - 1, PEER_TOPK))


def _top17(s):
    k = PEER_TOPK
    sub = 8
    lst = _apply_network([s[sub * v:sub * (v + 1)] for v in range(k)], _SORT16)
    for dist in (1, 2, 4):
        c = [jnp.maximum(lst[v], pltpu.roll(lst[k - 1 - v], dist, 0)) for v in range(k)]
        d = k // 2
        while d >= 1:
            for i in range(k):
                if i & d == 0:
                    c[i], c[i + d] = jnp.maximum(c[i], c[i + d]), jnp.minimum(c[i], c[i + d])
            d //= 2
        lst = c
    top = [v[sub - 1:sub, :] for v in lst]
    v16 = top[k - 1]
    ge = s >= v16
    cnt = jnp.sum(jnp.where(ge, 1.0, 0.0), axis=0, keepdims=True)
    below = jnp.max(jnp.where(ge, -jnp.inf, s), axis=0, keepdims=True)
    top.append(jnp.where(cnt > k, v16, below))
    return top


def _peer_select_kernel(s_ref, a2_ref, e2_ref, thr_ref, e1_ref):
    k = PEER_TOPK
    nh = PEER_HEADS
    half = lambda h, c: s_ref[(2 * h + c) * N_KEYS:(2 * h + c + 1) * N_KEYS, :]
    tops1 = [_top17(half(h, 0)) for h in range(nh)]
    tops2 = [_top17(half(h, 1)) for h in range(nh)]
    v1 = [jnp.concatenate([tops1[h][a] for h in range(nh)], axis=0) for a in range(k + 1)]
    v2 = [jnp.concatenate([tops2[h][a] for h in range(nh)], axis=0) for a in range(k + 1)]
    d1 = [v - v1[0] for v in v1]
    d2 = [v - v2[0] for v in v2]
    cands = [d1[a] + d2[b] for a, b in _CAND_PAIRS]
    pad = [jnp.full_like(cands[0], -jnp.inf)] * (_CAND_WIRES - len(cands))
    srt = _apply_network(cands + pad, _SELECT_16_17)
    tau = 0.5 * (srt[k - 1] + srt[k])
    x1 = [jnp.exp(d) for d in d1]
    x2 = [jnp.exp(d) for d in d2]
    z = jnp.zeros_like(tau)
    for (a, b), c in zip(_CAND_PAIRS, cands):
        z = z + jnp.where(c >= tau, x1[a] * x2[b], 0.0)
    rz = 1.0 / z
    ic = EXPERT_ROWS_PER_STEP
    for h in range(nh):
        a1 = half(h, 0) - v1[0][h:h + 1]
        a2 = half(h, 1) - v2[0][h:h + 1]
        thr = tau[h:h + 1] - a1
        e1 = jnp.exp(a1) * rz[h:h + 1]
        for c in range(N_KEYS // ic):
            thr_ref[c, h * ic:(h + 1) * ic, :] = thr[c * ic:(c + 1) * ic]
            e1_ref[c, h * ic:(h + 1) * ic, :] = e1[c * ic:(c + 1) * ic]
        a2_ref[h * N_KEYS:(h + 1) * N_KEYS, :] = a2
        e2_ref[h * N_KEYS:(h + 1) * N_KEYS, :] = jnp.exp(a2)


def _peer_select(s_t):
    nrow, tp = s_t.shape
    assert N_KEYS == 8 * PEER_TOPK
    tn = LANES
    hk = PEER_HEADS * N_KEYS
    ic = EXPERT_ROWS_PER_STEP
    flat = pl.BlockSpec((None, hk, tn), lambda t: (t, 0, 0))
    cube = pl.BlockSpec((None, N_KEYS // ic, PEER_HEADS * ic, tn), lambda t: (t, 0, 0, 0))
    return pl.pallas_call(
        _peer_select_kernel,
        grid=(tp // tn,),
        in_specs=[pl.BlockSpec((nrow, tn), lambda t: (0, t))],
        out_specs=(flat, flat, cube, cube),
        out_shape=(jax.ShapeDtypeStruct((tp // tn, hk, tn), F32),) * 2
        + (jax.ShapeDtypeStruct((tp // tn, N_KEYS // ic, PEER_HEADS * ic, tn), F32),) * 2,
        compiler_params=_cparams("arbitrary"),
        name="peer_select",
    )(s_t)


EXPERT_ROWS_PER_STEP = 8
PEER_TOKEN_TILE = 768


def _peer_dense_kernel(h_ref, u_ref, vt_ref, thr_ref, e1_ref, a2_ref, e2_ref, out_ref, gate_s, wg_s, acc_s):
    c = pl.program_id(1)

    @pl.when(c == 0)
    def _():
        acc_s[...] = jnp.zeros_like(acc_s)

    ic = EXPERT_ROWS_PER_STEP
    for ii in range(ic):
        es = slice(ii * N_KEYS, (ii + 1) * N_KEYS)
        for lt in range(h_ref.shape[1] // LANES):
            w = None
            for h in range(PEER_HEADS):
                hs = slice(h * N_KEYS, (h + 1) * N_KEYS)
                row = slice(h * ic + ii, h * ic + ii + 1)
                keep = a2_ref[lt, hs, :] >= thr_ref[lt, row, :]
                term = jnp.where(keep, e1_ref[lt, row, :] * e2_ref[lt, hs, :], 0.0)
                w = term if w is None else w + term
            gate_s[es, lt * LANES:(lt + 1) * LANES] = w
    halves = [slice(k * (ic // 2) * N_KEYS, (k + 1) * (ic // 2) * N_KEYS) for k in range(2)]
    acts = [_dot(u_ref[hf, :], h_ref[...]) for hf in halves]
    for k, hf in enumerate(halves):
        wg_s[hf, :] = (gate_s[hf, :] * jax.nn.gelu(acts[k])).astype(BF16)
    acc_s[...] += _dot(vt_ref[:, halves[0]], wg_s[halves[0], :]) + _dot(vt_ref[:, halves[1]], wg_s[halves[1], :])

    @pl.when(c == pl.num_programs(1) - 1)
    def _():
        out_ref[...] = acc_s[...].T


def _peer_dense(h_t, u_bf, vt_bf, thr, e1, a2, e2, tm):
    d, tp = h_t.shape
    ne = u_bf.shape[0]
    ec = EXPERT_ROWS_PER_STEP * N_KEYS
    hk = PEER_HEADS * N_KEYS
    once = pl.Buffered(1)
    nlt = tm // LANES
    slab = pl.BlockSpec((nlt, hk, LANES), lambda t, c: (t, 0, 0), pipeline_mode=once)
    rows = pl.BlockSpec((nlt, None, PEER_HEADS * EXPERT_ROWS_PER_STEP, LANES), lambda t, c: (t, c, 0, 0))
    return pl.pallas_call(
        _peer_dense_kernel,
        grid=(tp // tm, ne // ec),
        in_specs=[pl.BlockSpec((d, tm), lambda t, c: (0, t), pipeline_mode=once),
                  pl.BlockSpec((ec, d), lambda t, c: (c, 0)), pl.BlockSpec((d, ec), lambda t, c: (0, c)),
                  rows, rows, slab, slab],
        out_specs=pl.BlockSpec((tm, d), lambda t, c: (t, 0), pipeline_mode=once),
        out_shape=jax.ShapeDtypeStruct((tp, d), F32),
        scratch_shapes=[pltpu.VMEM((ec, tm), F32), pltpu.VMEM((ec, tm), BF16), pltpu.VMEM((d, tm), F32)],
        compiler_params=_cparams("arbitrary", "arbitrary"),
        name="peer_dense",
    )(h_t, u_bf, vt_bf, thr, e1, a2, e2)


def _final_kernel(x1_ref, f_ref, ga2_ref, fg_ref, y_ref):
    x2 = x1_ref[...] + ga2_ref[...] * f_ref[...]
    y = x2 * lax.rsqrt(jnp.mean(x2 * x2, axis=-1, keepdims=True) + EPS)
    y_ref[...] = y * fg_ref[...]


def _final(x1, f, f_row0, ga2, fg, per_row):
    r, d = x1.shape
    tm = r if per_row else 256
    assert f_row0 % tm == 0
    row_spec = pl.BlockSpec((tm, d), lambda i: (i, 0))
    f_spec = pl.BlockSpec((tm, d), lambda i: (i + f_row0 // tm, 0))
    vec_spec = pl.BlockSpec((1, d), lambda i: (0, 0))
    return pl.pallas_call(
        _final_kernel,
        grid=(r // tm,),
        in_specs=[row_spec, f_spec, row_spec if per_row else vec_spec, vec_spec],
        out_specs=row_spec,
        out_shape=jax.ShapeDtypeStruct((r, d), F32),
        compiler_params=_cparams("arbitrary"),
        name="final_sample" if per_row else "final_prompt",
    )(x1, f, ga2, fg)


def _prep_w_in(w_in):
    d = w_in.shape[0]
    g0 = 3 * (d // 2) + D_ATTN + 6 * KV_W
    w_gate = jnp.pad(w_in[:, g0:], ((0, 0), (0, LANES - 3 * N_HEADS)))
    return w_in[:, :g0].astype(BF16), w_gate.astype(BF16)


def _prep_w_out(w_out):
    d = w_out.shape[0]
    dc = d // 2
    wa = w_out[dc:].reshape(N_KV, GROUP, HEAD_DIM, d).transpose(1, 0, 2, 3).reshape(D_ATTN, d)
    return jnp.concatenate([w_out[:dc], wa], axis=0).astype(BF16)


def kernel(x_prompt, x_sample, c_prompt, c_sample, cache_kv, state_win, state_conv, page_table, w_ada, b_ada,
           norm1_g, norm2_g, w_in, conv_w, conv_b, cmp_pe, cmp_w1, cmp_b1, cmp_w2, cmp_b2, w_out, peer_wq,
           peer_keys, peer_u, peer_v, final_g):
    depth = w_ada.shape[0]
    assert depth == 1
    bp, t, d = x_prompt.shape
    assert bp == 1 and d - d // 2 == D_ATTN
    bs, ts, _ = x_sample.shape
    assert ts == 4
    dc = d // 2
    n_pages = page_table.shape[1]
    past = n_pages * PAGE
    wkeep = state_win.shape[2]
    assert wkeep == WINDOW and t % KEY_TILE == 0 and n_pages % PAGES_PER_STEP == 0
    rs = bs * ts

    n_c = 1 + bs
    n_cp = -(-n_c // 8) * 8
    c_all = jnp.concatenate([c_prompt, c_sample, jnp.zeros((n_cp - n_c, d), F32)], axis=0)
    ada = _ada(c_all, w_ada[0], b_ada[0])
    sh1, sc1, ga1, sh2, sc2, ga2 = [ada[:, k * d:(k + 1) * d] for k in range(6)]
    pr = lambda a: a[0:1]
    sm = lambda a: jnp.repeat(a[1:1 + bs], ts, axis=0)

    w_main, w_gate = _prep_w_in(w_in[0])
    wo_perm = _prep_w_out(w_out[0])
    g1 = norm1_g[0].reshape(1, d)
    g2 = norm2_g[0].reshape(1, d)
    cw = conv_w[0]
    cb = conv_b[0].reshape(1, dc)
    cmpw = _prep_cmp_weights(cmp_pe[0], cmp_w1[0], cmp_b1[0], cmp_w2[0], cmp_b2[0])

    xp = x_prompt[0]
    cos_p, sin_p = _rope_tables(jnp.arange(t))
    convy_p, q_p, rows_p, win_p, kvb_p, gates_p, zc_p = _proj(
        xp, g1, pr(sc1), pr(sh1), w_main, w_gate, cos_p, sin_p, cw, cb)
    ident = jnp.arange(t // PAGE, dtype=jnp.int32).reshape(1, t // PAGE)
    kc_p, vc_p = _compress(rows_p.reshape(t // PAGE, PAGE, 4 * KV_W), ident, cmpw, False)
    attn_p = _nsa_prompt(q_p, gates_p, kc_p[0], vc_p[0], kvb_p)
    wq_bf = peer_wq[0].astype(BF16)
    keys_bf = peer_keys[0].reshape(2 * PEER_HEADS, N_KEYS, -1).astype(BF16)
    x1_p, h2_p, s_p = _outproj(xp, convy_p, attn_p, pr(ga1), g2, pr(sc2), pr(sh2), wo_perm, wq_bf, keys_bf, False)

    xs = x_sample.reshape(rs, d)
    pos_s = jnp.tile(past + jnp.arange(ts), bs)
    cos_s, sin_s = _rope_tables(pos_s)
    st = state_conv[0]
    a1 = jnp.repeat(st[:, 1], ts, axis=0)
    a2 = jnp.stack([st[:, 0], st[:, 1], st[:, 1], st[:, 1]], axis=1).reshape(rs, dc)
    convy_s, q_s, rows_s, win_s, kvb_s, gates_s, zc_s = _proj(
        xs, g1, sm(sc1), sm(sh1), w_main, w_gate, cos_s, sin_s, cw, cb, a1, a2)
    pool = cache_kv[0].transpose(0, 2, 3, 4, 1).reshape(cache_kv.shape[1], 4 * KV_W, PAGE)
    kc_s, vc_s = _compress(pool, page_table, cmpw, True)
    q5 = q_s.reshape(bs, ts, N_KV, GROUP, 1, HEAD_DIM).transpose(0, 3, 2, 1, 4, 5)
    qrep = jnp.broadcast_to(q5, (bs, GROUP, N_KV, ts, N_KV, HEAD_DIM)).reshape(bs, GROUP * N_KV * ts, KV_W)
    gts = gates_s[:, :3 * N_HEADS].reshape(bs, ts, N_KV, GROUP, 3).transpose(0, 4, 3, 2, 1)
    gx = jnp.broadcast_to(gts.reshape(bs, 3, GROUP * N_KV * ts, 1), (bs, 3, GROUP * N_KV * ts, KV_W))
    new8 = jnp.pad(kvb_s.reshape(bs, ts, 4 * KV_W), ((0, 0), (0, 8 - ts), (0, 0)))
    sw = state_win[0].transpose(0, 2, 3, 4, 1).reshape(bs, 2 * KV_W, wkeep)
    o_s = _nsa_sample(pool, page_table, qrep, gx, kc_s, vc_s, sw, new8)
    o6 = o_s.reshape(bs, GROUP, N_KV, ts, N_KV, HEAD_DIM)
    gi = jnp.arange(N_KV)
    attn_s = o6[:, :, gi, :, gi, :]
    attn_s = attn_s.transpose(1, 3, 2, 0, 4).reshape(rs, D_ATTN).astype(BF16)
    x1_s, h2_s, s_s = _outproj(xs, convy_s, attn_s, sm(ga1), g2, sm(sc2), sm(sh2), wo_perm, wq_bf, keys_bf, True)

    ntok = t + rs
    tm = PEER_TOKEN_TILE
    tp = -(-ntok // tm) * tm
    padt = lambda a: jnp.pad(jnp.concatenate(a, axis=0), ((0, tp - ntok), (0, 0))).T
    h_t = padt([h2_p, h2_s])
    s_t = padt([s_p, s_s])
    a2p, e2p, thr, e1p = _peer_select(s_t)
    u_bf = peer_u[0].astype(BF16)
    vt_bf = peer_v[0].astype(BF16).T
    f = _peer_dense(h_t, u_bf, vt_bf, thr, e1p, a2p, e2p, tm)
    fg = final_g.reshape(1, d)
    y_p = _final(x1_p, f, 0, pr(ga2), fg, False)
    y_s = _final(x1_s, f, t, sm(ga2), fg, True)

    wmin = min(WINDOW, t)
    win_ctx = jnp.concatenate([state_win[0], win_s.reshape(bs, ts, 2, N_KV, HEAD_DIM)], axis=1)[:, ts:]
    zc3 = zc_s.reshape(bs, ts, dc)
    return (y_p.reshape(1, t, d), y_s.reshape(bs, ts, d),
            rows_p.reshape(1, 1, t, 4, N_KV, HEAD_DIM), rows_s.reshape(1, bs, ts, 4, N_KV, HEAD_DIM),
            win_p[t - wmin:].reshape(1, 1, wmin, 2, N_KV, HEAD_DIM), win_ctx[None],
            zc_p[8 - (3 - 1):].reshape(1, 1, 2, dc), zc3[:, ts - 2:][None])
```

```python
import functools

import numpy as np
import jax
import jax.numpy as jnp
from jax import lax
from jax.experimental import pallas as pl
from jax.experimental.pallas import tpu as pltpu

F32 = jnp.float32
BF16 = jnp.bfloat16

HEAD_DIM = 64
N_KV = 4
GROUP = 4
N_HEADS = N_KV * GROUP
KV_W = N_KV * HEAD_DIM
D_ATTN = N_HEADS * HEAD_DIM
CMP_STRIDE = 16
SLC_BLOCK = 64
TOP_N = 16
WINDOW = 512
Q_BLOCK = 128
PAGE = 128
FORCE_SCORE = 1.0e4
ROPE_THETA = 10000.0
N_KEYS = 128
PEER_HEADS = 8
PEER_TOPK = 16
EPS = 1e-6
LANES = 128
PAGES_PER_STEP = 32
KEY_TILE = 256
V7X_VMEM_LIMIT = 58 * 1024 * 1024
NEG_BIG = -1e30
LOG2E = 1.4426950408889634


def _cparams(*sem):
    return pltpu.CompilerParams(dimension_semantics=sem, vmem_limit_bytes=V7X_VMEM_LIMIT)


def _vmem():
    return pl.BlockSpec(memory_space=pltpu.VMEM)


def _dot(a, b):
    return jnp.dot(a, b, preferred_element_type=F32)


def _dot_nt(a, b):
    return lax.dot_general(a, b, (((1,), (1,)), ((), ())), preferred_element_type=F32)


def _split_dot(x, m):
    hi = x.astype(BF16)
    lo = (x - hi.astype(F32)).astype(BF16)
    return _dot(hi, m) + _dot(lo, m)


def _rope(x, cos, sinsg):
    w = x.shape[1]
    n = w // LANES
    c = jnp.concatenate([cos] * n, axis=1) if n > 1 else cos
    s = jnp.concatenate([sinsg] * n, axis=1) if n > 1 else sinsg
    lane = lax.broadcasted_iota(jnp.int32, x.shape, 1)
    first = (lane % HEAD_DIM) < (HEAD_DIM // 2)
    partner = jnp.where(first, pltpu.roll(x, w - HEAD_DIM // 2, 1), pltpu.roll(x, HEAD_DIM // 2, 1))
    return x * c + partner * s


def _rope_tables(pos):
    half = HEAD_DIM // 2
    inv = ROPE_THETA ** (-jnp.arange(half, dtype=F32) / half)
    ang = pos.astype(F32)[:, None] * inv[None, :]
    cos = jnp.cos(ang)
    sin = jnp.sin(ang)
    cos = jnp.concatenate([cos, cos, cos, cos], axis=1)
    sinsg = jnp.concatenate([-sin, sin, -sin, sin], axis=1)
    return cos, sinsg


def _rms_mod(xf, g, sc, sh):
    y = xf * lax.rsqrt(jnp.mean(xf * xf, axis=-1, keepdims=True) + EPS)
    return y * g * (1.0 + sc) + sh


def _ada_kernel(c_ref, w_ref, b_ref, o_ref):
    s = jax.nn.silu(c_ref[...]).astype(BF16)
    o_ref[...] = _dot(s, w_ref[...].astype(BF16)) + b_ref[...]


def _ada(c_all, w_ada, b_ada):
    r, d = c_all.shape
    n = w_ada.shape[1]
    tn = 1024
    return pl.pallas_call(
        _ada_kernel,
        grid=(n // tn,),
        in_specs=[pl.BlockSpec((r, d), lambda j: (0, 0)),
                  pl.BlockSpec((d, tn), lambda j: (0, j)),
                  pl.BlockSpec((1, tn), lambda j: (0, j))],
        out_specs=pl.BlockSpec((r, tn), lambda j: (0, j)),
        out_shape=jax.ShapeDtypeStruct((r, n), F32),
        compiler_params=_cparams("arbitrary"),
        name="ada",
    )(c_all, w_ada, b_ada.reshape(1, n))


def _proj_kernel(*refs, sample, tm, dc):
    if sample:
        (x_ref, g1_ref, sc_ref, sh_ref, w_ref, wg_ref, cos_ref, sin_ref, cw_ref, cb_ref, a1_ref, a2_ref,
         convy_ref, q_ref, rows_ref, win_ref, kvb_ref, gates_ref, zc_ref) = refs
    else:
        (x_ref, g1_ref, sc_ref, sh_ref, w_ref, wg_ref, cos_ref, sin_ref, cw_ref, cb_ref,
         convy_ref, q_ref, rows_ref, win_ref, kvb_ref, gates_ref, zc_ref, carry_ref) = refs
    h = _rms_mod(x_ref[...], g1_ref[...], sc_ref[...], sh_ref[...]).astype(BF16)

    def mm(a, b):
        return _dot(h, w_ref[:, a:b])

    cos = cos_ref[...]
    sin = sin_ref[...]
    q0 = 3 * dc
    kv0 = q0 + D_ATTN
    g0 = kv0 + 6 * KV_W

    b_g = mm(0, dc)
    zc = mm(dc, 2 * dc) * mm(2 * dc, 3 * dc)
    row = lax.broadcasted_iota(jnp.int32, zc.shape, 0)
    if sample:
        t = row % 4
        zm1 = jnp.where(t >= 1, pltpu.roll(zc, 1, 0), a1_ref[...])
        zm2 = jnp.where(t >= 2, pltpu.roll(zc, 2, 0), a2_ref[...])
        zc_ref[...] = zc
    else:
        @pl.when(pl.program_id(0) == 0)
        def _():
            carry_ref[...] = jnp.zeros_like(carry_ref)
        p1 = carry_ref[7:8, :]
        p2 = carry_ref[6:7, :]
        zm1 = jnp.where(row == 0, p1, pltpu.roll(zc, 1, 0))
        zm2 = jnp.where(row == 0, p2, jnp.where(row == 1, p1, pltpu.roll(zc, 2, 0)))
        carry_ref[...] = zc[tm - 8:tm, :]
        zc_ref[...] = zc[tm - 8:tm, :]
    conv = cb_ref[...] + cw_ref[0:1, :] * zm2
    conv = conv + cw_ref[1:2, :] * zm1
    conv = conv + cw_ref[2:3, :] * zc
    convy_ref[...] = (b_g * conv).astype(BF16)

    q = _rope(mm(q0, kv0), cos, sin) * (HEAD_DIM ** -0.5)
    q_ref[...] = q.astype(BF16)

    kv = mm(kv0, g0)
    k_cmp = kv[:, 0:KV_W]
    v_cmp = kv[:, KV_W:2 * KV_W]
    k_slc = _rope(kv[:, 2 * KV_W:3 * KV_W], cos, sin)
    v_slc = kv[:, 3 * KV_W:4 * KV_W]
    k_win = _rope(kv[:, 4 * KV_W:5 * KV_W], cos, sin)
    v_win = kv[:, 5 * KV_W:6 * KV_W]
    rows_ref[:, 0:KV_W] = k_cmp
    rows_ref[:, KV_W:2 * KV_W] = v_cmp
    rows_ref[:, 2 * KV_W:3 * KV_W] = k_slc
    rows_ref[:, 3 * KV_W:4 * KV_W] = v_slc
    win_ref[:, 0:KV_W] = k_win
    win_ref[:, KV_W:2 * KV_W] = v_win
    kvb_ref[:, 0:KV_W] = k_slc.astype(BF16)
    kvb_ref[:, KV_W:2 * KV_W] = v_slc.astype(BF16)
    kvb_ref[:, 2 * KV_W:3 * KV_W] = k_win.astype(BF16)
    kvb_ref[:, 3 * KV_W:4 * KV_W] = v_win.astype(BF16)
    gates_ref[...] = jax.nn.sigmoid(_dot(h, wg_ref[...]))


def _proj(x, g1, sc, sh, w_main, w_gate, cos, sin, conv_w, conv_b, a1=None, a2=None):
    sample = a1 is not None
    r, d = x.shape
    dc = d // 2
    tm = r if sample else 256
    row_spec = lambda wdt: pl.BlockSpec((tm, wdt), lambda i: (i, 0))
    full = lambda a: pl.BlockSpec(a.shape, lambda i: (0,) * a.ndim)
    mod_spec = row_spec(d) if sample else pl.BlockSpec((1, d), lambda i: (0, 0))
    in_specs = [row_spec(d), full(g1), mod_spec, mod_spec, _vmem(), _vmem(), row_spec(LANES), row_spec(LANES),
                full(conv_w), full(conv_b)]
    args = [x, g1, sc, sh, w_main, w_gate, cos, sin, conv_w, conv_b]
    scratch = []
    if sample:
        in_specs += [row_spec(dc), row_spec(dc)]
        args += [a1, a2]
        zc_shape = jax.ShapeDtypeStruct((r, dc), F32)
        zc_spec = row_spec(dc)
    else:
        zc_shape = jax.ShapeDtypeStruct((8, dc), F32)
        zc_spec = pl.BlockSpec((8, dc), lambda i: (0, 0))
        scratch = [pltpu.VMEM((8, dc), F32)]
    out_shape = (jax.ShapeDtypeStruct((r, dc), BF16), jax.ShapeDtypeStruct((r, D_ATTN), BF16),
                 jax.ShapeDtypeStruct((r, 4 * KV_W), F32), jax.ShapeDtypeStruct((r, 2 * KV_W), F32),
                 jax.ShapeDtypeStruct((r, 4 * KV_W), BF16), jax.ShapeDtypeStruct((r, LANES), F32), zc_shape)
    out_specs = (row_spec(dc), row_spec(D_ATTN), row_spec(4 * KV_W), row_spec(2 * KV_W),
                 row_spec(4 * KV_W), row_spec(LANES), zc_spec)
    return pl.pallas_call(
        functools.partial(_proj_kernel, sample=sample, tm=tm, dc=dc),
        grid=(r // tm,),
        in_specs=in_specs, out_specs=out_specs, out_shape=out_shape, scratch_shapes=scratch,
        compiler_params=_cparams("arbitrary"),
        name="proj_sample" if sample else "proj_prompt",
    )(*args)


def _cmp_kernel(pt_ref, *refs, transposed):
    npg = PAGES_PER_STEP
    pages = refs[:npg]
    (perm_ref, w1a_ref, w1b_ref, pea_ref, peb_ref, b1_ref, w2_ref, b2_ref, cos_ref, sin_ref,
     kc_ref, vc_ref, xa_s, xb_s, hprev_s) = refs[npg:]
    del pt_ref
    nch = PAGE // CMP_STRIDE
    rows = npg * nch

    @pl.when(pl.program_id(1) == 0)
    def _():
        hprev_s[...] = jnp.zeros_like(hprev_s)

    lane = lax.broadcasted_iota(jnp.int32, (nch, 2 * KV_W), 1)
    low = (lane % LANES) < HEAD_DIM
    perm = perm_ref[...]
    for k in range(npg):
        page = pages[k][...]
        for pe_ref, x_s in ((pea_ref, xa_s), (peb_ref, xb_s)):
            src = (page + pe_ref[...]).astype(BF16)
            y = _dot_nt(perm, src) if transposed else _dot(perm, src)
            for p in range(CMP_STRIDE // 2):
                ev = y[2 * p * nch:(2 * p + 1) * nch]
                od = y[(2 * p + 1) * nch:(2 * p + 2) * nch]
                a = jnp.where(low, ev, pltpu.roll(od, HEAD_DIM, 1))
                b = jnp.where(low, pltpu.roll(ev, 2 * KV_W - HEAD_DIM, 1), od)
                for j in range(4):
                    kv, ge = j // 2, 2 * (j % 2)
                    rsl = slice(nch * k, nch * (k + 1))
                    csl = slice(LANES * p, LANES * (p + 1))
                    x_s[kv, ge, rsl, csl] = a[:, LANES * j:LANES * (j + 1)]
                    x_s[kv, ge + 1, rsl, csl] = b[:, LANES * j:LANES * (j + 1)]

    row = lax.broadcasted_iota(jnp.int32, (N_KV * rows, 1), 0)
    for kv in range(2):
        ha = _dot(xa_s[kv].reshape(N_KV * rows, CMP_STRIDE * HEAD_DIM).astype(BF16), w1a_ref[kv])
        hb = _dot(xb_s[kv].reshape(N_KV * rows, CMP_STRIDE * HEAD_DIM).astype(BF16), w1b_ref[kv])
        hp = hprev_s[kv]
        prev = jnp.where(row % rows == 0, pltpu.roll(hp, N_KV * rows - (rows - 1), 0), pltpu.roll(ha, 1, 0))
        hprev_s[kv] = ha
        hid = jax.nn.gelu(prev + hb + b1_ref[kv])
        hcat = jnp.concatenate([hid[rows * g:rows * (g + 1)] for g in range(N_KV)], axis=1).astype(BF16)
        o = _dot(hcat, w2_ref[kv]) + b2_ref[kv]
        if kv == 0:
            kc_ref[...] = _rope(o, cos_ref[...], sin_ref[...]).astype(BF16)
        else:
            vc_ref[...] = o.astype(BF16)


def _compress(pool, page_table, cw, transposed):
    b, n_pages = page_table.shape
    npg = PAGES_PER_STEP
    nsteps = n_pages // npg
    rows = npg * (PAGE // CMP_STRIDE)
    nc = nsteps * rows
    cpos = CMP_STRIDE * jnp.arange(nc) + (CMP_STRIDE - 1)
    cos, sin = _rope_tables(cpos)
    page_block = (None, 2 * KV_W, PAGE) if transposed else (None, PAGE, 2 * KV_W)

    def page_spec(k):
        return pl.BlockSpec(page_block, lambda bi, i, pt: (pt[bi, i * npg + k], 0, 0))

    const = lambda a: pl.BlockSpec(a.shape, lambda bi, i, pt: (0,) * a.ndim)
    nch = PAGE // CMP_STRIDE
    perm = np.zeros((PAGE, PAGE), np.float32)
    for s in range(CMP_STRIDE):
        for c in range(nch):
            perm[s * nch + c, CMP_STRIDE * c + s] = 1.0
    pea, peb = (cw["pea"].T, cw["peb"].T) if transposed else (cw["pea"], cw["peb"])
    weights = [jnp.asarray(perm, BF16), cw["w1a"], cw["w1b"], pea, peb, cw["b1"], cw["w2bd"], cw["b2"]]
    tab_spec = pl.BlockSpec((rows, LANES), lambda bi, i, pt: (i, 0))
    out_spec = pl.BlockSpec((None, rows, KV_W), lambda bi, i, pt: (bi, i, 0))
    x_scratch = pltpu.VMEM((2, N_KV, rows, CMP_STRIDE * HEAD_DIM), F32)
    grid_spec = pltpu.PrefetchScalarGridSpec(
        num_scalar_prefetch=1,
        grid=(b, nsteps),
        in_specs=[page_spec(k) for k in range(npg)] + [const(a) for a in weights] + [tab_spec, tab_spec],
        out_specs=(out_spec, out_spec),
        scratch_shapes=[x_scratch, x_scratch, pltpu.VMEM((2, N_KV * rows, 2 * HEAD_DIM), F32)],
    )
    return pl.pallas_call(
        functools.partial(_cmp_kernel, transposed=transposed),
        grid_spec=grid_spec,
        out_shape=(jax.ShapeDtypeStruct((b, nc, KV_W), BF16), jax.ShapeDtypeStruct((b, nc, KV_W), BF16)),
        compiler_params=_cparams("arbitrary", "arbitrary"),
        name="compress",
    )(page_table, *([pool] * npg), *weights, cos, sin)


def _prep_cmp_weights(cmp_pe, cmp_w1, cmp_b1, cmp_w2, cmp_b2):
    s = CMP_STRIDE
    w2bd = jnp.zeros((2, N_KV * 2 * HEAD_DIM, KV_W), F32)
    for g in range(N_KV):
        w2bd = w2bd.at[:, g * 2 * HEAD_DIM:(g + 1) * 2 * HEAD_DIM, g * HEAD_DIM:(g + 1) * HEAD_DIM].set(cmp_w2)
    def pe_page(pe_half):
        t = jnp.broadcast_to(pe_half.transpose(1, 0, 2)[:, :, None, :], (s, 2, N_KV, HEAD_DIM))
        return jnp.tile(t.reshape(s, 2 * KV_W), (PAGE // s, 1))

    return {
        "w1a": cmp_w1[:, :s].reshape(2, s * HEAD_DIM, 2 * HEAD_DIM).astype(BF16),
        "w1b": cmp_w1[:, s:].reshape(2, s * HEAD_DIM, 2 * HEAD_DIM).astype(BF16),
        "pea": pe_page(cmp_pe[:, :s]),
        "peb": pe_page(cmp_pe[:, s:]),
        "b1": cmp_b1.reshape(2, 1, 2 * HEAD_DIM),
        "w2bd": w2bd.astype(BF16),
        "b2": jnp.tile(cmp_b2, (1, N_KV)).reshape(2, 1, KV_W),
    }


def _cmp_to_slc(nc, ns):
    j = np.arange(nc)[:, None] - 1
    b = np.arange(ns)[None, :]
    ov = (CMP_STRIDE * j < SLC_BLOCK * (b + 1)) & (CMP_STRIDE * j + 2 * CMP_STRIDE > SLC_BLOCK * b) & (j >= 0)
    return ov.astype(np.float32)


def _masked_softmax(s, mask):
    s = jnp.where(mask, s, -jnp.inf)
    m = jnp.max(s, axis=-1, keepdims=True)
    m = jnp.where(m > -jnp.inf, m, 0.0)
    p = jnp.exp(s - m)
    return p * (1.0 / jnp.maximum(jnp.sum(p, axis=-1, keepdims=True), 1e-30))


def _select_blocks(score, n_top):
    lane = lax.broadcasted_iota(jnp.int32, score.shape, 1)
    nb = score.shape[1]
    sel = jnp.zeros(score.shape, F32)
    cur = score
    for _ in range(n_top):
        mx = jnp.max(cur, axis=-1, keepdims=True)
        first = jnp.min(jnp.where(cur == mx, lane, nb), axis=-1, keepdims=True)
        hit = (lane == first) & (mx > -jnp.inf)
        sel = jnp.where(hit, 1.0, sel)
        cur = jnp.where(lane == first, -jnp.inf, cur)
    return sel


def _masked_softmax0(s, mask):
    s = jnp.where(mask, s, -jnp.inf)
    m = jnp.max(s, axis=0, keepdims=True)
    m = jnp.where(m > -jnp.inf, m, 0.0)
    p = jnp.exp(s - m)
    return p * (1.0 / jnp.maximum(jnp.sum(p, axis=0, keepdims=True), 1e-30))


def _select_blocks_t(score, n_top):
    rowi = lax.broadcasted_iota(jnp.int32, score.shape, 0)
    nb = score.shape[0]
    sel = jnp.zeros(score.shape, F32)
    cur = score
    for _ in range(n_top):
        mx = jnp.max(cur, axis=0, keepdims=True)
        first = jnp.min(jnp.where(cur == mx, rowi, nb), axis=0, keepdims=True)
        is_first = rowi == first
        sel = jnp.where(is_first & (mx > -jnp.inf), 1.0, sel)
        cur = jnp.where(is_first, -jnp.inf, cur)
    return sel


def _own_blocks(x, tq):
    rq = GROUP * tq
    return jnp.concatenate([x[g * HEAD_DIM:(g + 1) * HEAD_DIM, g * rq:(g + 1) * rq] for g in range(N_KV)], axis=0)


def _nsa_prompt_kernel(qt_ref, gt_ref, kc_ref, vct_ref, ks_ref, vst_ref, kw_ref, vwt_ref, mt_ref,
                       out_ref, qm_s, bias_s, acc_s, m_s, l_s, sa_s, sb_s, ma_s, mb_s, *, nc, ns):
    i = pl.program_id(0)
    tq = Q_BLOCK
    rq = GROUP * tq
    ncol = N_KV * rq
    colq = i * tq + lax.broadcasted_iota(jnp.int32, (1, rq), 1) % tq
    colq_g = i * tq + lax.broadcasted_iota(jnp.int32, (1, N_KV * tq), 1) % tq
    grows = [slice(g * HEAD_DIM, (g + 1) * HEAD_DIM) for g in range(N_KV)]
    gcols = [slice(g * rq, (g + 1) * rq) for g in range(N_KV)]

    @pl.when(i == 0)
    def _():
        qm_s[...] = jnp.zeros_like(qm_s)

    for g in range(N_KV):
        for r in range(GROUP):
            hd = (g * GROUP + r) * HEAD_DIM
            qm_s[grows[g], (g * GROUP + r) * tq:(g * GROUP + r + 1) * tq] = qt_ref[hd:hd + HEAD_DIM, :]

    crow = lax.broadcasted_iota(jnp.int32, (nc, 1), 0)
    cvalid = (crow >= 1) & (CMP_STRIDE * crow + (CMP_STRIDE - 1) <= colq)
    kc = kc_ref[...]
    o_c, psum = [], []
    scores = [_dot(kc, qm_s[:, gcols[g]]) for g in range(N_KV)]
    for g in range(N_KV):
        p = _masked_softmax0(scores[g], cvalid)
        o_c.append(_dot(vct_ref[grows[g], :], p.astype(BF16)))
        psum.append(sum(p[:, r * tq:(r + 1) * tq] for r in range(GROUP)))
    o_c = jnp.concatenate(o_c, axis=0)
    psum = jnp.concatenate(psum, axis=1)
    hi = psum.astype(BF16)
    lo = (psum - hi.astype(F32)).astype(BF16)
    imp = _dot(mt_ref[...], hi) + _dot(mt_ref[...], lo)
    blk = lax.broadcasted_iota(jnp.int32, (ns, 1), 0)
    cur = colq_g // SLC_BLOCK
    forced = (blk == 0) | (blk == cur) | (blk == cur - 1)
    score = jnp.where(blk <= cur, jnp.where(forced, FORCE_SCORE, imp), -jnp.inf)
    bias = (_select_blocks_t(score, min(TOP_N, ns)) - 1.0) * (-NEG_BIG)
    for g in range(N_KV):
        for r in range(GROUP):
            c0 = (g * GROUP + r) * tq
            bias_s[:, c0:c0 + tq] = bias[:, g * tq:(g + 1) * tq]

    w0 = pl.multiple_of(i * tq, tq)
    kw = kw_ref[pl.ds(w0, WINDOW + tq), :]
    nwt = (WINDOW + tq) // tq
    vwt = vwt_ref[pl.ds(i, nwt)]
    wpos = i * tq - WINDOW + lax.broadcasted_iota(jnp.int32, (WINDOW + tq, 1), 0)
    dist = colq - wpos
    wmask = (dist >= 0) & (dist < WINDOW) & (wpos >= 0)
    o_w = []
    scores = [_dot(kw, qm_s[:, gcols[g]]) for g in range(N_KV)]
    for g in range(N_KV):
        s = jnp.where(wmask, scores[g], -jnp.inf)
        p = jnp.exp2(((s - jnp.max(s, axis=0, keepdims=True)) * LOG2E).astype(BF16))
        r = _dot(jnp.concatenate([vwt[j, g] for j in range(nwt)], axis=1), p)
        o_w.append(r[0:HEAD_DIM] * (1.0 / r[HEAD_DIM:HEAD_DIM + 1]))
    o_w = jnp.concatenate(o_w, axis=0)

    acc_s[...] = jnp.zeros_like(acc_s)
    m_s[...] = jnp.full(m_s.shape, NEG_BIG, F32)
    l_s[...] = jnp.zeros_like(l_s)
    blocks_per_tile = KEY_TILE // SLC_BLOCK
    krow = lax.broadcasted_iota(jnp.int32, (KEY_TILE, 1), 0)

    def scores(kt, buf, mx, causal):
        k0 = pl.multiple_of(kt * KEY_TILE, KEY_TILE)
        kk = ks_ref[pl.ds(k0, KEY_TILE), :]
        raw = [_dot(kk, qm_s[:, gcols[g]]) for g in range(N_KV)]
        for g in range(N_KV):
            s = jnp.concatenate(
                [raw[g][j * SLC_BLOCK:(j + 1) * SLC_BLOCK] + bias_s[pl.ds(kt * blocks_per_tile + j, 1), gcols[g]]
                 for j in range(blocks_per_tile)], axis=0)
            if causal:
                s = jnp.where(k0 + krow <= colq, s, NEG_BIG)
            buf[g] = s
            mx[:, gcols[g]] = jnp.max(s, axis=0, keepdims=True)

    def tile(kt, buf, mx):
        va = vst_ref[kt]
        probs, alphas = [], []
        for g in range(N_KV):
            m_old = m_s[:, gcols[g]]
            m_new = jnp.maximum(m_old, mx[:, gcols[g]])
            m_s[:, gcols[g]] = m_new
            alphas.append(jnp.exp(m_old - m_new))
            probs.append(jnp.exp2(((buf[g] - m_new) * LOG2E).astype(BF16)))
        for g in range(N_KV):
            r = _dot(va[g], probs[g])
            acc_s[grows[g], :] = acc_s[grows[g], :] * alphas[g] + r[0:HEAD_DIM]
            l_s[:, gcols[g]] = alphas[g] * l_s[:, gcols[g]] + r[HEAD_DIM:HEAD_DIM + 1]

    last = (i * tq) // KEY_TILE

    scores(0, sa_s, ma_s, False)

    def body(j, carry):
        scores(2 * j + 1, sb_s, mb_s, False)
        tile(2 * j, sa_s, ma_s)
        scores(2 * j + 2, sa_s, ma_s, False)
        tile(2 * j + 1, sb_s, mb_s)
        return carry

    lax.fori_loop(0, last // 2, body, 0)

    @pl.when(last % 2 == 0)
    def _():
        scores(last, sa_s, ma_s, True)
        tile(last, sa_s, ma_s)

    @pl.when(last % 2 == 1)
    def _():
        scores(last, sb_s, mb_s, True)
        tile(last - 1, sa_s, ma_s)
        tile(last, sb_s, mb_s)

    rl = 1.0 / l_s[...]
    o_s = acc_s[...] * jnp.concatenate(
        [jnp.broadcast_to(rl[:, g * rq:(g + 1) * rq], (HEAD_DIM, rq)) for g in range(N_KV)], axis=0)

    gt = gt_ref[...]

    def gate(br):
        return jnp.concatenate(
            [jnp.concatenate(
                [jnp.broadcast_to(gt[(GROUP * g + r) * 3 + br:(GROUP * g + r) * 3 + br + 1, :], (HEAD_DIM, tq))
                 for r in range(GROUP)], axis=1) for g in range(N_KV)], axis=0)

    o = gate(0) * o_c + gate(1) * o_s + gate(2) * o_w
    for r in range(GROUP):
        out_ref[:, r * KV_W:(r + 1) * KV_W] = o[:, r * tq:(r + 1) * tq].T.astype(BF16)


def _nsa_prompt(q, gates, kc, vc, kvb):
    t = q.shape[0]
    nc = kc.shape[0]
    ns = t // SLC_BLOCK
    tq = Q_BLOCK
    qt = q.T
    gt = gates.T
    vct = vc.T
    ks = kvb[:, 0:KV_W]
    vst = kvb[:, KV_W:2 * KV_W].reshape(t // KEY_TILE, KEY_TILE, N_KV, HEAD_DIM).transpose(0, 2, 3, 1)
    vst = jnp.concatenate([vst, jnp.ones((t // KEY_TILE, N_KV, 16, KEY_TILE), BF16)], axis=2)
    kw = jnp.pad(kvb[:, 2 * KV_W:3 * KV_W], ((WINDOW, 0), (0, 0)))
    vwt = jnp.pad(kvb[:, 3 * KV_W:4 * KV_W], ((WINDOW, 0), (0, 0)))
    vwt = vwt.reshape((t + WINDOW) // tq, tq, N_KV, HEAD_DIM).transpose(0, 2, 3, 1)
    vwt = jnp.concatenate([vwt, jnp.ones(((t + WINDOW) // tq, N_KV, 16, tq), BF16)], axis=2)
    mt = jnp.asarray(_cmp_to_slc(nc, ns).T, BF16)
    full = lambda a: pl.BlockSpec(a.shape, lambda i: (0,) * a.ndim)
    ncol = N_KV * GROUP * tq
    return pl.pallas_call(
        functools.partial(_nsa_prompt_kernel, nc=nc, ns=ns),
        grid=(t // tq,),
        in_specs=[pl.BlockSpec((D_ATTN, tq), lambda i: (0, i)),
                  pl.BlockSpec((LANES, tq), lambda i: (0, i)),
                  full(kc), full(vct), _vmem(), _vmem(), _vmem(), _vmem(), full(mt)],
        out_specs=pl.BlockSpec((tq, D_ATTN), lambda i: (i, 0)),
        out_shape=jax.ShapeDtypeStruct((t, D_ATTN), BF16),
        scratch_shapes=[pltpu.VMEM((KV_W, ncol), BF16),
                        pltpu.VMEM((ns, ncol), F32),
                        pltpu.VMEM((KV_W, GROUP * tq), F32),
                        pltpu.VMEM((1, ncol), F32),
                        pltpu.VMEM((1, ncol), F32),
                        pltpu.VMEM((N_KV, KEY_TILE, GROUP * tq), F32),
                        pltpu.VMEM((N_KV, KEY_TILE, GROUP * tq), F32),
                        pltpu.VMEM((1, ncol), F32),
                        pltpu.VMEM((1, ncol), F32)],
        compiler_params=_cparams("arbitrary"),
        name="nsa_prompt",
    )(qt, gt, kc, vct, ks, vst, kw, vwt, mt)


def _nsa_sample_kernel(pt_ref, *refs, nc, ns, nsteps):
    npg = PAGES_PER_STEP
    pages = refs[:npg]
    (qrep_ref, gx_ref, kc_ref, vc_ref, sw_ref, new_ref, m_ref, e_ref, en_ref,
     out_ref, qall_s, sel_s, oc_s, ow_s, m_s, l_s, acc_s) = refs[npg:]
    del pt_ref
    i = pl.program_id(1)
    nr = GROUP * N_KV * 4
    row = lax.broadcasted_iota(jnp.int32, (nr, 1), 0)
    row_t = row % 4
    row_g = (row // 4) % N_KV
    lane_g = lax.broadcasted_iota(jnp.int32, (1, KV_W), 1) // HEAD_DIM
    own = row_g == lane_g
    new = new_ref[...]
    ucol = lax.broadcasted_iota(jnp.int32, (1, 8), 1)
    new_ok = (ucol <= row_t) & (ucol < 4)

    @pl.when(i == 0)
    def _():
        qall = jnp.where(own, qrep_ref[...], 0).astype(BF16)
        qall_s[...] = qall
        cidx = lax.broadcasted_iota(jnp.int32, (1, nc), 1)
        p = _masked_softmax(_dot_nt(qall, kc_ref[...]), cidx >= 1)
        oc_s[...] = _dot(p.astype(BF16), vc_ref[...])
        n16 = N_KV * 4
        psum = p[0:n16] + p[n16:2 * n16] + p[2 * n16:3 * n16] + p[3 * n16:4 * n16]
        imp = _split_dot(psum, m_ref[...])
        nb = imp.shape[1]
        blk = lax.broadcasted_iota(jnp.int32, (1, nb), 1)
        cur = ns - 1
        forced = (blk == 0) | (blk == cur) | (blk == cur - 1)
        score = jnp.where(blk <= cur, jnp.where(forced, FORCE_SCORE, imp), -jnp.inf)
        sel_s[...] = _select_blocks(score, min(TOP_N, ns)).astype(BF16)
        kw_t = sw_ref[0:KV_W, :].astype(BF16)
        vw_t = sw_ref[KV_W:2 * KV_W, :].astype(BF16)
        wcol = lax.broadcasted_iota(jnp.int32, (1, kw_t.shape[1]), 1)
        s1 = jnp.where(wcol > row_t, _dot(qall, kw_t), -jnp.inf)
        s2 = jnp.where(new_ok, _dot_nt(qall, new[:, 2 * KV_W:3 * KV_W]), -jnp.inf)
        mw = jnp.maximum(jnp.max(s1, axis=-1, keepdims=True), jnp.max(s2, axis=-1, keepdims=True))
        p1 = jnp.exp(s1 - mw)
        p2 = jnp.exp(s2 - mw)
        den = jnp.maximum(jnp.sum(p1, axis=-1, keepdims=True) + jnp.sum(p2, axis=-1, keepdims=True), 1e-30)
        ow_s[...] = (_dot_nt((p1 / den).astype(BF16), vw_t)
                     + _dot((p2 / den).astype(BF16), new[:, 3 * KV_W:4 * KV_W]))
        m_s[...] = jnp.full(m_s.shape, NEG_BIG, F32)
        l_s[...] = jnp.zeros_like(l_s)
        acc_s[...] = jnp.zeros_like(acc_s)

    qall = qall_s[...]
    kk_t = jnp.concatenate([pages[k][0:KV_W, :] for k in range(npg)], axis=1).astype(BF16)
    vv_t = jnp.concatenate([pages[k][KV_W:2 * KV_W, :] for k in range(npg)], axis=1).astype(BF16)
    mk = _dot(sel_s[...], e_ref[i]) > 0.5
    mk = jnp.concatenate([mk] * GROUP, axis=0)
    s = jnp.where(mk, _dot(qall, kk_t), NEG_BIG)
    m_old = m_s[...]
    m_new = jnp.maximum(m_old, jnp.max(s, axis=-1, keepdims=True))
    alpha = jnp.exp(m_old - m_new)
    p = jnp.exp(s - m_new)
    l_new = alpha * l_s[...] + jnp.sum(p, axis=-1, keepdims=True)
    acc_new = alpha * acc_s[...] + _dot_nt(p.astype(BF16), vv_t)
    m_s[...] = m_new
    l_s[...] = l_new
    acc_s[...] = acc_new

    @pl.when(i == nsteps - 1)
    def _():
        mkn = _dot(sel_s[...], en_ref[...]) > 0.5
        mkn = jnp.concatenate([mkn] * GROUP, axis=0) & new_ok
        sn = jnp.where(mkn, _dot_nt(qall, new[:, 0:KV_W]), NEG_BIG)
        m_f = jnp.maximum(m_new, jnp.max(sn, axis=-1, keepdims=True))
        al = jnp.exp(m_new - m_f)
        pn = jnp.exp(sn - m_f)
        l_f = al * l_new + jnp.sum(pn, axis=-1, keepdims=True)
        acc_f = al * acc_new + _dot(pn.astype(BF16), new[:, KV_W:2 * KV_W])
        o_s = acc_f / l_f
        out_ref[...] = gx_ref[0] * oc_s[...] + gx_ref[1] * o_s + gx_ref[2] * ow_s[...]


def _nsa_sample(pool, page_table, qrep, gx, kc, vc, sw, new8):
    b, n_pages = page_table.shape
    npg = PAGES_PER_STEP
    nsteps = n_pages // npg
    nc = kc.shape[1]
    past = n_pages * PAGE
    ns = past // SLC_BLOCK + 1
    nb = -(-ns // LANES) * LANES
    keys_step = npg * PAGE
    m = np.zeros((nc, nb), np.float32)
    m[:, :ns] = _cmp_to_slc(nc, ns)
    e = np.zeros((nsteps, nb, keys_step), np.float32)
    for st in range(nsteps):
        for j in range(keys_step):
            e[st, (st * keys_step + j) // SLC_BLOCK, j] = 1.0
    en = np.zeros((nb, 8), np.float32)
    en[ns - 1, :] = 1.0
    m, e, en = jnp.asarray(m, BF16), jnp.asarray(e, BF16), jnp.asarray(en, BF16)
    nr = GROUP * N_KV * 4

    def page_spec(k):
        return pl.BlockSpec((None, 2 * KV_W, PAGE), lambda bi, i, pt: (pt[bi, i * npg + k], 1, 0))

    per_b = lambda a: pl.BlockSpec((None,) + a.shape[1:], lambda bi, i, pt: (bi,) + (0,) * (a.ndim - 1))
    const = lambda a: pl.BlockSpec(a.shape, lambda bi, i, pt: (0,) * a.ndim)
    grid_spec = pltpu.PrefetchScalarGridSpec(
        num_scalar_prefetch=1,
        grid=(b, nsteps),
        in_specs=[page_spec(k) for k in range(npg)]
        + [per_b(qrep), per_b(gx), per_b(kc), per_b(vc), per_b(sw), per_b(new8), const(m), _vmem(), const(en)],
        out_specs=pl.BlockSpec((None, nr, KV_W), lambda bi, i, pt: (bi, 0, 0)),
        scratch_shapes=[pltpu.VMEM((nr, KV_W), BF16), pltpu.VMEM((N_KV * 4, nb), BF16),
                        pltpu.VMEM((nr, KV_W), F32), pltpu.VMEM((nr, KV_W), F32),
                        pltpu.VMEM((nr, 1), F32), pltpu.VMEM((nr, 1), F32), pltpu.VMEM((nr, KV_W), F32)],
    )
    return pl.pallas_call(
        functools.partial(_nsa_sample_kernel, nc=nc, ns=ns, nsteps=nsteps),
        grid_spec=grid_spec,
        out_shape=jax.ShapeDtypeStruct((b, nr, KV_W), F32),
        compiler_params=_cparams("arbitrary", "arbitrary"),
        name="nsa_sample",
    )(page_table, *([pool] * npg), qrep, gx, kc, vc, sw, new8, m, e, en)


def _outproj_kernel(x_ref, cy_ref, at_ref, ga1_ref, g2_ref, sc2_ref, sh2_ref, wo_ref, wq_ref, keys_ref,
                    x1_ref, h2_ref, s_ref, *, dc):
    u = _dot(cy_ref[...], wo_ref[0:dc, :]) + _dot(at_ref[...], wo_ref[dc:, :])
    x1 = x_ref[...] + ga1_ref[...] * u
    x1_ref[...] = x1
    h2 = _rms_mod(x1, g2_ref[...], sc2_ref[...], sh2_ref[...]).astype(BF16)
    h2_ref[...] = h2
    qp = _dot(h2, wq_ref[...]).astype(BF16)
    for hc in range(2 * PEER_HEADS):
        s_ref[:, hc * N_KEYS:(hc + 1) * N_KEYS] = _dot_nt(qp[:, hc * LANES:(hc + 1) * LANES], keys_ref[hc])


def _outproj(x, convy, attn, ga1, g2, sc2, sh2, wo, wq, keys, per_row):
    r, d = x.shape
    dc = d // 2
    tm = r if per_row else 256
    row_spec = lambda wdt: pl.BlockSpec((tm, wdt), lambda i: (i, 0))
    mod_spec = row_spec(d) if per_row else pl.BlockSpec((1, d), lambda i: (0, 0))
    nsc = keys.shape[0] * N_KEYS
    return pl.pallas_call(
        functools.partial(_outproj_kernel, dc=dc),
        grid=(r // tm,),
        in_specs=[row_spec(d), row_spec(dc), row_spec(d - dc), mod_spec,
                  pl.BlockSpec((1, d), lambda i: (0, 0)), mod_spec, mod_spec, _vmem(), _vmem(), _vmem()],
        out_specs=(row_spec(d), row_spec(d), row_spec(nsc)),
        out_shape=(jax.ShapeDtypeStruct((r, d), F32), jax.ShapeDtypeStruct((r, d), BF16),
                   jax.ShapeDtypeStruct((r, nsc), F32)),
        compiler_params=_cparams("arbitrary"),
        name="outproj_sample" if per_row else "outproj_prompt",
    )(x, convy, attn, ga1, g2, sc2, sh2, wo, wq, keys)


def _sort_network(n):
    pairs = []

    def merge(lo, cnt, r):
        step = r * 2
        if step < cnt:
            merge(lo, cnt, step)
            merge(lo + r, cnt, step)
            for i in range(lo + r, lo + cnt - r, step):
                pairs.append((i, i + r))
        else:
            pairs.append((lo, lo + r))

    def sort(lo, cnt):
        if cnt > 1:
            half = cnt // 2
            sort(lo, half)
            sort(lo + half, half)
            merge(lo, cnt, 1)

    sort(0, n)
    return pairs


def _prune_network(pairs, wanted):
    need = set(wanted)
    keep = []
    for i, j in reversed(pairs):
        if i in need or j in need:
            keep.append((i, j))
            need.update((i, j))
    return keep[::-1]


def _apply_network(x, pairs):
    x = list(x)
    for i, j in pairs:
        x[i], x[j] = jnp.maximum(x[i], x[j]), jnp.minimum(x[i], x[j])
    return x


_SORT16 = _sort_network(PEER_TOPK)
_CAND_PAIRS = [(a, b) for a in range(PEER_TOPK + 1) for b in range(PEER_TOPK + 1)
               if (a + 1) * (b + 1) <= PEER_TOPK + 1]
_CAND_WIRES = 64
_SELECT_16_17 = _prune_network(_sort_network(_CAND_WIRES), (PEER_TOPK - 1, PEER_TOPK))


def _top17(s):
    k = PEER_TOPK
    sub = 8
    lst = _apply_network([s[sub * v:sub * (v + 1)] for v in range(k)], _SORT16)
    for dist in (1, 2, 4):
        c = [jnp.maximum(lst[v], pltpu.roll(lst[k - 1 - v], dist, 0)) for v in range(k)]
        d = k // 2
        while d >= 1:
            for i in range(k):
                if i & d == 0:
                    c[i], c[i + d] = jnp.maximum(c[i], c[i + d]), jnp.minimum(c[i], c[i + d])
            d //= 2
        lst = c
    top = [v[sub - 1:sub, :] for v in lst]
    v16 = top[k - 1]
    ge = s >= v16
    cnt = jnp.sum(jnp.where(ge, 1.0, 0.0), axis=0, keepdims=True)
    below = jnp.max(jnp.where(ge, -jnp.inf, s), axis=0, keepdims=True)
    top.append(jnp.where(cnt > k, v16, below))
    return top


def _peer_select_kernel(s_ref, a2_ref, e2_ref, thr_ref, e1_ref):
    k = PEER_TOPK
    nh = PEER_HEADS
    half = lambda h, c: s_ref[(2 * h + c) * N_KEYS:(2 * h + c + 1) * N_KEYS, :]
    tops1 = [_top17(half(h, 0)) for h in range(nh)]
    tops2 = [_top17(half(h, 1)) for h in range(nh)]
    v1 = [jnp.concatenate([tops1[h][a] for h in range(nh)], axis=0) for a in range(k + 1)]
    v2 = [jnp.concatenate([tops2[h][a] for h in range(nh)], axis=0) for a in range(k + 1)]
    d1 = [v - v1[0] for v in v1]
    d2 = [v - v2[0] for v in v2]
    cands = [d1[a] + d2[b] for a, b in _CAND_PAIRS]
    pad = [jnp.full_like(cands[0], -jnp.inf)] * (_CAND_WIRES - len(cands))
    srt = _apply_network(cands + pad, _SELECT_16_17)
    tau = 0.5 * (srt[k - 1] + srt[k])
    x1 = [jnp.exp(d) for d in d1]
    x2 = [jnp.exp(d) for d in d2]
    z = jnp.zeros_like(tau)
    for (a, b), c in zip(_CAND_PAIRS, cands):
        z = z + jnp.where(c >= tau, x1[a] * x2[b], 0.0)
    rz = 1.0 / z
    ic = EXPERT_ROWS_PER_STEP
    for h in range(nh):
        a1 = half(h, 0) - v1[0][h:h + 1]
        a2 = half(h, 1) - v2[0][h:h + 1]
        thr = tau[h:h + 1] - a1
        e1 = jnp.exp(a1) * rz[h:h + 1]
        for c in range(N_KEYS // ic):
            thr_ref[c, h * ic:(h + 1) * ic, :] = thr[c * ic:(c + 1) * ic]
            e1_ref[c, h * ic:(h + 1) * ic, :] = e1[c * ic:(c + 1) * ic]
        a2_ref[h * N_KEYS:(h + 1) * N_KEYS, :] = a2
        e2_ref[h * N_KEYS:(h + 1) * N_KEYS, :] = jnp.exp(a2)


def _peer_select(s_t):
    nrow, tp = s_t.shape
    assert N_KEYS == 8 * PEER_TOPK
    tn = LANES
    hk = PEER_HEADS * N_KEYS
    ic = EXPERT_ROWS_PER_STEP
    flat = pl.BlockSpec((None, hk, tn), lambda t: (t, 0, 0))
    cube = pl.BlockSpec((None, N_KEYS // ic, PEER_HEADS * ic, tn), lambda t: (t, 0, 0, 0))
    return pl.pallas_call(
        _peer_select_kernel,
        grid=(tp // tn,),
        in_specs=[pl.BlockSpec((nrow, tn), lambda t: (0, t))],
        out_specs=(flat, flat, cube, cube),
        out_shape=(jax.ShapeDtypeStruct((tp // tn, hk, tn), F32),) * 2
        + (jax.ShapeDtypeStruct((tp // tn, N_KEYS // ic, PEER_HEADS * ic, tn), F32),) * 2,
        compiler_params=_cparams("arbitrary"),
        name="peer_select",
    )(s_t)


EXPERT_ROWS_PER_STEP = 8
PEER_TOKEN_TILE = 768


def _peer_dense_kernel(h_ref, u_ref, vt_ref, thr_ref, e1_ref, a2_ref, e2_ref, out_ref, gate_s, wg_s, acc_s):
    c = pl.program_id(1)

    @pl.when(c == 0)
    def _():
        acc_s[...] = jnp.zeros_like(acc_s)

    ic = EXPERT_ROWS_PER_STEP
    for ii in range(ic):
        es = slice(ii * N_KEYS, (ii + 1) * N_KEYS)
        for lt in range(h_ref.shape[1] // LANES):
            w = None
            for h in range(PEER_HEADS):
                hs = slice(h * N_KEYS, (h + 1) * N_KEYS)
                row = slice(h * ic + ii, h * ic + ii + 1)
                keep = a2_ref[lt, hs, :] >= thr_ref[lt, row, :]
                term = jnp.where(keep, e1_ref[lt, row, :] * e2_ref[lt, hs, :], 0.0)
                w = term if w is None else w + term
            gate_s[es, lt * LANES:(lt + 1) * LANES] = w
    halves = [slice(k * (ic // 2) * N_KEYS, (k + 1) * (ic // 2) * N_KEYS) for k in range(2)]
    acts = [_dot(u_ref[hf, :], h_ref[...]) for hf in halves]
    for k, hf in enumerate(halves):
        wg_s[hf, :] = (gate_s[hf, :] * jax.nn.gelu(acts[k])).astype(BF16)
    acc_s[...] += _dot(vt_ref[:, halves[0]], wg_s[halves[0], :]) + _dot(vt_ref[:, halves[1]], wg_s[halves[1], :])

    @pl.when(c == pl.num_programs(1) - 1)
    def _():
        out_ref[...] = acc_s[...].T


def _peer_dense(h_t, u_bf, vt_bf, thr, e1, a2, e2, tm):
    d, tp = h_t.shape
    ne = u_bf.shape[0]
    ec = EXPERT_ROWS_PER_STEP * N_KEYS
    hk = PEER_HEADS * N_KEYS
    once = pl.Buffered(1)
    nlt = tm // LANES
    slab = pl.BlockSpec((nlt, hk, LANES), lambda t, c: (t, 0, 0), pipeline_mode=once)
    rows = pl.BlockSpec((nlt, None, PEER_HEADS * EXPERT_ROWS_PER_STEP, LANES), lambda t, c: (t, c, 0, 0))
    return pl.pallas_call(
        _peer_dense_kernel,
        grid=(tp // tm, ne // ec),
        in_specs=[pl.BlockSpec((d, tm), lambda t, c: (0, t), pipeline_mode=once),
                  pl.BlockSpec((ec, d), lambda t, c: (c, 0)), pl.BlockSpec((d, ec), lambda t, c: (0, c)),
                  rows, rows, slab, slab],
        out_specs=pl.BlockSpec((tm, d), lambda t, c: (t, 0), pipeline_mode=once),
        out_shape=jax.ShapeDtypeStruct((tp, d), F32),
        scratch_shapes=[pltpu.VMEM((ec, tm), F32), pltpu.VMEM((ec, tm), BF16), pltpu.VMEM((d, tm), F32)],
        compiler_params=_cparams("arbitrary", "arbitrary"),
        name="peer_dense",
    )(h_t, u_bf, vt_bf, thr, e1, a2, e2)


def _final_kernel(x1_ref, f_ref, ga2_ref, fg_ref, y_ref):
    x2 = x1_ref[...] + ga2_ref[...] * f_ref[...]
    y = x2 * lax.rsqrt(jnp.mean(x2 * x2, axis=-1, keepdims=True) + EPS)
    y_ref[...] = y * fg_ref[...]


def _final(x1, f, f_row0, ga2, fg, per_row):
    r, d = x1.shape
    tm = r if per_row else 256
    assert f_row0 % tm == 0
    row_spec = pl.BlockSpec((tm, d), lambda i: (i, 0))
    f_spec = pl.BlockSpec((tm, d), lambda i: (i + f_row0 // tm, 0))
    vec_spec = pl.BlockSpec((1, d), lambda i: (0, 0))
    return pl.pallas_call(
        _final_kernel,
        grid=(r // tm,),
        in_specs=[row_spec, f_spec, row_spec if per_row else vec_spec, vec_spec],
        out_specs=row_spec,
        out_shape=jax.ShapeDtypeStruct((r, d), F32),
        compiler_params=_cparams("arbitrary"),
        name="final_sample" if per_row else "final_prompt",
    )(x1, f, ga2, fg)


def _prep_w_in(w_in):
    d = w_in.shape[0]
    g0 = 3 * (d // 2) + D_ATTN + 6 * KV_W
    w_gate = jnp.pad(w_in[:, g0:], ((0, 0), (0, LANES - 3 * N_HEADS)))
    return w_in[:, :g0].astype(BF16), w_gate.astype(BF16)


def _prep_w_out(w_out):
    d = w_out.shape[0]
    dc = d // 2
    wa = w_out[dc:].reshape(N_KV, GROUP, HEAD_DIM, d).transpose(1, 0, 2, 3).reshape(D_ATTN, d)
    return jnp.concatenate([w_out[:dc], wa], axis=0).astype(BF16)


def kernel(x_prompt, x_sample, c_prompt, c_sample, cache_kv, state_win, state_conv, page_table, w_ada, b_ada,
           norm1_g, norm2_g, w_in, conv_w, conv_b, cmp_pe, cmp_w1, cmp_b1, cmp_w2, cmp_b2, w_out, peer_wq,
           peer_keys, peer_u, peer_v, final_g):
    depth = w_ada.shape[0]
    assert depth == 1
    bp, t, d = x_prompt.shape
    assert bp == 1 and d - d // 2 == D_ATTN
    bs, ts, _ = x_sample.shape
    assert ts == 4
    dc = d // 2
    n_pages = page_table.shape[1]
    past = n_pages * PAGE
    wkeep = state_win.shape[2]
    assert wkeep == WINDOW and t % KEY_TILE == 0 and n_pages % PAGES_PER_STEP == 0
    rs = bs * ts

    n_c = 1 + bs
    n_cp = -(-n_c // 8) * 8
    c_all = jnp.concatenate([c_prompt, c_sample, jnp.zeros((n_cp - n_c, d), F32)], axis=0)
    ada = _ada(c_all, w_ada[0], b_ada[0])
    sh1, sc1, ga1, sh2, sc2, ga2 = [ada[:, k * d:(k + 1) * d] for k in range(6)]
    pr = lambda a: a[0:1]
    sm = lambda a: jnp.repeat(a[1:1 + bs], ts, axis=0)

    w_main, w_gate = _prep_w_in(w_in[0])
    wo_perm = _prep_w_out(w_out[0])
    g1 = norm1_g[0].reshape(1, d)
    g2 = norm2_g[0].reshape(1, d)
    cw = conv_w[0]
    cb = conv_b[0].reshape(1, dc)
    cmpw = _prep_cmp_weights(cmp_pe[0], cmp_w1[0], cmp_b1[0], cmp_w2[0], cmp_b2[0])

    xp = x_prompt[0]
    cos_p, sin_p = _rope_tables(jnp.arange(t))
    convy_p, q_p, rows_p, win_p, kvb_p, gates_p, zc_p = _proj(
        xp, g1, pr(sc1), pr(sh1), w_main, w_gate, cos_p, sin_p, cw, cb)
    ident = jnp.arange(t // PAGE, dtype=jnp.int32).reshape(1, t // PAGE)
    kc_p, vc_p = _compress(rows_p.reshape(t // PAGE, PAGE, 4 * KV_W), ident, cmpw, False)
    attn_p = _nsa_prompt(q_p, gates_p, kc_p[0], vc_p[0], kvb_p)
    wq_bf = peer_wq[0].astype(BF16)
    keys_bf = peer_keys[0].reshape(2 * PEER_HEADS, N_KEYS, -1).astype(BF16)
    x1_p, h2_p, s_p = _outproj(xp, convy_p, attn_p, pr(ga1), g2, pr(sc2), pr(sh2), wo_perm, wq_bf, keys_bf, False)

    xs = x_sample.reshape(rs, d)
    pos_s = jnp.tile(past + jnp.arange(ts), bs)
    cos_s, sin_s = _rope_tables(pos_s)
    st = state_conv[0]
    a1 = jnp.repeat(st[:, 1], ts, axis=0)
    a2 = jnp.stack([st[:, 0], st[:, 1], st[:, 1], st[:, 1]], axis=1).reshape(rs, dc)
    convy_s, q_s, rows_s, win_s, kvb_s, gates_s, zc_s = _proj(
        xs, g1, sm(sc1), sm(sh1), w_main, w_gate, cos_s, sin_s, cw, cb, a1, a2)
    pool = cache_kv[0].transpose(0, 2, 3, 4, 1).reshape(cache_kv.shape[1], 4 * KV_W, PAGE)
    kc_s, vc_s = _compress(pool, page_table, cmpw, True)
    q5 = q_s.reshape(bs, ts, N_KV, GROUP, 1, HEAD_DIM).transpose(0, 3, 2, 1, 4, 5)
    qrep = jnp.broadcast_to(q5, (bs, GROUP, N_KV, ts, N_KV, HEAD_DIM)).reshape(bs, GROUP * N_KV * ts, KV_W)
    gts = gates_s[:, :3 * N_HEADS].reshape(bs, ts, N_KV, GROUP, 3).transpose(0, 4, 3, 2, 1)
    gx = jnp.broadcast_to(gts.reshape(bs, 3, GROUP * N_KV * ts, 1), (bs, 3, GROUP * N_KV * ts, KV_W))
    new8 = jnp.pad(kvb_s.reshape(bs, ts, 4 * KV_W), ((0, 0), (0, 8 - ts), (0, 0)))
    sw = state_win[0].transpose(0, 2, 3, 4, 1).reshape(bs, 2 * KV_W, wkeep)
    o_s = _nsa_sample(pool, page_table, qrep, gx, kc_s, vc_s, sw, new8)
    o6 = o_s.reshape(bs, GROUP, N_KV, ts, N_KV, HEAD_DIM)
    gi = jnp.arange(N_KV)
    attn_s = o6[:, :, gi, :, gi, :]
    attn_s = attn_s.transpose(1, 3, 2, 0, 4).reshape(rs, D_ATTN).astype(BF16)
    x1_s, h2_s, s_s = _outproj(xs, convy_s, attn_s, sm(ga1), g2, sm(sc2), sm(sh2), wo_perm, wq_bf, keys_bf, True)

    ntok = t + rs
    tm = PEER_TOKEN_TILE
    tp = -(-ntok // tm) * tm
    padt = lambda a: jnp.pad(jnp.concatenate(a, axis=0), ((0, tp - ntok), (0, 0))).T
    h_t = padt([h2_p, h2_s])
    s_t = padt([s_p, s_s])
    a2p, e2p, thr, e1p = _peer_select(s_t)
    u_bf = peer_u[0].astype(BF16)
    vt_bf = peer_v[0].astype(BF16).T
    f = _peer_dense(h_t, u_bf, vt_bf, thr, e1p, a2p, e2p, tm)
    fg = final_g.reshape(1, d)
    y_p = _final(x1_p, f, 0, pr(ga2), fg, False)
    y_s = _final(x1_s, f, t, sm(ga2), fg, True)

    wmin = min(WINDOW, t)
    win_ctx = jnp.concatenate([state_win[0], win_s.reshape(bs, ts, 2, N_KV, HEAD_DIM)], axis=1)[:, ts:]
    zc3 = zc_s.reshape(bs, ts, dc)
    return (y_p.reshape(1, t, d), y_s.reshape(bs, ts, d),
            rows_p.reshape(1, 1, t, 4, N_KV, HEAD_DIM), rows_s.reshape(1, bs, ts, 4, N_KV, HEAD_DIM),
            win_p[t - wmin:].reshape(1, 1, wmin, 2, N_KV, HEAD_DIM), win_ctx[None],
            zc_p[8 - (3 - 1):].reshape(1, 1, 2, dc), zc3[:, ts - 2:][None])
```

```python
import functools

import numpy as np
import jax
import jax.numpy as jnp
from jax import lax
from jax.experimental import pallas as pl
from jax.experimental.pallas import tpu as pltpu

F32 = jnp.float32
BF16 = jnp.bfloat16

HEAD_DIM = 64
N_KV = 4
GROUP = 4
N_HEADS = N_KV * GROUP
KV_W = N_KV * HEAD_DIM
D_ATTN = N_HEADS * HEAD_DIM
CMP_STRIDE = 16
SLC_BLOCK = 64
TOP_N = 16
WINDOW = 512
Q_BLOCK = 128
PAGE = 128
FORCE_SCORE = 1.0e4
ROPE_THETA = 10000.0
N_KEYS = 128
PEER_HEADS = 8
PEER_TOPK = 16
EPS = 1e-6
LANES = 128
PAGES_PER_STEP = 32
KEY_TILE = 256
V7X_VMEM_LIMIT = 58 * 1024 * 1024
NEG_BIG = -1e30
LOG2E = 1.4426950408889634


def _cparams(*sem):
    return pltpu.CompilerParams(dimension_semantics=sem, vmem_limit_bytes=V7X_VMEM_LIMIT)


def _vmem():
    return pl.BlockSpec(memory_space=pltpu.VMEM)


def _dot(a, b):
    return jnp.dot(a, b, preferred_element_type=F32)


def _dot_nt(a, b):
    return lax.dot_general(a, b, (((1,), (1,)), ((), ())), preferred_element_type=F32)


def _split_dot(x, m):
    hi = x.astype(BF16)
    lo = (x - hi.astype(F32)).astype(BF16)
    return _dot(hi, m) + _dot(lo, m)


def _rope(x, cos, sinsg):
    w = x.shape[1]
    n = w // LANES
    c = jnp.concatenate([cos] * n, axis=1) if n > 1 else cos
    s = jnp.concatenate([sinsg] * n, axis=1) if n > 1 else sinsg
    lane = lax.broadcasted_iota(jnp.int32, x.shape, 1)
    first = (lane % HEAD_DIM) < (HEAD_DIM // 2)
    partner = jnp.where(first, pltpu.roll(x, w - HEAD_DIM // 2, 1), pltpu.roll(x, HEAD_DIM // 2, 1))
    return x * c + partner * s


def _rope_tables(pos):
    half = HEAD_DIM // 2
    inv = ROPE_THETA ** (-jnp.arange(half, dtype=F32) / half)
    ang = pos.astype(F32)[:, None] * inv[None, :]
    cos = jnp.cos(ang)
    sin = jnp.sin(ang)
    cos = jnp.concatenate([cos, cos, cos, cos], axis=1)
    sinsg = jnp.concatenate([-sin, sin, -sin, sin], axis=1)
    return cos, sinsg


def _rms_mod(xf, g, sc, sh):
    y = xf * lax.rsqrt(jnp.mean(xf * xf, axis=-1, keepdims=True) + EPS)
    return y * g * (1.0 + sc) + sh


def _ada_kernel(c_ref, w_ref, b_ref, o_ref):
    s = jax.nn.silu(c_ref[...]).astype(BF16)
    o_ref[...] = _dot(s, w_ref[...].astype(BF16)) + b_ref[...]


def _ada(c_all, w_ada, b_ada):
    r, d = c_all.shape
    n = w_ada.shape[1]
    tn = 1024
    return pl.pallas_call(
        _ada_kernel,
        grid=(n // tn,),
        in_specs=[pl.BlockSpec((r, d), lambda j: (0, 0)),
                  pl.BlockSpec((d, tn), lambda j: (0, j)),
                  pl.BlockSpec((1, tn), lambda j: (0, j))],
        out_specs=pl.BlockSpec((r, tn), lambda j: (0, j)),
        out_shape=jax.ShapeDtypeStruct((r, n), F32),
        compiler_params=_cparams("arbitrary"),
        name="ada",
    )(c_all, w_ada, b_ada.reshape(1, n))


def _proj_kernel(*refs, sample, tm, dc):
    if sample:
        (x_ref, g1_ref, sc_ref, sh_ref, w_ref, wg_ref, cos_ref, sin_ref, cw_ref, cb_ref, a1_ref, a2_ref,
         convy_ref, q_ref, rows_ref, win_ref, kvb_ref, gates_ref, zc_ref) = refs
    else:
        (x_ref, g1_ref, sc_ref, sh_ref, w_ref, wg_ref, cos_ref, sin_ref, cw_ref, cb_ref,
         convy_ref, q_ref, rows_ref, win_ref, kvb_ref, gates_ref, zc_ref, carry_ref) = refs
    h = _rms_mod(x_ref[...], g1_ref[...], sc_ref[...], sh_ref[...]).astype(BF16)

    def mm(a, b):
        return _dot(h, w_ref[:, a:b])

    cos = cos_ref[...]
    sin = sin_ref[...]
    q0 = 3 * dc
    kv0 = q0 + D_ATTN
    g0 = kv0 + 6 * KV_W

    b_g = mm(0, dc)
    zc = mm(dc, 2 * dc) * mm(2 * dc, 3 * dc)
    row = lax.broadcasted_iota(jnp.int32, zc.shape, 0)
    if sample:
        t = row % 4
        zm1 = jnp.where(t >= 1, pltpu.roll(zc, 1, 0), a1_ref[...])
        zm2 = jnp.where(t >= 2, pltpu.roll(zc, 2, 0), a2_ref[...])
        zc_ref[...] = zc
    else:
        @pl.when(pl.program_id(0) == 0)
        def _():
            carry_ref[...] = jnp.zeros_like(carry_ref)
        p1 = carry_ref[7:8, :]
        p2 = carry_ref[6:7, :]
        zm1 = jnp.where(row == 0, p1, pltpu.roll(zc, 1, 0))
        zm2 = jnp.where(row == 0, p2, jnp.where(row == 1, p1, pltpu.roll(zc, 2, 0)))
        carry_ref[...] = zc[tm - 8:tm, :]
        zc_ref[...] = zc[tm - 8:tm, :]
    conv = cb_ref[...] + cw_ref[0:1, :] * zm2
    conv = conv + cw_ref[1:2, :] * zm1
    conv = conv + cw_ref[2:3, :] * zc
    convy_ref[...] = (b_g * conv).astype(BF16)

    q = _rope(mm(q0, kv0), cos, sin) * (HEAD_DIM ** -0.5)
    q_ref[...] = q.astype(BF16)

    kv = mm(kv0, g0)
    k_cmp = kv[:, 0:KV_W]
    v_cmp = kv[:, KV_W:2 * KV_W]
    k_slc = _rope(kv[:, 2 * KV_W:3 * KV_W], cos, sin)
    v_slc = kv[:, 3 * KV_W:4 * KV_W]
    k_win = _rope(kv[:, 4 * KV_W:5 * KV_W], cos, sin)
    v_win = kv[:, 5 * KV_W:6 * KV_W]
    rows_ref[:, 0:KV_W] = k_cmp
    rows_ref[:, KV_W:2 * KV_W] = v_cmp
    rows_ref[:, 2 * KV_W:3 * KV_W] = k_slc
    rows_ref[:, 3 * KV_W:4 * KV_W] = v_slc
    win_ref[:, 0:KV_W] = k_win
    win_ref[:, KV_W:2 * KV_W] = v_win
    kvb_ref[:, 0:KV_W] = k_slc.astype(BF16)
    kvb_ref[:, KV_W:2 * KV_W] = v_slc.astype(BF16)
    kvb_ref[:, 2 * KV_W:3 * KV_W] = k_win.astype(BF16)
    kvb_ref[:, 3 * KV_W:4 * KV_W] = v_win.astype(BF16)
    gates_ref[...] = jax.nn.sigmoid(_dot(h, wg_ref[...]))


def _proj(x, g1, sc, sh, w_main, w_gate, cos, sin, conv_w, conv_b, a1=None, a2=None):
    sample = a1 is not None
    r, d = x.shape
    dc = d // 2
    tm = r if sample else 256
    row_spec = lambda wdt: pl.BlockSpec((tm, wdt), lambda i: (i, 0))
    full = lambda a: pl.BlockSpec(a.shape, lambda i: (0,) * a.ndim)
    mod_spec = row_spec(d) if sample else pl.BlockSpec((1, d), lambda i: (0, 0))
    in_specs = [row_spec(d), full(g1), mod_spec, mod_spec, _vmem(), _vmem(), row_spec(LANES), row_spec(LANES),
                full(conv_w), full(conv_b)]
    args = [x, g1, sc, sh, w_main, w_gate, cos, sin, conv_w, conv_b]
    scratch = []
    if sample:
        in_specs += [row_spec(dc), row_spec(dc)]
        args += [a1, a2]
        zc_shape = jax.ShapeDtypeStruct((r, dc), F32)
        zc_spec = row_spec(dc)
    else:
        zc_shape = jax.ShapeDtypeStruct((8, dc), F32)
        zc_spec = pl.BlockSpec((8, dc), lambda i: (0, 0))
        scratch = [pltpu.VMEM((8, dc), F32)]
    out_shape = (jax.ShapeDtypeStruct((r, dc), BF16), jax.ShapeDtypeStruct((r, D_ATTN), BF16),
                 jax.ShapeDtypeStruct((r, 4 * KV_W), F32), jax.ShapeDtypeStruct((r, 2 * KV_W), F32),
                 jax.ShapeDtypeStruct((r, 4 * KV_W), BF16), jax.ShapeDtypeStruct((r, LANES), F32), zc_shape)
    out_specs = (row_spec(dc), row_spec(D_ATTN), row_spec(4 * KV_W), row_spec(2 * KV_W),
                 row_spec(4 * KV_W), row_spec(LANES), zc_spec)
    return pl.pallas_call(
        functools.partial(_proj_kernel, sample=sample, tm=tm, dc=dc),
        grid=(r // tm,),
        in_specs=in_specs, out_specs=out_specs, out_shape=out_shape, scratch_shapes=scratch,
        compiler_params=_cparams("arbitrary"),
        name="proj_sample" if sample else "proj_prompt",
    )(*args)


def _cmp_kernel(pt_ref, *refs, transposed):
    npg = PAGES_PER_STEP
    pages = refs[:npg]
    (perm_ref, w1a_ref, w1b_ref, pea_ref, peb_ref, b1_ref, w2_ref, b2_ref, cos_ref, sin_ref,
     kc_ref, vc_ref, xa_s, xb_s, hprev_s) = refs[npg:]
    del pt_ref
    nch = PAGE // CMP_STRIDE
    rows = npg * nch

    @pl.when(pl.program_id(1) == 0)
    def _():
        hprev_s[...] = jnp.zeros_like(hprev_s)

    lane = lax.broadcasted_iota(jnp.int32, (nch, 2 * KV_W), 1)
    low = (lane % LANES) < HEAD_DIM
    perm = perm_ref[...]
    for k in range(npg):
        page = pages[k][...]
        for pe_ref, x_s in ((pea_ref, xa_s), (peb_ref, xb_s)):
            src = (page + pe_ref[...]).astype(BF16)
            y = _dot_nt(perm, src) if transposed else _dot(perm, src)
            for p in range(CMP_STRIDE // 2):
                ev = y[2 * p * nch:(2 * p + 1) * nch]
                od = y[(2 * p + 1) * nch:(2 * p + 2) * nch]
                a = jnp.where(low, ev, pltpu.roll(od, HEAD_DIM, 1))
                b = jnp.where(low, pltpu.roll(ev, 2 * KV_W - HEAD_DIM, 1), od)
                for j in range(4):
                    kv, ge = j // 2, 2 * (j % 2)
                    rsl = slice(nch * k, nch * (k + 1))
                    csl = slice(LANES * p, LANES * (p + 1))
                    x_s[kv, ge, rsl, csl] = a[:, LANES * j:LANES * (j + 1)]
                    x_s[kv, ge + 1, rsl, csl] = b[:, LANES * j:LANES * (j + 1)]

    row = lax.broadcasted_iota(jnp.int32, (N_KV * rows, 1), 0)
    for kv in range(2):
        ha = _dot(xa_s[kv].reshape(N_KV * rows, CMP_STRIDE * HEAD_DIM).astype(BF16), w1a_ref[kv])
        hb = _dot(xb_s[kv].reshape(N_KV * rows, CMP_STRIDE * HEAD_DIM).astype(BF16), w1b_ref[kv])
        hp = hprev_s[kv]
        prev = jnp.where(row % rows == 0, pltpu.roll(hp, N_KV * rows - (rows - 1), 0), pltpu.roll(ha, 1, 0))
        hprev_s[kv] = ha
        hid = jax.nn.gelu(prev + hb + b1_ref[kv])
        hcat = jnp.concatenate([hid[rows * g:rows * (g + 1)] for g in range(N_KV)], axis=1).astype(BF16)
        o = _dot(hcat, w2_ref[kv]) + b2_ref[kv]
        if kv == 0:
            kc_ref[...] = _rope(o, cos_ref[...], sin_ref[...]).astype(BF16)
        else:
            vc_ref[...] = o.astype(BF16)


def _compress(pool, page_table, cw, transposed):
    b, n_pages = page_table.shape
    npg = PAGES_PER_STEP
    nsteps = n_pages // npg
    rows = npg * (PAGE // CMP_STRIDE)
    nc = nsteps * rows
    cpos = CMP_STRIDE * jnp.arange(nc) + (CMP_STRIDE - 1)
    cos, sin = _rope_tables(cpos)
    page_block = (None, 2 * KV_W, PAGE) if transposed else (None, PAGE, 2 * KV_W)

    def page_spec(k):
        return pl.BlockSpec(page_block, lambda bi, i, pt: (pt[bi, i * npg + k], 0, 0))

    const = lambda a: pl.BlockSpec(a.shape, lambda bi, i, pt: (0,) * a.ndim)
    nch = PAGE // CMP_STRIDE
    perm = np.zeros((PAGE, PAGE), np.float32)
    for s in range(CMP_STRIDE):
        for c in range(nch):
            perm[s * nch + c, CMP_STRIDE * c + s] = 1.0
    pea, peb = (cw["pea"].T, cw["peb"].T) if transposed else (cw["pea"], cw["peb"])
    weights = [jnp.asarray(perm, BF16), cw["w1a"], cw["w1b"], pea, peb, cw["b1"], cw["w2bd"], cw["b2"]]
    tab_spec = pl.BlockSpec((rows, LANES), lambda bi, i, pt: (i, 0))
    out_spec = pl.BlockSpec((None, rows, KV_W), lambda bi, i, pt: (bi, i, 0))
    x_scratch = pltpu.VMEM((2, N_KV, rows, CMP_STRIDE * HEAD_DIM), F32)
    grid_spec = pltpu.PrefetchScalarGridSpec(
        num_scalar_prefetch=1,
        grid=(b, nsteps),
        in_specs=[page_spec(k) for k in range(npg)] + [const(a) for a in weights] + [tab_spec, tab_spec],
        out_specs=(out_spec, out_spec),
        scratch_shapes=[x_scratch, x_scratch, pltpu.VMEM((2, N_KV * rows, 2 * HEAD_DIM), F32)],
    )
    return pl.pallas_call(
        functools.partial(_cmp_kernel, transposed=transposed),
        grid_spec=grid_spec,
        out_shape=(jax.ShapeDtypeStruct((b, nc, KV_W), BF16), jax.ShapeDtypeStruct((b, nc, KV_W), BF16)),
        compiler_params=_cparams("arbitrary", "arbitrary"),
        name="compress",
    )(page_table, *([pool] * npg), *weights, cos, sin)


def _prep_cmp_weights(cmp_pe, cmp_w1, cmp_b1, cmp_w2, cmp_b2):
    s = CMP_STRIDE
    w2bd = jnp.zeros((2, N_KV * 2 * HEAD_DIM, KV_W), F32)
    for g in range(N_KV):
        w2bd = w2bd.at[:, g * 2 * HEAD_DIM:(g + 1) * 2 * HEAD_DIM, g * HEAD_DIM:(g + 1) * HEAD_DIM].set(cmp_w2)
    def pe_page(pe_half):
        t = jnp.broadcast_to(pe_half.transpose(1, 0, 2)[:, :, None, :], (s, 2, N_KV, HEAD_DIM))
        return jnp.tile(t.reshape(s, 2 * KV_W), (PAGE // s, 1))

    return {
        "w1a": cmp_w1[:, :s].reshape(2, s * HEAD_DIM, 2 * HEAD_DIM).astype(BF16),
        "w1b": cmp_w1[:, s:].reshape(2, s * HEAD_DIM, 2 * HEAD_DIM).astype(BF16),
        "pea": pe_page(cmp_pe[:, :s]),
        "peb": pe_page(cmp_pe[:, s:]),
        "b1": cmp_b1.reshape(2, 1, 2 * HEAD_DIM),
        "w2bd": w2bd.astype(BF16),
        "b2": jnp.tile(cmp_b2, (1, N_KV)).reshape(2, 1, KV_W),
    }


def _cmp_to_slc(nc, ns):
    j = np.arange(nc)[:, None] - 1
    b = np.arange(ns)[None, :]
    ov = (CMP_STRIDE * j < SLC_BLOCK * (b + 1)) & (CMP_STRIDE * j + 2 * CMP_STRIDE > SLC_BLOCK * b) & (j >= 0)
    return ov.astype(np.float32)


def _masked_softmax(s, mask):
    s = jnp.where(mask, s, -jnp.inf)
    m = jnp.max(s, axis=-1, keepdims=True)
    m = jnp.where(m > -jnp.inf, m, 0.0)
    p = jnp.exp(s - m)
    return p * (1.0 / jnp.maximum(jnp.sum(p, axis=-1, keepdims=True), 1e-30))


def _select_blocks(score, n_top):
    lane = lax.broadcasted_iota(jnp.int32, score.shape, 1)
    nb = score.shape[1]
    sel = jnp.zeros(score.shape, F32)
    cur = score
    for _ in range(n_top):
        mx = jnp.max(cur, axis=-1, keepdims=True)
        first = jnp.min(jnp.where(cur == mx, lane, nb), axis=-1, keepdims=True)
        hit = (lane == first) & (mx > -jnp.inf)
        sel = jnp.where(hit, 1.0, sel)
        cur = jnp.where(lane == first, -jnp.inf, cur)
    return sel


def _masked_softmax0(s, mask):
    s = jnp.where(mask, s, -jnp.inf)
    m = jnp.max(s, axis=0, keepdims=True)
    m = jnp.where(m > -jnp.inf, m, 0.0)
    p = jnp.exp(s - m)
    return p * (1.0 / jnp.maximum(jnp.sum(p, axis=0, keepdims=True), 1e-30))


def _select_blocks_t(score, n_top):
    rowi = lax.broadcasted_iota(jnp.int32, score.shape, 0)
    nb = score.shape[0]
    sel = jnp.zeros(score.shape, F32)
    cur = score
    for _ in range(n_top):
        mx = jnp.max(cur, axis=0, keepdims=True)
        first = jnp.min(jnp.where(cur == mx, rowi, nb), axis=0, keepdims=True)
        is_first = rowi == first
        sel = jnp.where(is_first & (mx > -jnp.inf), 1.0, sel)
        cur = jnp.where(is_first, -jnp.inf, cur)
    return sel


def _own_blocks(x, tq):
    rq = GROUP * tq
    return jnp.concatenate([x[g * HEAD_DIM:(g + 1) * HEAD_DIM, g * rq:(g + 1) * rq] for g in range(N_KV)], axis=0)


def _nsa_prompt_kernel(qt_ref, gt_ref, kc_ref, vct_ref, ks_ref, vst_ref, kw_ref, vwt_ref, mt_ref,
                       out_ref, qm_s, bias_s, acc_s, m_s, l_s, sa_s, sb_s, ma_s, mb_s, *, nc, ns):
    i = pl.program_id(0)
    tq = Q_BLOCK
    rq = GROUP * tq
    ncol = N_KV * rq
    colq = i * tq + lax.broadcasted_iota(jnp.int32, (1, rq), 1) % tq
    colq_g = i * tq + lax.broadcasted_iota(jnp.int32, (1, N_KV * tq), 1) % tq
    grows = [slice(g * HEAD_DIM, (g + 1) * HEAD_DIM) for g in range(N_KV)]
    gcols = [slice(g * rq, (g + 1) * rq) for g in range(N_KV)]

    @pl.when(i == 0)
    def _():
        qm_s[...] = jnp.zeros_like(qm_s)

    for g in range(N_KV):
        for r in range(GROUP):
            hd = (g * GROUP + r) * HEAD_DIM
            qm_s[grows[g], (g * GROUP + r) * tq:(g * GROUP + r + 1) * tq] = qt_ref[hd:hd + HEAD_DIM, :]

    crow = lax.broadcasted_iota(jnp.int32, (nc, 1), 0)
    cvalid = (crow >= 1) & (CMP_STRIDE * crow + (CMP_STRIDE - 1) <= colq)
    kc = kc_ref[...]
    o_c, psum = [], []
    scores = [_dot(kc, qm_s[:, gcols[g]]) for g in range(N_KV)]
    for g in range(N_KV):
        p = _masked_softmax0(scores[g], cvalid)
        o_c.append(_dot(vct_ref[grows[g], :], p.astype(BF16)))
        psum.append(sum(p[:, r * tq:(r + 1) * tq] for r in range(GROUP)))
    o_c = jnp.concatenate(o_c, axis=0)
    psum = jnp.concatenate(psum, axis=1)
    hi = psum.astype(BF16)
    lo = (psum - hi.astype(F32)).astype(BF16)
    imp = _dot(mt_ref[...], hi) + _dot(mt_ref[...], lo)
    blk = lax.broadcasted_iota(jnp.int32, (ns, 1), 0)
    cur = colq_g // SLC_BLOCK
    forced = (blk == 0) | (blk == cur) | (blk == cur - 1)
    score = jnp.where(blk <= cur, jnp.where(forced, FORCE_SCORE, imp), -jnp.inf)
    bias = (_select_blocks_t(score, min(TOP_N, ns)) - 1.0) * (-NEG_BIG)
    for g in range(N_KV):
        for r in range(GROUP):
            c0 = (g * GROUP + r) * tq
            bias_s[:, c0:c0 + tq] = bias[:, g * tq:(g + 1) * tq]

    w0 = pl.multiple_of(i * tq, tq)
    kw = kw_ref[pl.ds(w0, WINDOW + tq), :]
    nwt = (WINDOW + tq) // tq
    vwt = vwt_ref[pl.ds(i, nwt)]
    wpos = i * tq - WINDOW + lax.broadcasted_iota(jnp.int32, (WINDOW + tq, 1), 0)
    dist = colq - wpos
    wmask = (dist >= 0) & (dist < WINDOW) & (wpos >= 0)
    o_w = []
    scores = [_dot(kw, qm_s[:, gcols[g]]) for g in range(N_KV)]
    for g in range(N_KV):
        s = jnp.where(wmask, scores[g], -jnp.inf)
        p = jnp.exp2(((s - jnp.max(s, axis=0, keepdims=True)) * LOG2E).astype(BF16))
        r = _dot(jnp.concatenate([vwt[j, g] for j in range(nwt)], axis=1), p)
        o_w.append(r[0:HEAD_DIM] * (1.0 / r[HEAD_DIM:HEAD_DIM + 1]))
    o_w = jnp.concatenate(o_w, axis=0)

    acc_s[...] = jnp.zeros_like(acc_s)
    m_s[...] = jnp.full(m_s.shape, NEG_BIG, F32)
    l_s[...] = jnp.zeros_like(l_s)
    blocks_per_tile = KEY_TILE // SLC_BLOCK
    krow = lax.broadcasted_iota(jnp.int32, (KEY_TILE, 1), 0)

    def scores(kt, buf, mx, causal):
        k0 = pl.multiple_of(kt * KEY_TILE, KEY_TILE)
        kk = ks_ref[pl.ds(k0, KEY_TILE), :]
        raw = [_dot(kk, qm_s[:, gcols[g]]) for g in range(N_KV)]
        for g in range(N_KV):
            s = jnp.concatenate(
                [raw[g][j * SLC_BLOCK:(j + 1) * SLC_BLOCK] + bias_s[pl.ds(kt * blocks_per_tile + j, 1), gcols[g]]
                 for j in range(blocks_per_tile)], axis=0)
            if causal:
                s = jnp.where(k0 + krow <= colq, s, NEG_BIG)
            buf[g] = s
            mx[:, gcols[g]] = jnp.max(s, axis=0, keepdims=True)

    def tile(kt, buf, mx):
        va = vst_ref[kt]
        probs, alphas = [], []
        for g in range(N_KV):
            m_old = m_s[:, gcols[g]]
            m_new = jnp.maximum(m_old, mx[:, gcols[g]])
            m_s[:, gcols[g]] = m_new
            alphas.append(jnp.exp(m_old - m_new))
            probs.append(jnp.exp2(((buf[g] - m_new) * LOG2E).astype(BF16)))
        for g in range(N_KV):
            r = _dot(va[g], probs[g])
            acc_s[grows[g], :] = acc_s[grows[g], :] * alphas[g] + r[0:HEAD_DIM]
            l_s[:, gcols[g]] = alphas[g] * l_s[:, gcols[g]] + r[HEAD_DIM:HEAD_DIM + 1]

    last = (i * tq) // KEY_TILE

    scores(0, sa_s, ma_s, False)

    def body(j, carry):
        scores(2 * j + 1, sb_s, mb_s, False)
        tile(2 * j, sa_s, ma_s)
        scores(2 * j + 2, sa_s, ma_s, False)
        tile(2 * j + 1, sb_s, mb_s)
        return carry

    lax.fori_loop(0, last // 2, body, 0)

    @pl.when(last % 2 == 0)
    def _():
        scores(last, sa_s, ma_s, True)
        tile(last, sa_s, ma_s)

    @pl.when(last % 2 == 1)
    def _():
        scores(last, sb_s, mb_s, True)
        tile(last - 1, sa_s, ma_s)
        tile(last, sb_s, mb_s)

    rl = 1.0 / l_s[...]
    o_s = acc_s[...] * jnp.concatenate(
        [jnp.broadcast_to(rl[:, g * rq:(g + 1) * rq], (HEAD_DIM, rq)) for g in range(N_KV)], axis=0)

    gt = gt_ref[...]

    def gate(br):
        return jnp.concatenate(
            [jnp.concatenate(
                [jnp.broadcast_to(gt[(GROUP * g + r) * 3 + br:(GROUP * g + r) * 3 + br + 1, :], (HEAD_DIM, tq))
                 for r in range(GROUP)], axis=1) for g in range(N_KV)], axis=0)

    o = gate(0) * o_c + gate(1) * o_s + gate(2) * o_w
    for r in range(GROUP):
        out_ref[:, r * KV_W:(r + 1) * KV_W] = o[:, r * tq:(r + 1) * tq].T.astype(BF16)


def _nsa_prompt(q, gates, kc, vc, kvb):
    t = q.shape[0]
    nc = kc.shape[0]
    ns = t // SLC_BLOCK
    tq = Q_BLOCK
    qt = q.T
    gt = gates.T
    vct = vc.T
    ks = kvb[:, 0:KV_W]
    vst = kvb[:, KV_W:2 * KV_W].reshape(t // KEY_TILE, KEY_TILE, N_KV, HEAD_DIM).transpose(0, 2, 3, 1)
    vst = jnp.concatenate([vst, jnp.ones((t // KEY_TILE, N_KV, 16, KEY_TILE), BF16)], axis=2)
    kw = jnp.pad(kvb[:, 2 * KV_W:3 * KV_W], ((WINDOW, 0), (0, 0)))
    vwt = jnp.pad(kvb[:, 3 * KV_W:4 * KV_W], ((WINDOW, 0), (0, 0)))
    vwt = vwt.reshape((t + WINDOW) // tq, tq, N_KV, HEAD_DIM).transpose(0, 2, 3, 1)
    vwt = jnp.concatenate([vwt, jnp.ones(((t + WINDOW) // tq, N_KV, 16, tq), BF16)], axis=2)
    mt = jnp.asarray(_cmp_to_slc(nc, ns).T, BF16)
    full = lambda a: pl.BlockSpec(a.shape, lambda i: (0,) * a.ndim)
    ncol = N_KV * GROUP * tq
    return pl.pallas_call(
        functools.partial(_nsa_prompt_kernel, nc=nc, ns=ns),
        grid=(t // tq,),
        in_specs=[pl.BlockSpec((D_ATTN, tq), lambda i: (0, i)),
                  pl.BlockSpec((LANES, tq), lambda i: (0, i)),
                  full(kc), full(vct), _vmem(), _vmem(), _vmem(), _vmem(), full(mt)],
        out_specs=pl.BlockSpec((tq, D_ATTN), lambda i: (i, 0)),
        out_shape=jax.ShapeDtypeStruct((t, D_ATTN), BF16),
        scratch_shapes=[pltpu.VMEM((KV_W, ncol), BF16),
                        pltpu.VMEM((ns, ncol), F32),
                        pltpu.VMEM((KV_W, GROUP * tq), F32),
                        pltpu.VMEM((1, ncol), F32),
                        pltpu.VMEM((1, ncol), F32),
                        pltpu.VMEM((N_KV, KEY_TILE, GROUP * tq), F32),
                        pltpu.VMEM((N_KV, KEY_TILE, GROUP * tq), F32),
                        pltpu.VMEM((1, ncol), F32),
                        pltpu.VMEM((1, ncol), F32)],
        compiler_params=_cparams("arbitrary"),
        name="nsa_prompt",
    )(qt, gt, kc, vct, ks, vst, kw, vwt, mt)


def _nsa_sample_kernel(pt_ref, *refs, nc, ns, nsteps):
    npg = PAGES_PER_STEP
    pages = refs[:npg]
    (qrep_ref, gx_ref, kc_ref, vc_ref, sw_ref, new_ref, m_ref, e_ref, en_ref,
     out_ref, qall_s, sel_s, oc_s, ow_s, m_s, l_s, acc_s) = refs[npg:]
    del pt_ref
    i = pl.program_id(1)
    nr = GROUP * N_KV * 4
    row = lax.broadcasted_iota(jnp.int32, (nr, 1), 0)
    row_t = row % 4
    row_g = (row // 4) % N_KV
    lane_g = lax.broadcasted_iota(jnp.int32, (1, KV_W), 1) // HEAD_DIM
    own = row_g == lane_g
    new = new_ref[...]
    ucol = lax.broadcasted_iota(jnp.int32, (1, 8), 1)
    new_ok = (ucol <= row_t) & (ucol < 4)

    @pl.when(i == 0)
    def _():
        qall = jnp.where(own, qrep_ref[...], 0).astype(BF16)
        qall_s[...] = qall
        cidx = lax.broadcasted_iota(jnp.int32, (1, nc), 1)
        p = _masked_softmax(_dot_nt(qall, kc_ref[...]), cidx >= 1)
        oc_s[...] = _dot(p.astype(BF16), vc_ref[...])
        n16 = N_KV * 4
        psum = p[0:n16] + p[n16:2 * n16] + p[2 * n16:3 * n16] + p[3 * n16:4 * n16]
        imp = _split_dot(psum, m_ref[...])
        nb = imp.shape[1]
        blk = lax.broadcasted_iota(jnp.int32, (1, nb), 1)
        cur = ns - 1
        forced = (blk == 0) | (blk == cur) | (blk == cur - 1)
        score = jnp.where(blk <= cur, jnp.where(forced, FORCE_SCORE, imp), -jnp.inf)
        sel_s[...] = _select_blocks(score, min(TOP_N, ns)).astype(BF16)
        kw_t = sw_ref[0:KV_W, :].astype(BF16)
        vw_t = sw_ref[KV_W:2 * KV_W, :].astype(BF16)
        wcol = lax.broadcasted_iota(jnp.int32, (1, kw_t.shape[1]), 1)
        s1 = jnp.where(wcol > row_t, _dot(qall, kw_t), -jnp.inf)
        s2 = jnp.where(new_ok, _dot_nt(qall, new[:, 2 * KV_W:3 * KV_W]), -jnp.inf)
        mw = jnp.maximum(jnp.max(s1, axis=-1, keepdims=True), jnp.max(s2, axis=-1, keepdims=True))
        p1 = jnp.exp(s1 - mw)
        p2 = jnp.exp(s2 - mw)
        den = jnp.maximum(jnp.sum(p1, axis=-1, keepdims=True) + jnp.sum(p2, axis=-1, keepdims=True), 1e-30)
        ow_s[...] = (_dot_nt((p1 / den).astype(BF16), vw_t)
                     + _dot((p2 / den).astype(BF16), new[:, 3 * KV_W:4 * KV_W]))
        m_s[...] = jnp.full(m_s.shape, NEG_BIG, F32)
        l_s[...] = jnp.zeros_like(l_s)
        acc_s[...] = jnp.zeros_like(acc_s)

    qall = qall_s[...]
    kk_t = jnp.concatenate([pages[k][0:KV_W, :] for k in range(npg)], axis=1).astype(BF16)
    vv_t = jnp.concatenate([pages[k][KV_W:2 * KV_W, :] for k in range(npg)], axis=1).astype(BF16)
    mk = _dot(sel_s[...], e_ref[i]) > 0.5
    mk = jnp.concatenate([mk] * GROUP, axis=0)
    s = jnp.where(mk, _dot(qall, kk_t), NEG_BIG)
    m_old = m_s[...]
    m_new = jnp.maximum(m_old, jnp.max(s, axis=-1, keepdims=True))
    alpha = jnp.exp(m_old - m_new)
    p = jnp.exp(s - m_new)
    l_new = alpha * l_s[...] + jnp.sum(p, axis=-1, keepdims=True)
    acc_new = alpha * acc_s[...] + _dot_nt(p.astype(BF16), vv_t)
    m_s[...] = m_new
    l_s[...] = l_new
    acc_s[...] = acc_new

    @pl.when(i == nsteps - 1)
    def _():
        mkn = _dot(sel_s[...], en_ref[...]) > 0.5
        mkn = jnp.concatenate([mkn] * GROUP, axis=0) & new_ok
        sn = jnp.where(mkn, _dot_nt(qall, new[:, 0:KV_W]), NEG_BIG)
        m_f = jnp.maximum(m_new, jnp.max(sn, axis=-1, keepdims=True))
        al = jnp.exp(m_new - m_f)
        pn = jnp.exp(sn - m_f)
        l_f = al * l_new + jnp.sum(pn, axis=-1, keepdims=True)
        acc_f = al * acc_new + _dot(pn.astype(BF16), new[:, KV_W:2 * KV_W])
        o_s = acc_f / l_f
        out_ref[...] = gx_ref[0] * oc_s[...] + gx_ref[1] * o_s + gx_ref[2] * ow_s[...]


def _nsa_sample(pool, page_table, qrep, gx, kc, vc, sw, new8):
    b, n_pages = page_table.shape
    npg = PAGES_PER_STEP
    nsteps = n_pages // npg
    nc = kc.shape[1]
    past = n_pages * PAGE
    ns = past // SLC_BLOCK + 1
    nb = -(-ns // LANES) * LANES
    keys_step = npg * PAGE
    m = np.zeros((nc, nb), np.float32)
    m[:, :ns] = _cmp_to_slc(nc, ns)
    e = np.zeros((nsteps, nb, keys_step), np.float32)
    for st in range(nsteps):
        for j in range(keys_step):
            e[st, (st * keys_step + j) // SLC_BLOCK, j] = 1.0
    en = np.zeros((nb, 8), np.float32)
    en[ns - 1, :] = 1.0
    m, e, en = jnp.asarray(m, BF16), jnp.asarray(e, BF16), jnp.asarray(en, BF16)
    nr = GROUP * N_KV * 4

    def page_spec(k):
        return pl.BlockSpec((None, 2 * KV_W, PAGE), lambda bi, i, pt: (pt[bi, i * npg + k], 1, 0))

    per_b = lambda a: pl.BlockSpec((None,) + a.shape[1:], lambda bi, i, pt: (bi,) + (0,) * (a.ndim - 1))
    const = lambda a: pl.BlockSpec(a.shape, lambda bi, i, pt: (0,) * a.ndim)
    grid_spec = pltpu.PrefetchScalarGridSpec(
        num_scalar_prefetch=1,
        grid=(b, nsteps),
        in_specs=[page_spec(k) for k in range(npg)]
        + [per_b(qrep), per_b(gx), per_b(kc), per_b(vc), per_b(sw), per_b(new8), const(m), _vmem(), const(en)],
        out_specs=pl.BlockSpec((None, nr, KV_W), lambda bi, i, pt: (bi, 0, 0)),
        scratch_shapes=[pltpu.VMEM((nr, KV_W), BF16), pltpu.VMEM((N_KV * 4, nb), BF16),
                        pltpu.VMEM((nr, KV_W), F32), pltpu.VMEM((nr, KV_W), F32),
                        pltpu.VMEM((nr, 1), F32), pltpu.VMEM((nr, 1), F32), pltpu.VMEM((nr, KV_W), F32)],
    )
    return pl.pallas_call(
        functools.partial(_nsa_sample_kernel, nc=nc, ns=ns, nsteps=nsteps),
        grid_spec=grid_spec,
        out_shape=jax.ShapeDtypeStruct((b, nr, KV_W), F32),
        compiler_params=_cparams("arbitrary", "arbitrary"),
        name="nsa_sample",
    )(page_table, *([pool] * npg), qrep, gx, kc, vc, sw, new8, m, e, en)


def _outproj_kernel(x_ref, cy_ref, at_ref, ga1_ref, g2_ref, sc2_ref, sh2_ref, wo_ref, wq_ref, keys_ref,
                    x1_ref, h2_ref, s_ref, *, dc):
    u = _dot(cy_ref[...], wo_ref[0:dc, :]) + _dot(at_ref[...], wo_ref[dc:, :])
    x1 = x_ref[...] + ga1_ref[...] * u
    x1_ref[...] = x1
    h2 = _rms_mod(x1, g2_ref[...], sc2_ref[...], sh2_ref[...]).astype(BF16)
    h2_ref[...] = h2
    qp = _dot(h2, wq_ref[...]).astype(BF16)
    for hc in range(2 * PEER_HEADS):
        s_ref[:, hc * N_KEYS:(hc + 1) * N_KEYS] = _dot_nt(qp[:, hc * LANES:(hc + 1) * LANES], keys_ref[hc])


def _outproj(x, convy, attn, ga1, g2, sc2, sh2, wo, wq, keys, per_row):
    r, d = x.shape
    dc = d // 2
    tm = r if per_row else 256
    row_spec = lambda wdt: pl.BlockSpec((tm, wdt), lambda i: (i, 0))
    mod_spec = row_spec(d) if per_row else pl.BlockSpec((1, d), lambda i: (0, 0))
    nsc = keys.shape[0] * N_KEYS
    return pl.pallas_call(
        functools.partial(_outproj_kernel, dc=dc),
        grid=(r // tm,),
        in_specs=[row_spec(d), row_spec(dc), row_spec(d - dc), mod_spec,
                  pl.BlockSpec((1, d), lambda i: (0, 0)), mod_spec, mod_spec, _vmem(), _vmem(), _vmem()],
        out_specs=(row_spec(d), row_spec(d), row_spec(nsc)),
        out_shape=(jax.ShapeDtypeStruct((r, d), F32), jax.ShapeDtypeStruct((r, d), BF16),
                   jax.ShapeDtypeStruct((r, nsc), F32)),
        compiler_params=_cparams("arbitrary"),
        name="outproj_sample" if per_row else "outproj_prompt",
    )(x, convy, attn, ga1, g2, sc2, sh2, wo, wq, keys)


def _sort_network(n):
    pairs = []

    def merge(lo, cnt, r):
        step = r * 2
        if step < cnt:
            merge(lo, cnt, step)
            merge(lo + r, cnt, step)
            for i in range(lo + r, lo + cnt - r, step):
                pairs.append((i, i + r))
        else:
            pairs.append((lo, lo + r))

    def sort(lo, cnt):
        if cnt > 1:
            half = cnt // 2
            sort(lo, half)
            sort(lo + half, half)
            merge(lo, cnt, 1)

    sort(0, n)
    return pairs


def _prune_network(pairs, wanted):
    need = set(wanted)
    keep = []
    for i, j in reversed(pairs):
        if i in need or j in need:
            keep.append((i, j))
            need.update((i, j))
    return keep[::-1]


def _apply_network(x, pairs):
    x = list(x)
    for i, j in pairs:
        x[i], x[j] = jnp.maximum(x[i], x[j]), jnp.minimum(x[i], x[j])
    return x


_SORT16 = _sort_network(PEER_TOPK)
_CAND_PAIRS = [(a, b) for a in range(PEER_TOPK + 1) for b in range(PEER_TOPK + 1)
               if (a + 1) * (b + 1) <= PEER_TOPK + 1]
_CAND_WIRES = 64
_SELECT_16_17 = _prune_network(_sort_network(_CAND_WIRES), (PEER_TOPK - 1, PEER_TOPK))


def _top17(s):
    k = PEER_TOPK
    sub = 8
    lst = _apply_network([s[sub * v:sub * (v + 1)] for v in range(k)], _SORT16)
    for dist in (1, 2, 4):
        c = [jnp.maximum(lst[v], pltpu.roll(lst[k - 1 - v], dist, 0)) for v in range(k)]
        d = k // 2
        while d >= 1:
            for i in range(k):
                if i & d == 0:
                    c[i], c[i + d] = jnp.maximum(c[i], c[i + d]), jnp.minimum(c[i], c[i + d])
            d //= 2
        lst = c
    top = [v[sub - 1:sub, :] for v in lst]
    v16 = top[k - 1]
    ge = s >= v16
    cnt = jnp.sum(jnp.where(ge, 1.0, 0.0), axis=0, keepdims=True)
    below = jnp.max(jnp.where(ge, -jnp.inf, s), axis=0, keepdims=True)
    top.append(jnp.where(cnt > k, v16, below))
    return top


def _peer_select_kernel(s_ref, a2_ref, e2_ref, thr_ref, e1_ref):
    k = PEER_TOPK
    nh = PEER_HEADS
    half = lambda h, c: s_ref[(2 * h + c) * N_KEYS:(2 * h + c + 1) * N_KEYS, :]
    tops1 = [_top17(half(h, 0)) for h in range(nh)]
    tops2 = [_top17(half(h, 1)) for h in range(nh)]
    v1 = [jnp.concatenate([tops1[h][a] for h in range(nh)], axis=0) for a in range(k + 1)]
    v2 = [jnp.concatenate([tops2[h][a] for h in range(nh)], axis=0) for a in range(k + 1)]
    d1 = [v - v1[0] for v in v1]
    d2 = [v - v2[0] for v in v2]
    cands = [d1[a] + d2[b] for a, b in _CAND_PAIRS]
    pad = [jnp.full_like(cands[0], -jnp.inf)] * (_CAND_WIRES - len(cands))
    srt = _apply_network(cands + pad, _SELECT_16_17)
    tau = 0.5 * (srt[k - 1] + srt[k])
    x1 = [jnp.exp(d) for d in d1]
    x2 = [jnp.exp(d) for d in d2]
    z = jnp.zeros_like(tau)
    for (a, b), c in zip(_CAND_PAIRS, cands):
        z = z + jnp.where(c >= tau, x1[a] * x2[b], 0.0)
    rz = 1.0 / z
    ic = EXPERT_ROWS_PER_STEP
    for h in range(nh):
        a1 = half(h, 0) - v1[0][h:h + 1]
        a2 = half(h, 1) - v2[0][h:h + 1]
        thr = tau[h:h + 1] - a1
        e1 = jnp.exp(a1) * rz[h:h + 1]
        for c in range(N_KEYS // ic):
            thr_ref[c, h * ic:(h + 1) * ic, :] = thr[c * ic:(c + 1) * ic]
            e1_ref[c, h * ic:(h + 1) * ic, :] = e1[c * ic:(c + 1) * ic]
        a2_ref[h * N_KEYS:(h + 1) * N_KEYS, :] = a2
        e2_ref[h * N_KEYS:(h + 1) * N_KEYS, :] = jnp.exp(a2)


def _peer_select(s_t):
    nrow, tp = s_t.shape
    assert N_KEYS == 8 * PEER_TOPK
    tn = LANES
    hk = PEER_HEADS * N_KEYS
    ic = EXPERT_ROWS_PER_STEP
    flat = pl.BlockSpec((None, hk, tn), lambda t: (t, 0, 0))
    cube = pl.BlockSpec((None, N_KEYS // ic, PEER_HEADS * ic, tn), lambda t: (t, 0, 0, 0))
    return pl.pallas_call(
        _peer_select_kernel,
        grid=(tp // tn,),
        in_specs=[pl.BlockSpec((nrow, tn), lambda t: (0, t))],
        out_specs=(flat, flat, cube, cube),
        out_shape=(jax.ShapeDtypeStruct((tp // tn, hk, tn), F32),) * 2
        + (jax.ShapeDtypeStruct((tp // tn, N_KEYS // ic, PEER_HEADS * ic, tn), F32),) * 2,
        compiler_params=_cparams("arbitrary"),
        name="peer_select",
    )(s_t)


EXPERT_ROWS_PER_STEP = 8
PEER_TOKEN_TILE = 768


def _peer_dense_kernel(h_ref, u_ref, vt_ref, thr_ref, e1_ref, a2_ref, e2_ref, out_ref, gate_s, wg_s, acc_s):
    c = pl.program_id(1)

    @pl.when(c == 0)
    def _():
        acc_s[...] = jnp.zeros_like(acc_s)

    ic = EXPERT_ROWS_PER_STEP
    for ii in range(ic):
        es = slice(ii * N_KEYS, (ii + 1) * N_KEYS)
        for lt in range(h_ref.shape[1] // LANES):
            w = None
            for h in range(PEER_HEADS):
                hs = slice(h * N_KEYS, (h + 1) * N_KEYS)
                row = slice(h * ic + ii, h * ic + ii + 1)
                keep = a2_ref[lt, hs, :] >= thr_ref[lt, row, :]
                term = jnp.where(keep, e1_ref[lt, row, :] * e2_ref[lt, hs, :], 0.0)
                w = term if w is None else w + term
            gate_s[es, lt * LANES:(lt + 1) * LANES] = w
    halves = [slice(k * (ic // 2) * N_KEYS, (k + 1) * (ic // 2) * N_KEYS) for k in range(2)]
    acts = [_dot(u_ref[hf, :], h_ref[...]) for hf in halves]
    for k, hf in enumerate(halves):
        wg_s[hf, :] = (gate_s[hf, :] * jax.nn.gelu(acts[k])).astype(BF16)
    acc_s[...] += _dot(vt_ref[...], wg_s[...])

    @pl.when(c == pl.num_programs(1) - 1)
    def _():
        out_ref[...] = acc_s[...].T


def _peer_dense(h_t, u_bf, vt_bf, thr, e1, a2, e2, tm):
    d, tp = h_t.shape
    ne = u_bf.shape[0]
    ec = EXPERT_ROWS_PER_STEP * N_KEYS
    hk = PEER_HEADS * N_KEYS
    once = pl.Buffered(1)
    nlt = tm // LANES
    slab = pl.BlockSpec((nlt, hk, LANES), lambda t, c: (t, 0, 0), pipeline_mode=once)
    rows = pl.BlockSpec((nlt, None, PEER_HEADS * EXPERT_ROWS_PER_STEP, LANES), lambda t, c: (t, c, 0, 0))
    return pl.pallas_call(
        _peer_dense_kernel,
        grid=(tp // tm, ne // ec),
        in_specs=[pl.BlockSpec((d, tm), lambda t, c: (0, t), pipeline_mode=once),
                  pl.BlockSpec((ec, d), lambda t, c: (c, 0)), pl.BlockSpec((d, ec), lambda t, c: (0, c)),
                  rows, rows, slab, slab],
        out_specs=pl.BlockSpec((tm, d), lambda t, c: (t, 0), pipeline_mode=once),
        out_shape=jax.ShapeDtypeStruct((tp, d), F32),
        scratch_shapes=[pltpu.VMEM((ec, tm), F32), pltpu.VMEM((ec, tm), BF16), pltpu.VMEM((d, tm), F32)],
        compiler_params=_cparams("arbitrary", "arbitrary"),
        name="peer_dense",
    )(h_t, u_bf, vt_bf, thr, e1, a2, e2)


def _final_kernel(x1_ref, f_ref, ga2_ref, fg_ref, y_ref):
    x2 = x1_ref[...] + ga2_ref[...] * f_ref[...]
    y = x2 * lax.rsqrt(jnp.mean(x2 * x2, axis=-1, keepdims=True) + EPS)
    y_ref[...] = y * fg_ref[...]


def _final(x1, f, f_row0, ga2, fg, per_row):
    r, d = x1.shape
    tm = r if per_row else 256
    assert f_row0 % tm == 0
    row_spec = pl.BlockSpec((tm, d), lambda i: (i, 0))
    f_spec = pl.BlockSpec((tm, d), lambda i: (i + f_row0 // tm, 0))
    vec_spec = pl.BlockSpec((1, d), lambda i: (0, 0))
    return pl.pallas_call(
        _final_kernel,
        grid=(r // tm,),
        in_specs=[row_spec, f_spec, row_spec if per_row else vec_spec, vec_spec],
        out_specs=row_spec,
        out_shape=jax.ShapeDtypeStruct((r, d), F32),
        compiler_params=_cparams("arbitrary"),
        name="final_sample" if per_row else "final_prompt",
    )(x1, f, ga2, fg)


def _prep_w_in(w_in):
    d = w_in.shape[0]
    g0 = 3 * (d // 2) + D_ATTN + 6 * KV_W
    w_gate = jnp.pad(w_in[:, g0:], ((0, 0), (0, LANES - 3 * N_HEADS)))
    return w_in[:, :g0].astype(BF16), w_gate.astype(BF16)


def _prep_w_out(w_out):
    d = w_out.shape[0]
    dc = d // 2
    wa = w_out[dc:].reshape(N_KV, GROUP, HEAD_DIM, d).transpose(1, 0, 2, 3).reshape(D_ATTN, d)
    return jnp.concatenate([w_out[:dc], wa], axis=0).astype(BF16)


def kernel(x_prompt, x_sample, c_prompt, c_sample, cache_kv, state_win, state_conv, page_table, w_ada, b_ada,
           norm1_g, norm2_g, w_in, conv_w, conv_b, cmp_pe, cmp_w1, cmp_b1, cmp_w2, cmp_b2, w_out, peer_wq,
           peer_keys, peer_u, peer_v, final_g):
    depth = w_ada.shape[0]
    assert depth == 1
    bp, t, d = x_prompt.shape
    assert bp == 1 and d - d // 2 == D_ATTN
    bs, ts, _ = x_sample.shape
    assert ts == 4
    dc = d // 2
    n_pages = page_table.shape[1]
    past = n_pages * PAGE
    wkeep = state_win.shape[2]
    assert wkeep == WINDOW and t % KEY_TILE == 0 and n_pages % PAGES_PER_STEP == 0
    rs = bs * ts

    n_c = 1 + bs
    n_cp = -(-n_c // 8) * 8
    c_all = jnp.concatenate([c_prompt, c_sample, jnp.zeros((n_cp - n_c, d), F32)], axis=0)
    ada = _ada(c_all, w_ada[0], b_ada[0])
    sh1, sc1, ga1, sh2, sc2, ga2 = [ada[:, k * d:(k + 1) * d] for k in range(6)]
    pr = lambda a: a[0:1]
    sm = lambda a: jnp.repeat(a[1:1 + bs], ts, axis=0)

    w_main, w_gate = _prep_w_in(w_in[0])
    wo_perm = _prep_w_out(w_out[0])
    g1 = norm1_g[0].reshape(1, d)
    g2 = norm2_g[0].reshape(1, d)
    cw = conv_w[0]
    cb = conv_b[0].reshape(1, dc)
    cmpw = _prep_cmp_weights(cmp_pe[0], cmp_w1[0], cmp_b1[0], cmp_w2[0], cmp_b2[0])

    xp = x_prompt[0]
    cos_p, sin_p = _rope_tables(jnp.arange(t))
    convy_p, q_p, rows_p, win_p, kvb_p, gates_p, zc_p = _proj(
        xp, g1, pr(sc1), pr(sh1), w_main, w_gate, cos_p, sin_p, cw, cb)
    ident = jnp.arange(t // PAGE, dtype=jnp.int32).reshape(1, t // PAGE)
    kc_p, vc_p = _compress(rows_p.reshape(t // PAGE, PAGE, 4 * KV_W), ident, cmpw, False)
    attn_p = _nsa_prompt(q_p, gates_p, kc_p[0], vc_p[0], kvb_p)
    wq_bf = peer_wq[0].astype(BF16)
    keys_bf = peer_keys[0].reshape(2 * PEER_HEADS, N_KEYS, -1).astype(BF16)
    x1_p, h2_p, s_p = _outproj(xp, convy_p, attn_p, pr(ga1), g2, pr(sc2), pr(sh2), wo_perm, wq_bf, keys_bf, False)

    xs = x_sample.reshape(rs, d)
    pos_s = jnp.tile(past + jnp.arange(ts), bs)
    cos_s, sin_s = _rope_tables(pos_s)
    st = state_conv[0]
    a1 = jnp.repeat(st[:, 1], ts, axis=0)
    a2 = jnp.stack([st[:, 0], st[:, 1], st[:, 1], st[:, 1]], axis=1).reshape(rs, dc)
    convy_s, q_s, rows_s, win_s, kvb_s, gates_s, zc_s = _proj(
        xs, g1, sm(sc1), sm(sh1), w_main, w_gate, cos_s, sin_s, cw, cb, a1, a2)
    pool = cache_kv[0].transpose(0, 2, 3, 4, 1).reshape(cache_kv.shape[1], 4 * KV_W, PAGE)
    kc_s, vc_s = _compress(pool, page_table, cmpw, True)
    q5 = q_s.reshape(bs, ts, N_KV, GROUP, 1, HEAD_DIM).transpose(0, 3, 2, 1, 4, 5)
    qrep = jnp.broadcast_to(q5, (bs, GROUP, N_KV, ts, N_KV, HEAD_DIM)).reshape(bs, GROUP * N_KV * ts, KV_W)
    gts = gates_s[:, :3 * N_HEADS].reshape(bs, ts, N_KV, GROUP, 3).transpose(0, 4, 3, 2, 1)
    gx = jnp.broadcast_to(gts.reshape(bs, 3, GROUP * N_KV * ts, 1), (bs, 3, GROUP * N_KV * ts, KV_W))
    new8 = jnp.pad(kvb_s.reshape(bs, ts, 4 * KV_W), ((0, 0), (0, 8 - ts), (0, 0)))
    sw = state_win[0].transpose(0, 2, 3, 4, 1).reshape(bs, 2 * KV_W, wkeep)
    o_s = _nsa_sample(pool, page_table, qrep, gx, kc_s, vc_s, sw, new8)
    o6 = o_s.reshape(bs, GROUP, N_KV, ts, N_KV, HEAD_DIM)
    gi = jnp.arange(N_KV)
    attn_s = o6[:, :, gi, :, gi, :]
    attn_s = attn_s.transpose(1, 3, 2, 0, 4).reshape(rs, D_ATTN).astype(BF16)
    x1_s, h2_s, s_s = _outproj(xs, convy_s, attn_s, sm(ga1), g2, sm(sc2), sm(sh2), wo_perm, wq_bf, keys_bf, True)

    ntok = t + rs
    tm = PEER_TOKEN_TILE
    tp = -(-ntok // tm) * tm
    padt = lambda a: jnp.pad(jnp.concatenate(a, axis=0), ((0, tp - ntok), (0, 0))).T
    h_t = padt([h2_p, h2_s])
    s_t = padt([s_p, s_s])
    a2p, e2p, thr, e1p = _peer_select(s_t)
    u_bf = peer_u[0].astype(BF16)
    vt_bf = peer_v[0].astype(BF16).T
    f = _peer_dense(h_t, u_bf, vt_bf, thr, e1p, a2p, e2p, tm)
    fg = final_g.reshape(1, d)
    y_p = _final(x1_p, f, 0, pr(ga2), fg, False)
    y_s = _final(x1_s, f, t, sm(ga2), fg, True)

    wmin = min(WINDOW, t)
    win_ctx = jnp.concatenate([state_win[0], win_s.reshape(bs, ts, 2, N_KV, HEAD_DIM)], axis=1)[:, ts:]
    zc3 = zc_s.reshape(bs, ts, dc)
    return (y_p.reshape(1, t, d), y_s.reshape(bs, ts, d),
            rows_p.reshape(1, 1, t, 4, N_KV, HEAD_DIM), rows_s.reshape(1, bs, ts, 4, N_KV, HEAD_DIM),
            win_p[t - wmin:].reshape(1, 1, wmin, 2, N_KV, HEAD_DIM), win_ctx[None],
            zc_p[8 - (3 - 1):].reshape(1, 1, 2, dc), zc3[:, ts - 2:][None])
```
